```python
import jax, jax.numpy as jnp
from jax import lax
import numpy as np

D_MODEL = 1024
BATCH = 8
SEQ = 4096
DEPTH = 2

GRID_W = 64
CTX_LEN = 256
N_MIXERS = 2
HEAD_DIM = 64
NORM_EPS = 1e-6
ATTN_SCALE = HEAD_DIM ** -0.5
A_HEADS = 16
A_KV_HEADS = 4
A_GROUP = A_HEADS // A_KV_HEADS
A_WIDTH = A_HEADS * HEAD_DIM
A_KV_WIDTH = A_KV_HEADS * HEAD_DIM
A_IN = 2 * A_WIDTH + 2 * A_KV_WIDTH
ROPE_THETA = 10000.0
ROPE_AXIS_DIM = HEAD_DIM // 2
ROPE_HALF = ROPE_AXIS_DIM // 2
Q_BLOCK = 128
B_HEADS = 16
B_WIDTH = B_HEADS * HEAD_DIM
B_IN = 4 * B_WIDTH
WIN_R = 8
WIN_C = 16
N_A_LAYERS = (DEPTH + N_MIXERS - 1) // N_MIXERS
N_B_LAYERS = DEPTH // N_MIXERS

kernel_name = "hybrid_gqa_natten_prefix_dit"


def rms_norm(x, g):
    xf = x.astype(jnp.float32)
    y = xf * lax.rsqrt(jnp.mean(xf * xf, axis=-1, keepdims=True) + NORM_EPS)
    return (y * g.astype(jnp.float32)).astype(x.dtype)


def axial_rope_tables(t_len):
    pos = jnp.arange(t_len, dtype=jnp.int32)
    row = (pos // GRID_W).astype(jnp.float32)
    col = (pos % GRID_W).astype(jnp.float32)
    inv = ROPE_THETA ** (-jnp.arange(0, ROPE_AXIS_DIM, 2, dtype=jnp.float32) / ROPE_AXIS_DIM)
    ang = jnp.stack([row[:, None] * inv, col[:, None] * inv], axis=1)
    return jnp.cos(ang), jnp.sin(ang)


def apply_axial_rope(x, cos, sin):
    shp = x.shape
    xs = x.reshape(shp[0], shp[1], shp[2], 2, 2, ROPE_HALF)
    x1 = xs[..., 0, :]
    x2 = xs[..., 1, :]
    cc = cos.astype(x.dtype)[None, :, None]
    ss = sin.astype(x.dtype)[None, :, None]
    out = jnp.stack([x1 * cc - x2 * ss, x2 * cc + x1 * ss], axis=-2)
    return out.reshape(shp)


def gqa_axial_mixer(hx, hc, w_in, q_g, k_g, w_out, cos, sin, need_ctx_out):
    b, t, _ = hx.shape
    lc = hc.shape[1]
    px = hx @ w_in
    q = px[..., :A_WIDTH].reshape(b, t, A_HEADS, HEAD_DIM)
    k = px[..., A_WIDTH:A_WIDTH + A_KV_WIDTH].reshape(b, t, A_KV_HEADS, HEAD_DIM)
    v = px[..., A_WIDTH + A_KV_WIDTH:A_WIDTH + 2 * A_KV_WIDTH].reshape(b, t, A_KV_HEADS, HEAD_DIM)
    z = px[..., A_WIDTH + 2 * A_KV_WIDTH:]
    q = apply_axial_rope(rms_norm(q, q_g), cos, sin)
    k = apply_axial_rope(rms_norm(k, k_g), cos, sin)
    pkv = hc @ w_in[:, A_WIDTH:A_WIDTH + 2 * A_KV_WIDTH]
    kc = rms_norm(pkv[..., :A_KV_WIDTH].reshape(b, lc, A_KV_HEADS, HEAD_DIM), k_g)
    vc = pkv[..., A_KV_WIDTH:].reshape(b, lc, A_KV_HEADS, HEAD_DIM)
    k_all = jnp.concatenate([kc, k], axis=1)
    v_all = jnp.concatenate([vc, v], axis=1)
    n_blk = t // Q_BLOCK
    qb = q.reshape(b, n_blk, Q_BLOCK, A_KV_HEADS, A_GROUP, HEAD_DIM).transpose(1, 0, 2, 3, 4, 5)

    def block(qi):
        s = jnp.einsum('bqkgd,bskd->bkgqs', qi, k_all).astype(jnp.float32) * ATTN_SCALE
        p = jax.nn.softmax(s, axis=-1).astype(v_all.dtype)
        return jnp.einsum('bkgqs,bskd->bqkgd', p, v_all)

    o = lax.map(block, qb)
    o = o.transpose(1, 0, 2, 3, 4, 5).reshape(b, t, A_WIDTH)
    yx = (o * jax.nn.silu(z)) @ w_out
    if not need_ctx_out:
        return yx, None
    qc = rms_norm((hc @ w_in[:, :A_WIDTH]).reshape(b, lc, A_HEADS, HEAD_DIM), q_g)
    qc = qc.reshape(b, lc, A_KV_HEADS, A_GROUP, HEAD_DIM)
    zc = hc @ w_in[:, A_WIDTH + 2 * A_KV_WIDTH:]
    sc = jnp.einsum('bqkgd,bskd->bkgqs', qc, kc).astype(jnp.float32) * ATTN_SCALE
    pc = jax.nn.softmax(sc, axis=-1).astype(vc.dtype)
    oc = jnp.einsum('bkgqs,bskd->bqkgd', pc, vc).reshape(b, lc, A_WIDTH)
    yc = (oc * jax.nn.silu(zc)) @ w_out
    return yx, yc


def neighbourhood_mixer(hx, hc, w_in, rpb, w_out, need_ctx_out):
    b, t, _ = hx.shape
    lc = hc.shape[1]
    rows = t // GRID_W
    wr = min(WIN_R, rows)
    px = hx @ w_in
    q = px[..., :B_WIDTH].reshape(b, rows, GRID_W, B_HEADS, HEAD_DIM)
    k = px[..., B_WIDTH:2 * B_WIDTH].reshape(b, rows, GRID_W, B_HEADS, HEAD_DIM)
    v = px[..., 2 * B_WIDTH:3 * B_WIDTH].reshape(b, rows, GRID_W, B_HEADS, HEAD_DIM)
    z = px[..., 3 * B_WIDTH:]
    pkv = hc @ w_in[:, B_WIDTH:3 * B_WIDTH]
    kc = pkv[..., :B_WIDTH].reshape(b, lc, B_HEADS, HEAD_DIM)
    vc = pkv[..., B_WIDTH:].reshape(b, lc, B_HEADS, HEAD_DIM)
    qcol = np.arange(GRID_W)
    c0 = np.clip(qcol - WIN_C // 2, 0, GRID_W - WIN_C)
    col_idx = c0[:, None] + np.arange(WIN_C)[None, :]
    dc_idx = col_idx - qcol[:, None] + (WIN_C - 1)
    n_nb = wr * WIN_C

    def row_block(r):
        r0 = jnp.clip(r - wr // 2, 0, rows - wr)
        kb = lax.dynamic_slice_in_dim(k, r0, wr, axis=1)
        vb = lax.dynamic_slice_in_dim(v, r0, wr, axis=1)
        kw = kb[:, :, col_idx].transpose(0, 2, 1, 3, 4, 5).reshape(b, GRID_W, n_nb, B_HEADS, HEAD_DIM)
        vw = vb[:, :, col_idx].transpose(0, 2, 1, 3, 4, 5).reshape(b, GRID_W, n_nb, B_HEADS, HEAD_DIM)
        qr = lax.dynamic_index_in_dim(q, r, axis=1, keepdims=False)
        dr_idx = r0 + jnp.arange(wr) - r + (WIN_R - 1)
        bias = rpb[:, dr_idx][:, :, dc_idx]
        bias = bias.transpose(0, 2, 1, 3).reshape(B_HEADS, GRID_W, n_nb).astype(jnp.float32)
        s_nb = jnp.einsum('bqhd,bqkhd->bhqk', qr, kw).astype(jnp.float32) * ATTN_SCALE + bias
        s_cx = jnp.einsum('bqhd,bshd->bhqs', qr, kc).astype(jnp.float32) * ATTN_SCALE
        p = jax.nn.softmax(jnp.concatenate([s_nb, s_cx], axis=-1), axis=-1).astype(vw.dtype)
        return (jnp.einsum('bhqk,bqkhd->bqhd', p[..., :n_nb], vw)
                + jnp.einsum('bhqs,bshd->bqhd', p[..., n_nb:], vc))

    o = lax.map(row_block, jnp.arange(rows))
    o = o.transpose(1, 0, 2, 3, 4).reshape(b, t, B_WIDTH)
    yx = (o * jax.nn.silu(z)) @ w_out
    if not need_ctx_out:
        return yx, None
    qc = (hc @ w_in[:, :B_WIDTH]).reshape(b, lc, B_HEADS, HEAD_DIM)
    zc = hc @ w_in[:, 3 * B_WIDTH:]
    sc = jnp.einsum('bqhd,bshd->bhqs', qc, kc).astype(jnp.float32) * ATTN_SCALE
    pc = jax.nn.softmax(sc, axis=-1).astype(vc.dtype)
    oc = jnp.einsum('bhqs,bshd->bqhd', pc, vc).reshape(b, lc, B_WIDTH)
    yc = (oc * jax.nn.silu(zc)) @ w_out
    return yx, yc


def setup_inputs(seed: int = 0) -> dict:
    key = jax.random.key(seed)
    ks = jax.random.split(key, 20)
    f32 = jnp.float32
    d = D_MODEL
    nrm = lambda k, shp, s: (jax.random.normal(k, shp, f32) * s)
    return {
        "x": nrm(ks[0], (BATCH, SEQ, d), 1.0),
        "c": nrm(ks[1], (BATCH, d), 1.0),
        "ctx": nrm(ks[2], (BATCH, CTX_LEN, d), 1.0),
        "c_ctx": nrm(ks[3], (d,), 1.0),
        "norm_g": 1.0 + nrm(ks[4], (DEPTH, d), 0.02),
        "w_mod": nrm(ks[5], (DEPTH, d, 3 * d), 0.5 * d ** -0.5),
        "b_mod": nrm(ks[6], (DEPTH, 3 * d), 0.01),
        "a_w_in": nrm(ks[7], (N_A_LAYERS, d, A_IN), d ** -0.5),
        "a_q_norm_g": 1.0 + nrm(ks[8], (N_A_LAYERS, HEAD_DIM), 0.02),
        "a_k_norm_g": 1.0 + nrm(ks[9], (N_A_LAYERS, HEAD_DIM), 0.02),
        "a_w_out": nrm(ks[10], (N_A_LAYERS, A_WIDTH, d), A_WIDTH ** -0.5),
        "b_w_in": nrm(ks[11], (N_B_LAYERS, d, B_IN), d ** -0.5),
        "b_rpb": nrm(ks[12], (N_B_LAYERS, B_HEADS, 2 * WIN_R - 1, 2 * WIN_C - 1), 0.02),
        "b_w_out": nrm(ks[13], (N_B_LAYERS, B_WIDTH, d), B_WIDTH ** -0.5),
        "final_norm_g": 1.0 + nrm(ks[14], (d,), 0.02),
    }


def reference(x, c, ctx, c_ctx, norm_g, w_mod, b_mod, a_w_in, a_q_norm_g, a_k_norm_g, a_w_out,
              b_w_in, b_rpb, b_w_out, final_norm_g):
    t = x.shape[1]
    cos, sin = axial_rope_tables(t)
    silu_c = jax.nn.silu(c)
    silu_cc = jax.nn.silu(c_ctx)
    ia = 0
    ib = 0
    for i in range(DEPTH):
        last = i == DEPTH - 1
        mod_x = (silu_c @ w_mod[i] + b_mod[i])[:, None, :]
        mod_c = silu_cc @ w_mod[i] + b_mod[i]
        sh_x, sc_x, g_x = jnp.split(mod_x, 3, axis=-1)
        sh_c, sc_c, g_c = jnp.split(mod_c, 3, axis=-1)
        hx = rms_norm(x, norm_g[i]) * (1.0 + sc_x) + sh_x
        hc = rms_norm(ctx, norm_g[i]) * (1.0 + sc_c) + sh_c
        if i % N_MIXERS == 0:
            yx, yc = gqa_axial_mixer(hx, hc, a_w_in[ia], a_q_norm_g[ia], a_k_norm_g[ia], a_w_out[ia],
                                     cos, sin, not last)
            ia += 1
        else:
            yx, yc = neighbourhood_mixer(hx, hc, b_w_in[ib], b_rpb[ib], b_w_out[ib], not last)
            ib += 1
        x = x + g_x * yx
        if not last:
            ctx = ctx + g_c * yc
    return rms_norm(x, final_norm_g)
```

```python
import functools

import jax
import jax.numpy as jnp
import numpy as np
from jax import lax
from jax.experimental import pallas as pl
from jax.experimental.pallas import tpu as pltpu

F32 = jnp.float32
BF16 = jnp.bfloat16

LANES = 128
VMEM_LIMIT = 56 * 1024 * 1024

D_MODEL = 1024
GRID_W = 64
HEAD_DIM = 64
NORM_EPS = 1e-6
ATTN_SCALE = HEAD_DIM ** -0.5
A_HEADS = 16
A_KV_HEADS = 4
A_WIDTH = A_HEADS * HEAD_DIM
A_KV_WIDTH = A_KV_HEADS * HEAD_DIM
ROPE_THETA = 10000.0
ROPE_AXIS_DIM = HEAD_DIM // 2
ROPE_HALF = ROPE_AXIS_DIM // 2
B_HEADS = 16
B_WIDTH = B_HEADS * HEAD_DIM
WIN_R = 8
WIN_C = 16
N_PAIRS = D_MODEL // LANES
NEG_BIG = -1e30


def _cparams(sem):
    return pltpu.CompilerParams(dimension_semantics=sem, vmem_limit_bytes=VMEM_LIMIT)


def _mod_kernel(c_ref, w_ref, b_ref, o_ref):
    c = c_ref[...]
    s = c * jax.nn.sigmoid(c)
    o_ref[0] = jnp.dot(s, w_ref[0], precision=lax.Precision.HIGHEST,
                       preferred_element_type=F32) + b_ref[0]


def _modulation(c_rows, w_mod, b_mod):
    depth, d, n = w_mod.shape
    rows = c_rows.shape[0]
    tn = 1024
    return pl.pallas_call(
        _mod_kernel,
        grid=(depth, n // tn),
        in_specs=[
            pl.BlockSpec((rows, d), lambda l, j: (0, 0)),
            pl.BlockSpec((1, d, tn), lambda l, j: (l, 0, j)),
            pl.BlockSpec((1, 1, tn), lambda l, j: (l, 0, j)),
        ],
        out_specs=pl.BlockSpec((1, rows, tn), lambda l, j: (l, 0, j)),
        out_shape=jax.ShapeDtypeStruct((depth, rows, n), F32),
        compiler_params=_cparams(("arbitrary", "arbitrary")),
        name="adaln_mod",
    )(c_rows, w_mod, b_mod.reshape(depth, 1, n))


def _adaln(x, ng, sc, sh):
    ms = jnp.mean(x * x, axis=-1, keepdims=True)
    y = x * lax.rsqrt(ms + NORM_EPS) * ng
    return y * (1.0 + sc) + sh


def _silu(z):
    return z * jax.nn.sigmoid(z)


def _head_norm_rope(blk, gain, bd, cos_t, sin_up, sin_dn):
    sq = blk * blk
    hi = sq.astype(BF16)
    lo = (sq - hi.astype(F32)).astype(BF16)
    ssum = (jnp.dot(hi, bd, preferred_element_type=F32) + jnp.dot(lo, bd, preferred_element_type=F32))
    n = blk * lax.rsqrt(ssum * (1.0 / HEAD_DIM) + NORM_EPS) * gain
    up = pltpu.roll(n, LANES - ROPE_HALF, 1)
    dn = pltpu.roll(n, ROPE_HALF, 1)
    return n * cos_t + up * sin_up + dn * sin_dn


def _inproj_a_kernel(x_ref, ng_ref, sc_ref, sh_ref, w_ref, qg_ref, kg_ref, bd_ref, cos_ref, sup_ref, sdn_ref,
                     q_ref, k_ref, v_ref, g_ref):
    h = _adaln(x_ref[0], ng_ref[...], sc_ref[0, 0], sh_ref[0, 0])
    p = jnp.dot(h.astype(BF16), w_ref[...], preferred_element_type=F32)
    bd = bd_ref[...]
    cos_t, sup, sdn = cos_ref[...], sup_ref[...], sdn_ref[...]
    qg, kg = qg_ref[...], kg_ref[...]
    for j in range(A_WIDTH // LANES):
        blk = p[:, j * LANES:(j + 1) * LANES]
        q_ref[0, :, j * LANES:(j + 1) * LANES] = _head_norm_rope(blk, qg, bd, cos_t, sup, sdn).astype(BF16)
    for j in range(A_KV_WIDTH // LANES):
        blk = p[:, A_WIDTH + j * LANES:A_WIDTH + (j + 1) * LANES]
        k_ref[0, :, j * LANES:(j + 1) * LANES] = _head_norm_rope(blk, kg, bd, cos_t, sup, sdn).astype(BF16)
    v_ref[0] = p[:, A_WIDTH + A_KV_WIDTH:A_WIDTH + 2 * A_KV_WIDTH].astype(BF16)
    g_ref[0] = _silu(p[:, A_WIDTH + 2 * A_KV_WIDTH:]).astype(BF16)


def _inproj_a(x, ng, mods, w, qg, kg, bd, cos_t, sup, sdn, tm):
    b, r, d = x.shape
    n = w.shape[1]
    shared = mods.shape[0] == 1
    mod_idx = (lambda i, j, c: (0, c, 0, 0)) if shared else (lambda i, j, c: (i, c, 0, 0))
    row = lambda i, j: (i, j, 0)
    const2 = lambda i, j: (0, 0)
    return pl.pallas_call(
        _inproj_a_kernel,
        grid=(b, r // tm),
        in_specs=[
            pl.BlockSpec((1, tm, d), row),
            pl.BlockSpec((1, d), const2),
            pl.BlockSpec((1, 1, 1, d), lambda i, j: mod_idx(i, j, 1)),
            pl.BlockSpec((1, 1, 1, d), lambda i, j: mod_idx(i, j, 0)),
            pl.BlockSpec((d, n), const2),
            pl.BlockSpec((1, LANES), const2),
            pl.BlockSpec((1, LANES), const2),
            pl.BlockSpec((LANES, LANES), const2),
            pl.BlockSpec((tm, LANES), lambda i, j: (j, 0)),
            pl.BlockSpec((tm, LANES), lambda i, j: (j, 0)),
            pl.BlockSpec((tm, LANES), lambda i, j: (j, 0)),
        ],
        out_specs=[
            pl.BlockSpec((1, tm, A_WIDTH), row),
            pl.BlockSpec((1, tm, A_KV_WIDTH), row),
            pl.BlockSpec((1, tm, A_KV_WIDTH), row),
            pl.BlockSpec((1, tm, A_WIDTH), row),
        ],
        out_shape=[
            jax.ShapeDtypeStruct((b, r, A_WIDTH), BF16),
            jax.ShapeDtypeStruct((b, r, A_KV_WIDTH), BF16),
            jax.ShapeDtypeStruct((b, r, A_KV_WIDTH), BF16),
            jax.ShapeDtypeStruct((b, r, A_WIDTH), BF16),
        ],
        compiler_params=_cparams(("parallel", "parallel")),
        name="inproj_a",
    )(x, ng, mods, mods, w, qg, kg, bd, cos_t, sup, sdn)


def _pair_stack(q2, half):
    qf = q2.astype(F32)
    rolled = pltpu.roll(qf, HEAD_DIM, 1)
    lane_half = lax.broadcasted_iota(jnp.int32, qf.shape, 1) // HEAD_DIM
    keep = lane_half == half
    first = half == 0
    qa = jnp.where(keep, jnp.where(first, qf, rolled), 0.0)
    qb = jnp.where(keep, jnp.where(first, rolled, qf), 0.0)
    return jnp.concatenate([qa, qb], axis=0).astype(BF16)


def _pair_unstack(o, half, tq):
    oa, ob = o[:tq], o[tq:]
    lane_half = lax.broadcasted_iota(jnp.int32, oa.shape, 1) // HEAD_DIM
    first = half == 0
    left = jnp.where(first, oa, pltpu.roll(oa, HEAD_DIM, 1))
    right = jnp.where(first, pltpu.roll(ob, HEAD_DIM, 1), ob)
    return jnp.where(lane_half == 0, left, right)


def _softmax_chunk(a, kblk, vblk, carry):
    m, l, acc = carry
    s = lax.dot_general(a, kblk, (((1,), (1,)), ((), ())), preferred_element_type=F32)
    m_new = jnp.maximum(m, jnp.max(s, axis=1, keepdims=True))
    alpha = jnp.exp(m - m_new)
    p = jnp.exp(s - m_new)
    l = alpha * l + jnp.sum(p, axis=1, keepdims=True)
    acc = alpha * acc + jnp.dot(p.astype(BF16), vblk, preferred_element_type=F32)
    return m_new, l, acc


def _attn_a_kernel(*refs, tq, tk, n_x_chunks):
    if n_x_chunks:
        q_ref, kx_ref, vx_ref, kc_ref, vc_ref, gate_ref, x_ref, gx_ref, w_ref, out_ref, o_scr = refs
    else:
        q_ref, kc_ref, vc_ref, gate_ref, x_ref, gx_ref, w_ref, out_ref, o_scr = refs
    hp = pl.program_id(2)
    half = (hp // 2) % 2
    a = _pair_stack(q_ref[0], half)
    rows = 2 * tq
    carry = (jnp.full((rows, 1), NEG_BIG, F32), jnp.zeros((rows, 1), F32), jnp.zeros((rows, LANES), F32))
    carry = _softmax_chunk(a, kc_ref[0], vc_ref[0], carry)
    if n_x_chunks:
        def body(c, carry):
            start = pl.multiple_of(c * tk, tk)
            return _softmax_chunk(a, kx_ref[0, pl.ds(start, tk), :], vx_ref[0, pl.ds(start, tk), :], carry)
        carry = lax.fori_loop(0, n_x_chunks, body, carry)
    _, l, acc = carry
    o_scr[hp] = _pair_unstack(acc / l, half, tq)

    @pl.when(hp == N_PAIRS - 1)
    def _():
        o_full = jnp.concatenate([o_scr[j] for j in range(N_PAIRS)], axis=1)
        u = (o_full * gate_ref[0].astype(F32)).astype(BF16)
        y = jnp.dot(u, w_ref[...], preferred_element_type=F32)
        out_ref[0] = x_ref[0] + gx_ref[0, 0] * y


def _attn_a(q, kx, vx, kc, vc, gate, x, mods, w_out, tq, tk):
    b, r, d = x.shape
    has_x = kx is not None
    shared = mods.shape[0] == 1
    qrow = lambda i, j, h: (i, j, 0)
    kvp = lambda i, j, h: (i, 0, h // 4)
    in_specs = [pl.BlockSpec((1, tq, LANES), lambda i, j, h: (i, j, h))]
    args = [q]
    if has_x:
        t = kx.shape[1]
        in_specs += [pl.BlockSpec((1, t, LANES), kvp), pl.BlockSpec((1, t, LANES), kvp)]
        args += [kx, vx]
        n_x_chunks = t // tk
    else:
        n_x_chunks = 0
    lc = kc.shape[1]
    in_specs += [
        pl.BlockSpec((1, lc, LANES), kvp),
        pl.BlockSpec((1, lc, LANES), kvp),
        pl.BlockSpec((1, tq, d), qrow),
        pl.BlockSpec((1, tq, d), qrow),
        pl.BlockSpec((1, 1, 1, d), (lambda i, j, h: (0, 2, 0, 0)) if shared else (lambda i, j, h: (i, 2, 0, 0))),
        pl.BlockSpec((d, d), lambda i, j, h: (0, 0)),
    ]
    args += [kc, vc, gate, x, mods, w_out]
    return pl.pallas_call(
        functools.partial(_attn_a_kernel, tq=tq, tk=tk, n_x_chunks=n_x_chunks),
        grid=(b, r // tq, N_PAIRS),
        in_specs=in_specs,
        out_specs=pl.BlockSpec((1, tq, d), qrow),
        out_shape=jax.ShapeDtypeStruct((b, r, d), F32),
        scratch_shapes=[pltpu.VMEM((N_PAIRS, tq, LANES), F32)],
        compiler_params=_cparams(("parallel", "parallel", "arbitrary")),
        name="attn_a_x" if has_x else "attn_a_ctx",
    )(*args)


def _inproj_b_kernel(x_ref, ng_ref, sc_ref, sh_ref, w_ref, *out_refs, kv_only):
    h = _adaln(x_ref[0], ng_ref[...], sc_ref[0, 0], sh_ref[0, 0])
    p = jnp.dot(h.astype(BF16), w_ref[...], preferred_element_type=F32)
    if kv_only:
        k_ref, v_ref = out_refs
        k_ref[0] = p[:, :B_WIDTH].astype(BF16)
        v_ref[0] = p[:, B_WIDTH:].astype(BF16)
    else:
        q_ref, k_ref, v_ref, g_ref = out_refs
        q_ref[0] = (p[:, :B_WIDTH] * ATTN_SCALE).astype(BF16)
        k_ref[0] = p[:, B_WIDTH:2 * B_WIDTH].astype(BF16)
        v_ref[0] = p[:, 2 * B_WIDTH:3 * B_WIDTH].astype(BF16)
        g_ref[0] = _silu(p[:, 3 * B_WIDTH:]).astype(BF16)


def _inproj_b(x, ng, mods, w, tm, kv_only):
    b, r, d = x.shape
    n = w.shape[1]
    shared = mods.shape[0] == 1
    mod_idx = (lambda i, c: (0, c, 0, 0)) if shared else (lambda i, c: (i, c, 0, 0))
    row = lambda i, j: (i, j, 0)
    n_out = 2 if kv_only else 4
    return pl.pallas_call(
        functools.partial(_inproj_b_kernel, kv_only=kv_only),
        grid=(b, r // tm),
        in_specs=[
            pl.BlockSpec((1, tm, d), row),
            pl.BlockSpec((1, d), lambda i, j: (0, 0)),
            pl.BlockSpec((1, 1, 1, d), lambda i, j: mod_idx(i, 1)),
            pl.BlockSpec((1, 1, 1, d), lambda i, j: mod_idx(i, 0)),
            pl.BlockSpec((d, n), lambda i, j: (0, 0)),
        ],
        out_specs=[pl.BlockSpec((1, tm, B_WIDTH), row)] * n_out,
        out_shape=[jax.ShapeDtypeStruct((b, r, B_WIDTH), BF16)] * n_out,
        compiler_params=_cparams(("parallel", "parallel")),
        name="inproj_b_ctx" if kv_only else "inproj_b_x",
    )(x, ng, mods, mods, w)


def _natten_kernel(q_ref, k_ref, v_ref, kc_ref, vc_ref, tab_ref, o_ref, *, rows, unroll):
    win = WIN_R * GRID_W
    kc = kc_ref[0]
    vc = vc_ref[0]
    nt = (((1,), (1,)), ((), ()))
    lane_half = lax.broadcasted_iota(jnp.int32, (GRID_W, LANES), 1) // HEAD_DIM

    def body(r, carry):
        r0 = jnp.clip(r - WIN_R // 2, 0, rows - WIN_R)
        delta = r - r0
        qs = pl.multiple_of(r * GRID_W, GRID_W)
        ks = pl.multiple_of(r0 * GRID_W, GRID_W)
        qr = q_ref[0, pl.ds(qs, GRID_W), :].astype(F32)
        a = jnp.concatenate([jnp.where(lane_half == 0, qr, 0.0), jnp.where(lane_half == 0, 0.0, qr)],
                            axis=0).astype(BF16)
        kw = k_ref[0, pl.ds(ks, win), :]
        vw = v_ref[0, pl.ds(ks, win), :]
        s = lax.dot_general(a, kw, nt, preferred_element_type=F32) + tab_ref[0, delta]
        sc = lax.dot_general(a, kc, nt, preferred_element_type=F32)
        m = jnp.maximum(jnp.max(s, axis=1, keepdims=True), jnp.max(sc, axis=1, keepdims=True))
        p = jnp.exp(s - m)
        pc = jnp.exp(sc - m)
        l = jnp.sum(p, axis=1, keepdims=True) + jnp.sum(pc, axis=1, keepdims=True)
        o = (jnp.dot(p.astype(BF16), vw, preferred_element_type=F32)
             + jnp.dot(pc.astype(BF16), vc, preferred_element_type=F32)) / l
        o_ref[0, pl.ds(qs, GRID_W), :] = jnp.where(lane_half == 0, o[:GRID_W], o[GRID_W:]).astype(BF16)
        return carry

    lax.fori_loop(0, rows, body, 0, unroll=unroll)


def _natten(q, k, v, kc, vc, tab):
    b, t, d = q.shape
    lc = kc.shape[1]
    rows = t // GRID_W
    blk = lambda i, h: (i, 0, h)
    return pl.pallas_call(
        functools.partial(_natten_kernel, rows=rows, unroll=2),
        grid=(b, N_PAIRS),
        in_specs=[
            pl.BlockSpec((1, t, LANES), blk),
            pl.BlockSpec((1, t, LANES), blk),
            pl.BlockSpec((1, t, LANES), blk),
            pl.BlockSpec((1, lc, LANES), blk),
            pl.BlockSpec((1, lc, LANES), blk),
            pl.BlockSpec((1, WIN_R, 2 * GRID_W, WIN_R * GRID_W), lambda i, h: (h, 0, 0, 0)),
        ],
        out_specs=pl.BlockSpec((1, t, LANES), blk),
        out_shape=jax.ShapeDtypeStruct((b, t, d), BF16),
        compiler_params=_cparams(("parallel", "parallel")),
        name="natten_b",
    )(q, k, v, kc, vc, tab)


def _outproj_final_kernel(o_ref, gate_ref, x_ref, gx_ref, w_ref, fg_ref, out_ref):
    u = (o_ref[0].astype(F32) * gate_ref[0].astype(F32)).astype(BF16)
    y = jnp.dot(u, w_ref[...], preferred_element_type=F32)
    x2 = x_ref[0] + gx_ref[0, 0] * y
    ms = jnp.mean(x2 * x2, axis=-1, keepdims=True)
    out_ref[0] = x2 * lax.rsqrt(ms + NORM_EPS) * fg_ref[...]


def _outproj_final(o, gate, x, mods, w_out, fg, tm):
    b, t, d = x.shape
    row = lambda i, j: (i, j, 0)
    return pl.pallas_call(
        _outproj_final_kernel,
        grid=(b, t // tm),
        in_specs=[
            pl.BlockSpec((1, tm, d), row),
            pl.BlockSpec((1, tm, d), row),
            pl.BlockSpec((1, tm, d), row),
            pl.BlockSpec((1, 1, 1, d), lambda i, j: (i, 2, 0, 0)),
            pl.BlockSpec((d, d), lambda i, j: (0, 0)),
            pl.BlockSpec((1, d), lambda i, j: (0, 0)),
        ],
        out_specs=pl.BlockSpec((1, tm, d), row),
        out_shape=jax.ShapeDtypeStruct((b, t, d), F32),
        compiler_params=_cparams(("parallel", "parallel")),
        name="outproj_final",
    )(o, gate, x, mods, w_out, fg)


def _rope_tables(t_len):
    pos = jnp.arange(t_len, dtype=jnp.int32)
    row = (pos // GRID_W).astype(F32)
    col = (pos % GRID_W).astype(F32)
    inv = ROPE_THETA ** (-jnp.arange(0, ROPE_AXIS_DIM, 2, dtype=F32) / ROPE_AXIS_DIM)
    ang_r = row[:, None] * inv
    ang_c = col[:, None] * inv
    zero = jnp.zeros_like(ang_r)
    cos_h = jnp.concatenate([jnp.cos(ang_r), jnp.cos(ang_r), jnp.cos(ang_c), jnp.cos(ang_c)], axis=1)
    sup_h = jnp.concatenate([-jnp.sin(ang_r), zero, -jnp.sin(ang_c), zero], axis=1)
    sdn_h = jnp.concatenate([zero, jnp.sin(ang_r), zero, jnp.sin(ang_c)], axis=1)
    two = lambda a: jnp.concatenate([a, a], axis=1)
    return two(cos_h), two(sup_h), two(sdn_h)


def _natten_tables(rpb):
    qcol = np.arange(GRID_W)
    c0 = np.clip(qcol - WIN_C // 2, 0, GRID_W - WIN_C)
    kcol = np.arange(GRID_W)
    valid = (kcol[None, :] >= c0[:, None]) & (kcol[None, :] < c0[:, None] + WIN_C)
    dc = np.clip(kcol[None, :] - qcol[:, None] + (WIN_C - 1), 0, 2 * WIN_C - 2)
    delta = np.arange(WIN_R)
    krow = np.arange(WIN_R)
    dr = krow[None, :] - delta[:, None] + (WIN_R - 1)
    flat = (dr[:, None, :, None] * (2 * WIN_C - 1) + dc[None, :, None, :])
    flat = flat.reshape(WIN_R, GRID_W, WIN_R * GRID_W).astype(np.int32)
    mask = np.broadcast_to(valid[None, :, None, :], (WIN_R, GRID_W, WIN_R, GRID_W)).reshape(flat.shape)
    h = rpb.shape[0]
    g = jnp.take(rpb.reshape(h, -1).astype(F32), jnp.asarray(flat.reshape(-1)), axis=1)
    g = jnp.where(jnp.asarray(mask.reshape(-1))[None, :], g, NEG_BIG)
    g = g.reshape(h // 2, 2, WIN_R, GRID_W, WIN_R * GRID_W)
    return g.transpose(0, 2, 1, 3, 4).reshape(h // 2, WIN_R, 2 * GRID_W, WIN_R * GRID_W)


def kernel(x, c, ctx, c_ctx, norm_g, w_mod, b_mod, a_w_in, a_q_norm_g, a_k_norm_g, a_w_out,
           b_w_in, b_rpb, b_w_out, final_norm_g):
    bsz, t, d = x.shape
    lc = ctx.shape[1]

    n_rows = ((bsz + 1 + 7) // 8) * 8
    c_rows = jnp.zeros((n_rows, d), F32).at[:bsz].set(c).at[bsz].set(c_ctx)
    mods = _modulation(c_rows, w_mod, b_mod).reshape(w_mod.shape[0], n_rows, 3, 1, d)

    cos_t, sup, sdn = _rope_tables(t)
    one_t = jnp.ones((lc, LANES), F32)
    zero_t = jnp.zeros((lc, LANES), F32)
    head_block = np.kron(np.eye(LANES // HEAD_DIM), np.ones((HEAD_DIM, HEAD_DIM)))
    bd = jnp.asarray(head_block, BF16)
    two = lambda g: jnp.concatenate([g, g]).reshape(1, LANES).astype(F32)

    mx, mc = mods[0, :bsz], mods[0, bsz:bsz + 1]
    ng = norm_g[0].reshape(1, d)
    w_in = a_w_in[0].astype(BF16)
    qg, kg = two(a_q_norm_g[0]) * ATTN_SCALE, two(a_k_norm_g[0])
    q, k, v, gate = _inproj_a(x, ng, mx, w_in, qg, kg, bd, cos_t, sup, sdn, tm=256)
    qc, kc, vc, gate_c = _inproj_a(ctx, ng, mc, w_in, qg, kg, bd, one_t, zero_t, zero_t, tm=lc)
    w_out = a_w_out[0].astype(BF16)
    x1 = _attn_a(q, k, v, kc, vc, gate, x, mx, w_out, tq=512, tk=512)
    ctx1 = _attn_a(qc, None, None, kc, vc, gate_c, ctx, mc, w_out, tq=lc, tk=lc)

    mx, mc = mods[1, :bsz], mods[1, bsz:bsz + 1]
    ng = norm_g[1].reshape(1, d)
    w_in = b_w_in[0].astype(BF16)
    q, k, v, gate = _inproj_b(x1, ng, mx, w_in, tm=256, kv_only=False)
    kc, vc = _inproj_b(ctx1, ng, mc, w_in[:, B_WIDTH:3 * B_WIDTH], tm=lc, kv_only=True)
    o = _natten(q, k, v, kc, vc, _natten_tables(b_rpb[0]))
    return _outproj_final(o, gate, x1, mx, b_w_out[0].astype(BF16), final_norm_g.reshape(1, d), tm=512)
```

```python
import functools

import jax
import jax.numpy as jnp
import numpy as np
from jax import lax
from jax.experimental import pallas as pl
from jax.experimental.pallas import tpu as pltpu

F32 = jnp.float32
BF16 = jnp.bfloat16

LANES = 128
VMEM_LIMIT = 56 * 1024 * 1024

D_MODEL = 1024
GRID_W = 64
HEAD_DIM = 64
NORM_EPS = 1e-6
ATTN_SCALE = HEAD_DIM ** -0.5
A_HEADS = 16
A_KV_HEADS = 4
A_WIDTH = A_HEADS * HEAD_DIM
A_KV_WIDTH = A_KV_HEADS * HEAD_DIM
ROPE_THETA = 10000.0
ROPE_AXIS_DIM = HEAD_DIM // 2
ROPE_HALF = ROPE_AXIS_DIM // 2
B_HEADS = 16
B_WIDTH = B_HEADS * HEAD_DIM
WIN_R = 8
WIN_C = 16
N_PAIRS = D_MODEL // LANES
NEG_BIG = -1e30
MAX_ROWS = 128
EXP_ROWS = 64
SUM_ROWS = 16
N_SCORE_BUFS = 3
LOG2E = 1.4426950408889634


def _cparams(sem, flags=None):
    return pltpu.CompilerParams(dimension_semantics=sem, vmem_limit_bytes=VMEM_LIMIT, flags=flags)


def _mod_kernel(c_ref, w_ref, b_ref, o_ref):
    c = c_ref[...]
    s = c * jax.nn.sigmoid(c)
    o_ref[0] = jnp.dot(s, w_ref[0], precision=lax.Precision.HIGHEST,
                       preferred_element_type=F32) + b_ref[0]


def _modulation(c_rows, w_mod, b_mod):
    depth, d, n = w_mod.shape
    rows = c_rows.shape[0]
    tn = 1024
    return pl.pallas_call(
        _mod_kernel,
        grid=(depth, n // tn),
        in_specs=[
            pl.BlockSpec((rows, d), lambda l, j: (0, 0)),
            pl.BlockSpec((1, d, tn), lambda l, j: (l, 0, j)),
            pl.BlockSpec((1, 1, tn), lambda l, j: (l, 0, j)),
        ],
        out_specs=pl.BlockSpec((1, rows, tn), lambda l, j: (l, 0, j)),
        out_shape=jax.ShapeDtypeStruct((depth, rows, n), F32),
        compiler_params=_cparams(("arbitrary", "arbitrary")),
        name="adaln_mod",
    )(c_rows, w_mod, b_mod.reshape(depth, 1, n))


def _adaln(x, ng, sc, sh):
    ms = jnp.mean(x * x, axis=-1, keepdims=True)
    y = x * lax.rsqrt(ms + NORM_EPS) * ng
    return y * (1.0 + sc) + sh


def _silu(z):
    return z * jax.nn.sigmoid(z)


def _head_norm_rope(blk, gain, bd, cos_t, sin_up, sin_dn):
    sq = blk * blk
    hi = sq.astype(BF16)
    lo = (sq - hi.astype(F32)).astype(BF16)
    ssum = (jnp.dot(hi, bd, preferred_element_type=F32) + jnp.dot(lo, bd, preferred_element_type=F32))
    n = blk * lax.rsqrt(ssum * (1.0 / HEAD_DIM) + NORM_EPS) * gain
    up = pltpu.roll(n, LANES - ROPE_HALF, 1)
    dn = pltpu.roll(n, ROPE_HALF, 1)
    return n * cos_t + up * sin_up + dn * sin_dn


def _inproj_a_kernel(x_ref, ng_ref, sc_ref, sh_ref, w_ref, qg_ref, kg_ref, bd_ref, cos_ref, sup_ref, sdn_ref,
                     q_ref, k_ref, v_ref, g_ref):
    h = _adaln(x_ref[0], ng_ref[...], sc_ref[0, 0], sh_ref[0, 0])
    p = jnp.dot(h.astype(BF16), w_ref[...], preferred_element_type=F32)
    bd = bd_ref[...]
    cos_t, sup, sdn = cos_ref[...], sup_ref[...], sdn_ref[...]
    qg, kg = qg_ref[...], kg_ref[...]
    for j in range(A_WIDTH // LANES):
        blk = p[:, j * LANES:(j + 1) * LANES]
        q_ref[0, :, j * LANES:(j + 1) * LANES] = _head_norm_rope(blk, qg, bd, cos_t, sup, sdn).astype(BF16)
    for j in range(A_KV_WIDTH // LANES):
        blk = p[:, A_WIDTH + j * LANES:A_WIDTH + (j + 1) * LANES]
        k_ref[0, :, j * LANES:(j + 1) * LANES] = _head_norm_rope(blk, kg, bd, cos_t, sup, sdn).astype(BF16)
    v_ref[0, 0] = p[:, A_WIDTH + A_KV_WIDTH:A_WIDTH + 2 * A_KV_WIDTH].T.astype(BF16)
    g_ref[0] = _silu(p[:, A_WIDTH + 2 * A_KV_WIDTH:]).astype(BF16)


def _inproj_a(x, ng, mods, w, qg, kg, bd, cos_t, sup, sdn, tm):
    b, r, d = x.shape
    n = w.shape[1]
    shared = mods.shape[0] == 1
    mod_idx = (lambda i, j, c: (0, c, 0, 0)) if shared else (lambda i, j, c: (i, c, 0, 0))
    row = lambda i, j: (i, j, 0)
    const2 = lambda i, j: (0, 0)
    return pl.pallas_call(
        _inproj_a_kernel,
        grid=(b, r // tm),
        in_specs=[
            pl.BlockSpec((1, tm, d), row),
            pl.BlockSpec((1, d), const2),
            pl.BlockSpec((1, 1, 1, d), lambda i, j: mod_idx(i, j, 1)),
            pl.BlockSpec((1, 1, 1, d), lambda i, j: mod_idx(i, j, 0)),
            pl.BlockSpec((d, n), const2),
            pl.BlockSpec((1, LANES), const2),
            pl.BlockSpec((1, LANES), const2),
            pl.BlockSpec((LANES, LANES), const2),
            pl.BlockSpec((tm, LANES), lambda i, j: (j, 0)),
            pl.BlockSpec((tm, LANES), lambda i, j: (j, 0)),
            pl.BlockSpec((tm, LANES), lambda i, j: (j, 0)),
        ],
        out_specs=[
            pl.BlockSpec((1, tm, A_WIDTH), row),
            pl.BlockSpec((1, tm, A_KV_WIDTH), row),
            pl.BlockSpec((1, 1, A_KV_WIDTH, tm), lambda i, j: (i, j, 0, 0)),
            pl.BlockSpec((1, tm, A_WIDTH), row),
        ],
        out_shape=[
            jax.ShapeDtypeStruct((b, r, A_WIDTH), BF16),
            jax.ShapeDtypeStruct((b, r, A_KV_WIDTH), BF16),
            jax.ShapeDtypeStruct((b, r // tm, A_KV_WIDTH, tm), BF16),
            jax.ShapeDtypeStruct((b, r, A_WIDTH), BF16),
        ],
        compiler_params=_cparams(("parallel", "parallel")),
        name="inproj_a",
    )(x, ng, mods, mods, w, qg, kg, bd, cos_t, sup, sdn)


def _stack_qt(q2, half):
    qt = q2.astype(F32).T
    zero = jnp.zeros((HEAD_DIM, qt.shape[1]), F32)
    first = half == 0

    def place(h_t):
        return jnp.where(first, jnp.concatenate([h_t, zero], axis=0), jnp.concatenate([zero, h_t], axis=0))

    return jnp.concatenate([place(qt[:HEAD_DIM]), place(qt[HEAD_DIM:])], axis=1).astype(BF16)


def _softmax_cols(s_scr, p_scr, nk, m):
    m_out, a_out = [], []
    for j in range(s_scr.shape[1] // LANES):
        sl = slice(j * LANES, (j + 1) * LANES)
        mx = s_scr[0:MAX_ROWS, sl]
        for r in range(1, nk // MAX_ROWS):
            mx = jnp.maximum(mx, s_scr[r * MAX_ROWS:(r + 1) * MAX_ROWS, sl])
        m_new = jnp.maximum(m[:, sl], jnp.max(mx, axis=0, keepdims=True))
        for r in range(nk // EXP_ROWS):
            rows = slice(r * EXP_ROWS, (r + 1) * EXP_ROWS)
            p_scr[rows, sl] = jnp.exp2(s_scr[rows, sl] - m_new).astype(BF16)
        m_out.append(m_new)
        a_out.append(jnp.exp2(m[:, sl] - m_new))
    cat = lambda xs: jnp.concatenate(xs, axis=1)
    return cat(m_out), cat(a_out)


def _attn_a_kernel(*refs, tq, tk, n_x_chunks):
    n_in = 10 if n_x_chunks else 8
    if n_x_chunks:
        q_ref, kx_ref, vxt_ref, kc_ref, vct_ref, gate_ref, x_ref, gx_ref, w_ref, out_ref = refs[:n_in]
    else:
        q_ref, kc_ref, vct_ref, gate_ref, x_ref, gx_ref, w_ref, out_ref = refs[:n_in]
    o_scr = refs[n_in]
    s_bufs = refs[n_in + 1:n_in + 1 + N_SCORE_BUFS]
    p_bufs = refs[n_in + 1 + N_SCORE_BUFS:]
    hp = pl.program_id(2)
    kv_head = hp // 2
    vrow = pl.multiple_of(kv_head * HEAD_DIM, HEAD_DIM)
    a_t = _stack_qt(q_ref[0], kv_head % 2)
    cols = 2 * tq
    lc = kc_ref.shape[1]

    n_chunks = 1 + n_x_chunks
    chunk_rows = lambda i: lc if i == 0 else tk
    k_blk = lambda i: kc_ref[0] if i == 0 else kx_ref[0, (i - 1) * tk:i * tk, :]
    v_blk = lambda i: (vct_ref[0, 0, pl.ds(vrow, HEAD_DIM), :] if i == 0
                       else vxt_ref[0, i - 1, pl.ds(vrow, HEAD_DIM), :])

    def scores(i):
        nk = chunk_rows(i)
        s_bufs[i % N_SCORE_BUFS][0:nk, :] = jnp.dot(k_blk(i), a_t, preferred_element_type=F32)

    def consume(i, carry):
        m, acc = carry
        nk = chunk_rows(i)
        p_scr = p_bufs[i % len(p_bufs)]
        m, alpha = _softmax_cols(s_bufs[i % N_SCORE_BUFS], p_scr, nk, m)
        v_ext = jnp.concatenate([v_blk(i), jnp.ones((SUM_ROWS, nk), BF16)], axis=0)
        acc = alpha * acc + jnp.dot(v_ext, p_scr[0:nk, :], preferred_element_type=F32)
        return m, acc

    carry = (jnp.full((1, cols), NEG_BIG, F32), jnp.zeros((HEAD_DIM + SUM_ROWS, cols), F32))
    scores(0)
    for i in range(n_chunks):
        if i + 1 < n_chunks:
            scores(i + 1)
        carry = consume(i, carry)
    _, acc = carry
    o_t = acc[:HEAD_DIM] / acc[HEAD_DIM:HEAD_DIM + 1]
    o_scr[hp] = jnp.concatenate([o_t[:, :tq], o_t[:, tq:]], axis=0).T

    @pl.when(hp == N_PAIRS - 1)
    def _():
        o_full = jnp.concatenate([o_scr[j] for j in range(N_PAIRS)], axis=1)
        u = (o_full * gate_ref[0].astype(F32)).astype(BF16)
        y = jnp.dot(u, w_ref[...], preferred_element_type=F32)
        out_ref[0] = x_ref[0] + gx_ref[0, 0] * y


def _attn_a(q, kx, vxt, kc, vct, gate, x, mods, w_out, tq):
    b, r, d = x.shape
    has_x = kx is not None
    shared = mods.shape[0] == 1
    qrow = lambda i, j, h: (i, j, 0)
    kvp = lambda i, j, h: (i, 0, h // 4)
    whole = lambda i, j, h: (i, 0, 0, 0)
    in_specs = [pl.BlockSpec((1, tq, LANES), lambda i, j, h: (i, j, h))]
    args = [q]
    lc = kc.shape[1]
    tk = lc
    n_x_chunks = 0
    if has_x:
        t = kx.shape[1]
        n_x_chunks, tk = vxt.shape[1], vxt.shape[3]
        in_specs += [pl.BlockSpec((1, t, LANES), kvp), pl.BlockSpec((1,) + vxt.shape[1:], whole)]
        args += [kx, vxt]
    in_specs += [
        pl.BlockSpec((1, lc, LANES), kvp),
        pl.BlockSpec((1,) + vct.shape[1:], whole),
        pl.BlockSpec((1, tq, d), qrow),
        pl.BlockSpec((1, tq, d), qrow),
        pl.BlockSpec((1, 1, 1, d), (lambda i, j, h: (0, 2, 0, 0)) if shared else (lambda i, j, h: (i, 2, 0, 0))),
        pl.BlockSpec((d, d), lambda i, j, h: (0, 0)),
    ]
    args += [kc, vct, gate, x, mods, w_out]
    return pl.pallas_call(
        functools.partial(_attn_a_kernel, tq=tq, tk=tk, n_x_chunks=n_x_chunks),
        grid=(b, r // tq, N_PAIRS),
        in_specs=in_specs,
        out_specs=pl.BlockSpec((1, tq, d), qrow),
        out_shape=jax.ShapeDtypeStruct((b, r, d), F32),
        scratch_shapes=[
            pltpu.VMEM((N_PAIRS, tq, LANES), F32),
            *[pltpu.VMEM((max(tk, lc), 2 * tq), F32)] * N_SCORE_BUFS,
            *[pltpu.VMEM((max(tk, lc), 2 * tq), BF16)] * 2,
        ],
        compiler_params=_cparams(("parallel", "parallel", "arbitrary")),
        name="attn_a_x" if has_x else "attn_a_ctx",
    )(*args)


def _inproj_b_kernel(x_ref, ng_ref, sc_ref, sh_ref, w_ref, *out_refs, kv_only):
    h = _adaln(x_ref[0], ng_ref[...], sc_ref[0, 0], sh_ref[0, 0])
    p = jnp.dot(h.astype(BF16), w_ref[...], preferred_element_type=F32)
    if kv_only:
        k_ref, v_ref = out_refs
        k_ref[0] = p[:, :B_WIDTH].astype(BF16)
        v_ref[0] = p[:, B_WIDTH:].astype(BF16)
    else:
        q_ref, k_ref, v_ref, g_ref = out_refs
        q_ref[0] = (p[:, :B_WIDTH] * ATTN_SCALE).astype(BF16)
        k_ref[0] = p[:, B_WIDTH:2 * B_WIDTH].astype(BF16)
        v_ref[0] = p[:, 2 * B_WIDTH:3 * B_WIDTH].astype(BF16)
        g_ref[0] = _silu(p[:, 3 * B_WIDTH:]).astype(BF16)


def _inproj_b(x, ng, mods, w, tm, kv_only):
    b, r, d = x.shape
    n = w.shape[1]
    shared = mods.shape[0] == 1
    mod_idx = (lambda i, c: (0, c, 0, 0)) if shared else (lambda i, c: (i, c, 0, 0))
    row = lambda i, j: (i, j, 0)
    n_out = 2 if kv_only else 4
    return pl.pallas_call(
        functools.partial(_inproj_b_kernel, kv_only=kv_only),
        grid=(b, r // tm),
        in_specs=[
            pl.BlockSpec((1, tm, d), row),
            pl.BlockSpec((1, d), lambda i, j: (0, 0)),
            pl.BlockSpec((1, 1, 1, d), lambda i, j: mod_idx(i, 1)),
            pl.BlockSpec((1, 1, 1, d), lambda i, j: mod_idx(i, 0)),
            pl.BlockSpec((d, n), lambda i, j: (0, 0)),
        ],
        out_specs=[pl.BlockSpec((1, tm, B_WIDTH), row)] * n_out,
        out_shape=[jax.ShapeDtypeStruct((b, r, B_WIDTH), BF16)] * n_out,
        compiler_params=_cparams(("parallel", "parallel")),
        name="inproj_b_ctx" if kv_only else "inproj_b_x",
    )(x, ng, mods, mods, w)


def _natten_kernel(q_ref, k_ref, v_ref, kc_ref, vc_ref, tab_ref, o_ref, *, rows, unroll):
    win = WIN_R * GRID_W
    kc = kc_ref[0]
    vc = vc_ref[0]
    nt = (((1,), (1,)), ((), ()))
    lane_half = lax.broadcasted_iota(jnp.int32, (GRID_W, LANES), 1) // HEAD_DIM

    def body(r, carry):
        r0 = jnp.clip(r - WIN_R // 2, 0, rows - WIN_R)
        delta = r - r0
        qs = pl.multiple_of(r * GRID_W, GRID_W)
        ks = pl.multiple_of(r0 * GRID_W, GRID_W)
        qr = q_ref[0, pl.ds(qs, GRID_W), :].astype(F32)
        a = jnp.concatenate([jnp.where(lane_half == 0, qr, 0.0), jnp.where(lane_half == 0, 0.0, qr)],
                            axis=0).astype(BF16)
        kw = k_ref[0, pl.ds(ks, win), :]
        vw = v_ref[0, pl.ds(ks, win), :]
        s = lax.dot_general(a, kw, nt, preferred_element_type=F32) + tab_ref[0, delta]
        sc = lax.dot_general(a, kc, nt, preferred_element_type=F32)
        m = jnp.maximum(jnp.max(s, axis=1, keepdims=True), jnp.max(sc, axis=1, keepdims=True))
        p = jnp.exp(s - m)
        pc = jnp.exp(sc - m)
        l = jnp.sum(p, axis=1, keepdims=True) + jnp.sum(pc, axis=1, keepdims=True)
        o = (jnp.dot(p.astype(BF16), vw, preferred_element_type=F32)
             + jnp.dot(pc.astype(BF16), vc, preferred_element_type=F32)) / l
        o_ref[0, pl.ds(qs, GRID_W), :] = jnp.where(lane_half == 0, o[:GRID_W], o[GRID_W:]).astype(BF16)
        return carry

    lax.fori_loop(0, rows, body, 0, unroll=unroll)


def _natten(q, k, v, kc, vc, tab):
    b, t, d = q.shape
    lc = kc.shape[1]
    rows = t // GRID_W
    blk = lambda i, h: (i, 0, h)
    return pl.pallas_call(
        functools.partial(_natten_kernel, rows=rows, unroll=2),
        grid=(b, N_PAIRS),
        in_specs=[
            pl.BlockSpec((1, t, LANES), blk),
            pl.BlockSpec((1, t, LANES), blk),
            pl.BlockSpec((1, t, LANES), blk),
            pl.BlockSpec((1, lc, LANES), blk),
            pl.BlockSpec((1, lc, LANES), blk),
            pl.BlockSpec((1, WIN_R, 2 * GRID_W, WIN_R * GRID_W), lambda i, h: (h, 0, 0, 0)),
        ],
        out_specs=pl.BlockSpec((1, t, LANES), blk),
        out_shape=jax.ShapeDtypeStruct((b, t, d), BF16),
        compiler_params=_cparams(("parallel", "parallel")),
        name="natten_b",
    )(q, k, v, kc, vc, tab)


def _outproj_final_kernel(o_ref, gate_ref, x_ref, gx_ref, w_ref, fg_ref, out_ref):
    u = (o_ref[0].astype(F32) * gate_ref[0].astype(F32)).astype(BF16)
    y = jnp.dot(u, w_ref[...], preferred_element_type=F32)
    x2 = x_ref[0] + gx_ref[0, 0] * y
    ms = jnp.mean(x2 * x2, axis=-1, keepdims=True)
    out_ref[0] = x2 * lax.rsqrt(ms + NORM_EPS) * fg_ref[...]


def _outproj_final(o, gate, x, mods, w_out, fg, tm):
    b, t, d = x.shape
    row = lambda i, j: (i, j, 0)
    return pl.pallas_call(
        _outproj_final_kernel,
        grid=(b, t // tm),
        in_specs=[
            pl.BlockSpec((1, tm, d), row),
            pl.BlockSpec((1, tm, d), row),
            pl.BlockSpec((1, tm, d), row),
            pl.BlockSpec((1, 1, 1, d), lambda i, j: (i, 2, 0, 0)),
            pl.BlockSpec((d, d), lambda i, j: (0, 0)),
            pl.BlockSpec((1, d), lambda i, j: (0, 0)),
        ],
        out_specs=pl.BlockSpec((1, tm, d), row),
        out_shape=jax.ShapeDtypeStruct((b, t, d), F32),
        compiler_params=_cparams(("parallel", "parallel")),
        name="outproj_final",
    )(o, gate, x, mods, w_out, fg)


def _rope_tables(t_len):
    pos = jnp.arange(t_len, dtype=jnp.int32)
    row = (pos // GRID_W).astype(F32)
    col = (pos % GRID_W).astype(F32)
    inv = ROPE_THETA ** (-jnp.arange(0, ROPE_AXIS_DIM, 2, dtype=F32) / ROPE_AXIS_DIM)
    ang_r = row[:, None] * inv
    ang_c = col[:, None] * inv
    zero = jnp.zeros_like(ang_r)
    cos_h = jnp.concatenate([jnp.cos(ang_r), jnp.cos(ang_r), jnp.cos(ang_c), jnp.cos(ang_c)], axis=1)
    sup_h = jnp.concatenate([-jnp.sin(ang_r), zero, -jnp.sin(ang_c), zero], axis=1)
    sdn_h = jnp.concatenate([zero, jnp.sin(ang_r), zero, jnp.sin(ang_c)], axis=1)
    two = lambda a: jnp.concatenate([a, a], axis=1)
    return two(cos_h), two(sup_h), two(sdn_h)


def _natten_tables(rpb):
    h = rpb.shape[0]
    qcol = np.arange(GRID_W)
    c0 = np.clip(qcol - WIN_C // 2, 0, GRID_W - WIN_C)
    kcol = np.arange(GRID_W)
    valid = (kcol[None, :] >= c0[:, None]) & (kcol[None, :] < c0[:, None] + WIN_C)
    pad = GRID_W - WIN_C
    padded = jnp.pad(rpb.astype(F32), ((0, 0), (0, 0), (pad, pad)))
    toep = jnp.stack([padded[:, :, GRID_W - 1 - j:2 * GRID_W - 1 - j] for j in range(GRID_W)], axis=2)
    toep = jnp.where(jnp.asarray(valid)[None, None], toep, NEG_BIG)
    per_delta = jnp.stack([toep[:, WIN_R - 1 - dl:2 * WIN_R - 1 - dl] for dl in range(WIN_R)], axis=1)
    tab = per_delta.reshape(h // 2, 2, WIN_R, WIN_R, GRID_W, GRID_W).transpose(0, 2, 1, 4, 3, 5)
    return tab.reshape(h // 2, WIN_R, 2 * GRID_W, WIN_R * GRID_W)


def kernel(x, c, ctx, c_ctx, norm_g, w_mod, b_mod, a_w_in, a_q_norm_g, a_k_norm_g, a_w_out,
           b_w_in, b_rpb, b_w_out, final_norm_g):
    bsz, t, d = x.shape
    lc = ctx.shape[1]

    n_rows = ((bsz + 1 + 7) // 8) * 8
    c_rows = jnp.zeros((n_rows, d), F32).at[:bsz].set(c).at[bsz].set(c_ctx)
    mods = _modulation(c_rows, w_mod, b_mod).reshape(w_mod.shape[0], n_rows, 3, 1, d)

    cos_t, sup, sdn = _rope_tables(t)
    one_t = jnp.ones((lc, LANES), F32)
    zero_t = jnp.zeros((lc, LANES), F32)
    head_block = np.kron(np.eye(LANES // HEAD_DIM), np.ones((HEAD_DIM, HEAD_DIM)))
    bd = jnp.asarray(head_block, BF16)
    two = lambda g: jnp.concatenate([g, g]).reshape(1, LANES).astype(F32)

    mx, mc = mods[0, :bsz], mods[0, bsz:bsz + 1]
    ng = norm_g[0].reshape(1, d)
    w_in = a_w_in[0].astype(BF16)
    qg, kg = two(a_q_norm_g[0]) * (ATTN_SCALE * LOG2E), two(a_k_norm_g[0])
    q, k, vt, gate = _inproj_a(x, ng, mx, w_in, qg, kg, bd, cos_t, sup, sdn, tm=512)
    qc, kc, vct, gate_c = _inproj_a(ctx, ng, mc, w_in, qg, kg, bd, one_t, zero_t, zero_t, tm=lc)
    w_out = a_w_out[0].astype(BF16)
    x1 = _attn_a(q, k, vt, kc, vct, gate, x, mx, w_out, tq=512)
    ctx1 = _attn_a(qc, None, None, kc, vct, gate_c, ctx, mc, w_out, tq=lc)

    mx, mc = mods[1, :bsz], mods[1, bsz:bsz + 1]
    ng = norm_g[1].reshape(1, d)
    w_in = b_w_in[0].astype(BF16)
    q, k, v, gate = _inproj_b(x1, ng, mx, w_in, tm=256, kv_only=False)
    kc, vc = _inproj_b(ctx1, ng, mc, w_in[:, B_WIDTH:3 * B_WIDTH], tm=lc, kv_only=True)
    o = _natten(q, k, v, kc, vc, _natten_tables(b_rpb[0]))
    return _outproj_final(o, gate, x1, mx, b_w_out[0].astype(BF16), final_norm_g.reshape(1, d), tm=512)
```

```python
import functools

import jax
import jax.numpy as jnp
import numpy as np
from jax import lax
from jax.experimental import pallas as pl
from jax.experimental.pallas import tpu as pltpu

F32 = jnp.float32
BF16 = jnp.bfloat16

LANES = 128
VMEM_LIMIT = 56 * 1024 * 1024

D_MODEL = 1024
GRID_W = 64
HEAD_DIM = 64
NORM_EPS = 1e-6
ATTN_SCALE = HEAD_DIM ** -0.5
A_HEADS = 16
A_KV_HEADS = 4
A_WIDTH = A_HEADS * HEAD_DIM
A_KV_WIDTH = A_KV_HEADS * HEAD_DIM
ROPE_THETA = 10000.0
ROPE_AXIS_DIM = HEAD_DIM // 2
ROPE_HALF = ROPE_AXIS_DIM // 2
B_HEADS = 16
B_WIDTH = B_HEADS * HEAD_DIM
WIN_R = 8
WIN_C = 16
N_PAIRS = D_MODEL // LANES
NEG_BIG = -1e30
EXP_ROWS = 64
SUM_ROWS = 16
NAT_ROWS = 4
NAT_CHUNK = NAT_ROWS * GRID_W
NAT_SPAN = 3
NAT_KEYS = NAT_SPAN * NAT_CHUNK
N_SCORE_BUFS = 2
LOG2E = 1.4426950408889634


def _cparams(sem, flags=None):
    return pltpu.CompilerParams(dimension_semantics=sem, vmem_limit_bytes=VMEM_LIMIT, flags=flags)


def _mod_kernel(c_ref, w_ref, b_ref, o_ref):
    c = c_ref[...]
    s = c * jax.nn.sigmoid(c)
    o_ref[0] = jnp.dot(s, w_ref[0], precision=lax.Precision.HIGHEST,
                       preferred_element_type=F32) + b_ref[0]


def _modulation(c_rows, w_mod, b_mod):
    depth, d, n = w_mod.shape
    rows = c_rows.shape[0]
    tn = 1024
    return pl.pallas_call(
        _mod_kernel,
        grid=(depth, n // tn),
        in_specs=[
            pl.BlockSpec((rows, d), lambda l, j: (0, 0)),
            pl.BlockSpec((1, d, tn), lambda l, j: (l, 0, j)),
            pl.BlockSpec((1, 1, tn), lambda l, j: (l, 0, j)),
        ],
        out_specs=pl.BlockSpec((1, rows, tn), lambda l, j: (l, 0, j)),
        out_shape=jax.ShapeDtypeStruct((depth, rows, n), F32),
        compiler_params=_cparams(("arbitrary", "arbitrary")),
        name="adaln_mod",
    )(c_rows, w_mod, b_mod.reshape(depth, 1, n))


def _adaln(x, ng, sc, sh):
    ms = jnp.mean(x * x, axis=-1, keepdims=True)
    y = x * lax.rsqrt(ms + NORM_EPS) * ng
    return y * (1.0 + sc) + sh


def _silu(z):
    return z * jax.nn.sigmoid(z)


def _head_norm_rope(blk, gain, bd, cos_t, sin_up, sin_dn):
    sq = blk * blk
    hi = sq.astype(BF16)
    lo = (sq - hi.astype(F32)).astype(BF16)
    ssum = (jnp.dot(hi, bd, preferred_element_type=F32) + jnp.dot(lo, bd, preferred_element_type=F32))
    n = blk * lax.rsqrt(ssum * (1.0 / HEAD_DIM) + NORM_EPS) * gain
    up = pltpu.roll(n, LANES - ROPE_HALF, 1)
    dn = pltpu.roll(n, ROPE_HALF, 1)
    return n * cos_t + up * sin_up + dn * sin_dn


def _inproj_a_kernel(x_ref, ng_ref, sc_ref, sh_ref, w_ref, qg_ref, kg_ref, bd_ref, cos_ref, sup_ref, sdn_ref,
                     q_ref, k_ref, v_ref, g_ref):
    h = _adaln(x_ref[0], ng_ref[...], sc_ref[0, 0], sh_ref[0, 0])
    p = jnp.dot(h.astype(BF16), w_ref[...], preferred_element_type=F32)
    bd = bd_ref[...]
    cos_t, sup, sdn = cos_ref[...], sup_ref[...], sdn_ref[...]
    qg, kg = qg_ref[...], kg_ref[...]
    for j in range(A_WIDTH // LANES):
        blk = p[:, j * LANES:(j + 1) * LANES]
        q_ref[0, :, j * LANES:(j + 1) * LANES] = _head_norm_rope(blk, qg, bd, cos_t, sup, sdn).astype(BF16)
    for j in range(A_KV_WIDTH // LANES):
        blk = p[:, A_WIDTH + j * LANES:A_WIDTH + (j + 1) * LANES]
        k_ref[0, :, j * LANES:(j + 1) * LANES] = _head_norm_rope(blk, kg, bd, cos_t, sup, sdn).astype(BF16)
    v_ref[0, 0] = p[:, A_WIDTH + A_KV_WIDTH:A_WIDTH + 2 * A_KV_WIDTH].T.astype(BF16)
    g_ref[0] = _silu(p[:, A_WIDTH + 2 * A_KV_WIDTH:]).astype(BF16)


def _inproj_a(x, ng, mods, w, qg, kg, bd, cos_t, sup, sdn, tm):
    b, r, d = x.shape
    n = w.shape[1]
    shared = mods.shape[0] == 1
    mod_idx = (lambda i, j, c: (0, c, 0, 0)) if shared else (lambda i, j, c: (i, c, 0, 0))
    row = lambda i, j: (i, j, 0)
    const2 = lambda i, j: (0, 0)
    return pl.pallas_call(
        _inproj_a_kernel,
        grid=(b, r // tm),
        in_specs=[
            pl.BlockSpec((1, tm, d), row),
            pl.BlockSpec((1, d), const2),
            pl.BlockSpec((1, 1, 1, d), lambda i, j: mod_idx(i, j, 1)),
            pl.BlockSpec((1, 1, 1, d), lambda i, j: mod_idx(i, j, 0)),
            pl.BlockSpec((d, n), const2),
            pl.BlockSpec((1, LANES), const2),
            pl.BlockSpec((1, LANES), const2),
            pl.BlockSpec((LANES, LANES), const2),
            pl.BlockSpec((tm, LANES), lambda i, j: (j, 0)),
            pl.BlockSpec((tm, LANES), lambda i, j: (j, 0)),
            pl.BlockSpec((tm, LANES), lambda i, j: (j, 0)),
        ],
        out_specs=[
            pl.BlockSpec((1, tm, A_WIDTH), row),
            pl.BlockSpec((1, tm, A_KV_WIDTH), row),
            pl.BlockSpec((1, 1, A_KV_WIDTH, tm), lambda i, j: (i, j, 0, 0)),
            pl.BlockSpec((1, tm, A_WIDTH), row),
        ],
        out_shape=[
            jax.ShapeDtypeStruct((b, r, A_WIDTH), BF16),
            jax.ShapeDtypeStruct((b, r, A_KV_WIDTH), BF16),
            jax.ShapeDtypeStruct((b, r // tm, A_KV_WIDTH, tm), BF16),
            jax.ShapeDtypeStruct((b, r, A_WIDTH), BF16),
        ],
        compiler_params=_cparams(("parallel", "parallel")),
        name="inproj_a",
    )(x, ng, mods, mods, w, qg, kg, bd, cos_t, sup, sdn)


def _stack_qt(q2, half):
    qt = q2.astype(F32).T
    zero = jnp.zeros((HEAD_DIM, qt.shape[1]), F32)
    first = half == 0

    def place(h_t):
        return jnp.where(first, jnp.concatenate([h_t, zero], axis=0), jnp.concatenate([zero, h_t], axis=0))

    return jnp.concatenate([place(qt[:HEAD_DIM]), place(qt[HEAD_DIM:])], axis=1).astype(BF16)


def _softmax_cols(s_scr, p_scr, nk, m, chunk_max, row0):
    m_new = jnp.maximum(m, chunk_max)
    for j in range(s_scr.shape[1] // LANES):
        sl = slice(j * LANES, (j + 1) * LANES)
        for r in range(nk // EXP_ROWS):
            rows = slice(r * EXP_ROWS, (r + 1) * EXP_ROWS)
            src_rows = pl.ds(pl.multiple_of(row0 + r * EXP_ROWS, EXP_ROWS), EXP_ROWS)
            p_scr[rows, sl] = jnp.exp2(s_scr[src_rows, sl] - m_new[:, sl]).astype(BF16)
    return m_new, jnp.exp2(m - m_new)


def _attn_a_kernel(*refs, tq, tk, n_x_chunks):
    n_in = 10 if n_x_chunks else 8
    if n_x_chunks:
        q_ref, kx_ref, vxt_ref, kc_ref, vct_ref, gate_ref, x_ref, gx_ref, w_ref, out_ref = refs[:n_in]
    else:
        q_ref, kc_ref, vct_ref, gate_ref, x_ref, gx_ref, w_ref, out_ref = refs[:n_in]
    o_scr = refs[n_in]
    s_bufs = refs[n_in + 1:n_in + 1 + N_SCORE_BUFS]
    p_bufs = refs[n_in + 1 + N_SCORE_BUFS:]
    hp = pl.program_id(2)
    row0 = hp // N_PAIRS
    kv_head = hp // 2
    vrow = pl.multiple_of(kv_head * HEAD_DIM, HEAD_DIM)
    a_t = _stack_qt(q_ref[0], kv_head % 2)
    cols = 2 * tq
    lc = kc_ref.shape[1]

    n_chunks = 1 + n_x_chunks
    chunk_rows = lambda i: lc if i == 0 else tk
    k_blk = lambda i: kc_ref[0] if i == 0 else kx_ref[0, (i - 1) * tk:i * tk, :]
    v_blk = lambda i: (vct_ref[0, 0, pl.ds(vrow, HEAD_DIM), :] if i == 0
                       else vxt_ref[0, i - 1, pl.ds(vrow, HEAD_DIM), :])

    def scores(i):
        nk = chunk_rows(i)
        s = jnp.dot(k_blk(i), a_t, preferred_element_type=F32)
        s_bufs[i % N_SCORE_BUFS][0:nk, :] = s
        return jnp.max(s, axis=0, keepdims=True)

    def consume(i, carry):
        m, acc = carry
        nk = chunk_rows(i)
        p_scr = p_bufs[i % len(p_bufs)]
        m, alpha = _softmax_cols(s_bufs[i % N_SCORE_BUFS], p_scr, nk, m, cmax[i], row0)
        v_ext = jnp.concatenate([v_blk(i), jnp.ones((SUM_ROWS, nk), BF16)], axis=0)
        acc = alpha * acc + jnp.dot(v_ext, p_scr[0:nk, :], preferred_element_type=F32)
        return m, acc

    carry = (jnp.full((1, cols), NEG_BIG, F32), jnp.zeros((HEAD_DIM + SUM_ROWS, cols), F32))
    cmax = {0: scores(0)}
    for i in range(n_chunks):
        if i + 1 < n_chunks:
            cmax[i + 1] = scores(i + 1)
        carry = consume(i, carry)
    _, acc = carry
    o_t = acc[:HEAD_DIM] / acc[HEAD_DIM:HEAD_DIM + 1]
    o_scr[hp] = jnp.concatenate([o_t[:, :tq], o_t[:, tq:]], axis=0).T

    @pl.when(hp == N_PAIRS - 1)
    def _():
        o_full = jnp.concatenate([o_scr[j] for j in range(N_PAIRS)], axis=1)
        u = (o_full * gate_ref[0].astype(F32)).astype(BF16)
        y = jnp.dot(u, w_ref[...], preferred_element_type=F32)
        out_ref[0] = x_ref[0] + gx_ref[0, 0] * y


def _attn_a(q, kx, vxt, kc, vct, gate, x, mods, w_out, tq):
    b, r, d = x.shape
    has_x = kx is not None
    shared = mods.shape[0] == 1
    qrow = lambda i, j, h: (i, j, 0)
    kvp = lambda i, j, h: (i, 0, h // 4)
    whole = lambda i, j, h: (i, 0, 0, 0)
    in_specs = [pl.BlockSpec((1, tq, LANES), lambda i, j, h: (i, j, h))]
    args = [q]
    lc = kc.shape[1]
    tk = lc
    n_x_chunks = 0
    if has_x:
        t = kx.shape[1]
        n_x_chunks, tk = vxt.shape[1], vxt.shape[3]
        in_specs += [pl.BlockSpec((1, t, LANES), kvp), pl.BlockSpec((1,) + vxt.shape[1:], whole)]
        args += [kx, vxt]
    in_specs += [
        pl.BlockSpec((1, lc, LANES), kvp),
        pl.BlockSpec((1,) + vct.shape[1:], whole),
        pl.BlockSpec((1, tq, d), qrow),
        pl.BlockSpec((1, tq, d), qrow),
        pl.BlockSpec((1, 1, 1, d), (lambda i, j, h: (0, 2, 0, 0)) if shared else (lambda i, j, h: (i, 2, 0, 0))),
        pl.BlockSpec((d, d), lambda i, j, h: (0, 0)),
    ]
    args += [kc, vct, gate, x, mods, w_out]
    return pl.pallas_call(
        functools.partial(_attn_a_kernel, tq=tq, tk=tk, n_x_chunks=n_x_chunks),
        grid=(b, r // tq, N_PAIRS),
        in_specs=in_specs,
        out_specs=pl.BlockSpec((1, tq, d), qrow),
        out_shape=jax.ShapeDtypeStruct((b, r, d), F32),
        scratch_shapes=[
            pltpu.VMEM((N_PAIRS, tq, LANES), F32),
            *[pltpu.VMEM((max(tk, lc), 2 * tq), F32)] * N_SCORE_BUFS,
            *[pltpu.VMEM((max(tk, lc), 2 * tq), BF16)] * 2,
        ],
        compiler_params=_cparams(("parallel", "parallel", "arbitrary")),
        name="attn_a_x" if has_x else "attn_a_ctx",
    )(*args)


def _store_vt_chunks(v_ref, v):
    for c in range(v.shape[0] // NAT_CHUNK):
        v_ref[0, c] = v[c * NAT_CHUNK:(c + 1) * NAT_CHUNK].T.astype(BF16)


def _inproj_b_kernel(x_ref, ng_ref, sc_ref, sh_ref, w_ref, *out_refs, kv_only):
    h = _adaln(x_ref[0], ng_ref[...], sc_ref[0, 0], sh_ref[0, 0])
    p = jnp.dot(h.astype(BF16), w_ref[...], preferred_element_type=F32)
    if kv_only:
        k_ref, v_ref = out_refs
        k_ref[0] = p[:, :B_WIDTH].astype(BF16)
        _store_vt_chunks(v_ref, p[:, B_WIDTH:])
    else:
        q_ref, k_ref, v_ref, g_ref = out_refs
        q_ref[0] = (p[:, :B_WIDTH] * (ATTN_SCALE * LOG2E)).astype(BF16)
        k_ref[0] = p[:, B_WIDTH:2 * B_WIDTH].astype(BF16)
        _store_vt_chunks(v_ref, p[:, 2 * B_WIDTH:3 * B_WIDTH])
        g_ref[0] = _silu(p[:, 3 * B_WIDTH:]).astype(BF16)


def _inproj_b(x, ng, mods, w, tm, kv_only):
    b, r, d = x.shape
    n = w.shape[1]
    shared = mods.shape[0] == 1
    mod_idx = (lambda i, c: (0, c, 0, 0)) if shared else (lambda i, c: (i, c, 0, 0))
    row = lambda i, j: (i, j, 0)
    row_spec = pl.BlockSpec((1, tm, B_WIDTH), row)
    row_shape = jax.ShapeDtypeStruct((b, r, B_WIDTH), BF16)
    vt_spec = pl.BlockSpec((1, tm // NAT_CHUNK, B_WIDTH, NAT_CHUNK), lambda i, j: (i, j, 0, 0))
    vt_shape = jax.ShapeDtypeStruct((b, r // NAT_CHUNK, B_WIDTH, NAT_CHUNK), BF16)
    if kv_only:
        out_specs, out_shape = [row_spec, vt_spec], [row_shape, vt_shape]
    else:
        out_specs, out_shape = [row_spec, row_spec, vt_spec, row_spec], [row_shape, row_shape, vt_shape, row_shape]
    return pl.pallas_call(
        functools.partial(_inproj_b_kernel, kv_only=kv_only),
        grid=(b, r // tm),
        in_specs=[
            pl.BlockSpec((1, tm, d), row),
            pl.BlockSpec((1, d), lambda i, j: (0, 0)),
            pl.BlockSpec((1, 1, 1, d), lambda i, j: mod_idx(i, 1)),
            pl.BlockSpec((1, 1, 1, d), lambda i, j: mod_idx(i, 0)),
            pl.BlockSpec((d, n), lambda i, j: (0, 0)),
        ],
        out_specs=out_specs,
        out_shape=out_shape,
        compiler_params=_cparams(("parallel", "parallel")),
        name="inproj_b_ctx" if kv_only else "inproj_b_x",
    )(x, ng, mods, mods, w)


def _nat_scores(g, n_groups, q_ref, k_ref, kc_ref, tab_ref, s_scr):
    c0 = jnp.clip(g - 1, 0, n_groups - NAT_SPAN)
    kind = (g > 0).astype(jnp.int32) + (g == n_groups - 1).astype(jnp.int32)
    qs = pl.multiple_of(g * NAT_CHUNK, NAT_CHUNK)
    ks = pl.multiple_of(c0 * NAT_CHUNK, NAT_CHUNK)
    qt = q_ref[0, pl.ds(qs, NAT_CHUNK), :].astype(F32).T
    top = lax.broadcasted_iota(jnp.int32, qt.shape, 0) < HEAD_DIM
    a_t = jnp.concatenate([jnp.where(top, qt, 0.0), jnp.where(top, 0.0, qt)], axis=1).astype(BF16)
    s_span = jnp.dot(k_ref[0, pl.ds(ks, NAT_KEYS), :], a_t, preferred_element_type=F32) + tab_ref[0, kind]
    s_ctx = jnp.dot(kc_ref[0], a_t, preferred_element_type=F32)
    s_scr[0:NAT_KEYS, :] = s_span
    s_scr[NAT_KEYS:, :] = s_ctx
    return jnp.maximum(jnp.max(s_span, axis=0, keepdims=True), jnp.max(s_ctx, axis=0, keepdims=True))


def _nat_softmax(s_scr, p_scr, m, row0):
    for j in range(s_scr.shape[1] // LANES):
        sl = slice(j * LANES, (j + 1) * LANES)
        for r in range(s_scr.shape[0] // EXP_ROWS):
            src_rows = pl.ds(pl.multiple_of(row0 + r * EXP_ROWS, EXP_ROWS), EXP_ROWS)
            p_scr[r * EXP_ROWS:(r + 1) * EXP_ROWS, sl] = jnp.exp2(s_scr[src_rows, sl] - m[:, sl]).astype(BF16)


def _nat_output(g, n_groups, vt_ref, vct_ref, p_scr, o_ref):
    nk = p_scr.shape[0]
    c0 = jnp.clip(g - 1, 0, n_groups - NAT_SPAN)
    qs = pl.multiple_of(g * NAT_CHUNK, NAT_CHUNK)
    v_t = jnp.concatenate([vt_ref[0, c0 + c] for c in range(NAT_SPAN)] + [vct_ref[0, 0]], axis=1)
    v_ext = jnp.concatenate([v_t, jnp.ones((SUM_ROWS, nk), BF16)], axis=0)
    acc = jnp.dot(v_ext, p_scr[...], preferred_element_type=F32)
    o_t = acc[:LANES] / acc[LANES:LANES + 1]
    both = jnp.concatenate([o_t[:HEAD_DIM, :NAT_CHUNK], o_t[HEAD_DIM:, NAT_CHUNK:]], axis=0)
    o_ref[0, pl.ds(qs, NAT_CHUNK), :] = both.T.astype(BF16)


def _natten_kernel(q_ref, k_ref, vt_ref, kc_ref, vct_ref, tab_ref, o_ref, s0, s1, p0, p1, *, n_groups):
    row0 = pl.program_id(0) // N_PAIRS

    def body(i, carry):
        for u, (s_scr, p_scr) in enumerate(((s0, p0), (s1, p1))):
            g = 2 * i + u
            m = _nat_scores(g, n_groups, q_ref, k_ref, kc_ref, tab_ref, s_scr)
            _nat_softmax(s_scr, p_scr, m, row0)
            _nat_output(g, n_groups, vt_ref, vct_ref, p_scr, o_ref)
        return carry

    lax.fori_loop(0, n_groups // 2, body, 0)


def _natten(q, k, vt, kc, vct, tab):
    b, t, d = q.shape
    lc = kc.shape[1]
    n_groups = t // NAT_CHUNK
    nk = NAT_KEYS + lc
    blk = lambda h, i: (i, 0, h)
    return pl.pallas_call(
        functools.partial(_natten_kernel, n_groups=n_groups),
        grid=(N_PAIRS, b),
        in_specs=[
            pl.BlockSpec((1, t, LANES), blk),
            pl.BlockSpec((1, t, LANES), blk),
            pl.BlockSpec((1, n_groups, LANES, NAT_CHUNK), lambda h, i: (i, 0, h, 0)),
            pl.BlockSpec((1, lc, LANES), blk),
            pl.BlockSpec((1, 1, LANES, lc), lambda h, i: (i, 0, h, 0)),
            pl.BlockSpec((1,) + tab.shape[1:], lambda h, i: (h, 0, 0, 0)),
        ],
        out_specs=pl.BlockSpec((1, t, LANES), blk),
        out_shape=jax.ShapeDtypeStruct((b, t, d), BF16),
        scratch_shapes=[pltpu.VMEM((nk, 2 * NAT_CHUNK), F32)] * 2 + [pltpu.VMEM((nk, 2 * NAT_CHUNK), BF16)] * 2,
        compiler_params=_cparams(("parallel", "parallel")),
        name="natten_b",
    )(q, k, vt, kc, vct, tab)


def _outproj_final_kernel(o_ref, gate_ref, x_ref, gx_ref, w_ref, fg_ref, out_ref):
    u = (o_ref[0].astype(F32) * gate_ref[0].astype(F32)).astype(BF16)
    y = jnp.dot(u, w_ref[...], preferred_element_type=F32)
    x2 = x_ref[0] + gx_ref[0, 0] * y
    ms = jnp.mean(x2 * x2, axis=-1, keepdims=True)
    out_ref[0] = x2 * lax.rsqrt(ms + NORM_EPS) * fg_ref[...]


def _outproj_final(o, gate, x, mods, w_out, fg, tm):
    b, t, d = x.shape
    row = lambda i, j: (i, j, 0)
    return pl.pallas_call(
        _outproj_final_kernel,
        grid=(b, t // tm),
        in_specs=[
            pl.BlockSpec((1, tm, d), row),
            pl.BlockSpec((1, tm, d), row),
            pl.BlockSpec((1, tm, d), row),
            pl.BlockSpec((1, 1, 1, d), lambda i, j: (i, 2, 0, 0)),
            pl.BlockSpec((d, d), lambda i, j: (0, 0)),
            pl.BlockSpec((1, d), lambda i, j: (0, 0)),
        ],
        out_specs=pl.BlockSpec((1, tm, d), row),
        out_shape=jax.ShapeDtypeStruct((b, t, d), F32),
        compiler_params=_cparams(("parallel", "parallel")),
        name="outproj_final",
    )(o, gate, x, mods, w_out, fg)


def _rope_tables(t_len):
    pos = jnp.arange(t_len, dtype=jnp.int32)
    row = (pos // GRID_W).astype(F32)
    col = (pos % GRID_W).astype(F32)
    inv = ROPE_THETA ** (-jnp.arange(0, ROPE_AXIS_DIM, 2, dtype=F32) / ROPE_AXIS_DIM)
    ang_r = row[:, None] * inv
    ang_c = col[:, None] * inv
    zero = jnp.zeros_like(ang_r)
    cos_h = jnp.concatenate([jnp.cos(ang_r), jnp.cos(ang_r), jnp.cos(ang_c), jnp.cos(ang_c)], axis=1)
    sup_h = jnp.concatenate([-jnp.sin(ang_r), zero, -jnp.sin(ang_c), zero], axis=1)
    sdn_h = jnp.concatenate([zero, jnp.sin(ang_r), zero, jnp.sin(ang_c)], axis=1)
    two = lambda a: jnp.concatenate([a, a], axis=1)
    return two(cos_h), two(sup_h), two(sdn_h)


def _natten_tables(rpb, rows):
    h = rpb.shape[0]
    qcol = np.arange(GRID_W)
    c0 = np.clip(qcol - WIN_C // 2, 0, GRID_W - WIN_C)
    kcol = np.arange(GRID_W)
    valid = (kcol[None, :] >= c0[:, None]) & (kcol[None, :] < c0[:, None] + WIN_C)
    pad = GRID_W - WIN_C
    padded = jnp.pad(rpb.astype(F32) * LOG2E, ((0, 0), (0, 0), (pad, pad)))
    toep = jnp.stack([padded[:, :, GRID_W - 1 - j:2 * GRID_W - 1 - j] for j in range(GRID_W)], axis=2)
    toep = jnp.where(jnp.asarray(valid)[None, None], toep, NEG_BIG)
    toep_t = jnp.swapaxes(toep, 2, 3)
    neg = jnp.full((h, GRID_W, GRID_W), NEG_BIG, F32)
    n_groups = rows // NAT_ROWS
    span_rows = NAT_SPAN * NAT_ROWS

    def plan(g):
        ks = int(np.clip(g - 1, 0, n_groups - NAT_SPAN)) * NAT_ROWS
        out = []
        for s in range(span_rows):
            for i in range(NAT_ROWS):
                rq, rk = NAT_ROWS * g + i, ks + s
                r0 = int(np.clip(rq - WIN_R // 2, 0, rows - WIN_R))
                out.append(rk - rq + WIN_R - 1 if r0 <= rk < r0 + WIN_R else None)
        return out

    kinds = [plan(0), plan(1), plan(n_groups - 1)]
    assert all(plan(g) == kinds[1] for g in range(1, n_groups - 1))
    blocks = jnp.stack([jnp.stack([neg if dr is None else toep_t[:, dr] for dr in kind]) for kind in kinds])
    blocks = blocks.reshape(3, span_rows, NAT_ROWS, h // 2, 2, GRID_W, GRID_W).transpose(3, 0, 1, 5, 4, 2, 6)
    return blocks.reshape(h // 2, 3, span_rows * GRID_W, 2 * NAT_ROWS * GRID_W)


def kernel(x, c, ctx, c_ctx, norm_g, w_mod, b_mod, a_w_in, a_q_norm_g, a_k_norm_g, a_w_out,
           b_w_in, b_rpb, b_w_out, final_norm_g):
    bsz, t, d = x.shape
    lc = ctx.shape[1]

    n_rows = ((bsz + 1 + 7) // 8) * 8
    c_rows = jnp.zeros((n_rows, d), F32).at[:bsz].set(c).at[bsz].set(c_ctx)
    mods = _modulation(c_rows, w_mod, b_mod).reshape(w_mod.shape[0], n_rows, 3, 1, d)

    cos_t, sup, sdn = _rope_tables(t)
    one_t = jnp.ones((lc, LANES), F32)
    zero_t = jnp.zeros((lc, LANES), F32)
    head_block = np.kron(np.eye(LANES // HEAD_DIM), np.ones((HEAD_DIM, HEAD_DIM)))
    bd = jnp.asarray(head_block, BF16)
    two = lambda g: jnp.concatenate([g, g]).reshape(1, LANES).astype(F32)

    mx, mc = mods[0, :bsz], mods[0, bsz:bsz + 1]
    ng = norm_g[0].reshape(1, d)
    w_in = a_w_in[0].astype(BF16)
    qg, kg = two(a_q_norm_g[0]) * (ATTN_SCALE * LOG2E), two(a_k_norm_g[0])
    q, k, vt, gate = _inproj_a(x, ng, mx, w_in, qg, kg, bd, cos_t, sup, sdn, tm=512)
    qc, kc, vct, gate_c = _inproj_a(ctx, ng, mc, w_in, qg, kg, bd, one_t, zero_t, zero_t, tm=lc)
    w_out = a_w_out[0].astype(BF16)
    x1 = _attn_a(q, k, vt, kc, vct, gate, x, mx, w_out, tq=512)
    ctx1 = _attn_a(qc, None, None, kc, vct, gate_c, ctx, mc, w_out, tq=lc)

    mx, mc = mods[1, :bsz], mods[1, bsz:bsz + 1]
    ng = norm_g[1].reshape(1, d)
    w_in = b_w_in[0].astype(BF16)
    q, k, vt, gate = _inproj_b(x1, ng, mx, w_in, tm=256, kv_only=False)
    kc, vct = _inproj_b(ctx1, ng, mc, w_in[:, B_WIDTH:3 * B_WIDTH], tm=lc, kv_only=True)
    o = _natten(q, k, vt, kc, vct, _natten_tables(b_rpb[0], t // GRID_W))
    return _outproj_final(o, gate, x1, mx, b_w_out[0].astype(BF16), final_norm_g.reshape(1, d), tm=512)
```

```python
import functools

import jax
import jax.numpy as jnp
import numpy as np
from jax import lax
from jax.experimental import pallas as pl
from jax.experimental.pallas import tpu as pltpu

F32 = jnp.float32
BF16 = jnp.bfloat16

LANES = 128
VMEM_LIMIT = 56 * 1024 * 1024

D_MODEL = 1024
GRID_W = 64
HEAD_DIM = 64
NORM_EPS = 1e-6
ATTN_SCALE = HEAD_DIM ** -0.5
A_HEADS = 16
A_KV_HEADS = 4
A_WIDTH = A_HEADS * HEAD_DIM
A_KV_WIDTH = A_KV_HEADS * HEAD_DIM
ROPE_THETA = 10000.0
ROPE_AXIS_DIM = HEAD_DIM // 2
ROPE_HALF = ROPE_AXIS_DIM // 2
B_HEADS = 16
B_WIDTH = B_HEADS * HEAD_DIM
WIN_R = 8
WIN_C = 16
N_PAIRS = D_MODEL // LANES
NEG_BIG = -1e30
MAX_ROWS = 128
EXP_ROWS = 64
SUM_ROWS = 16
NAT_ROWS = 4
NAT_CHUNK = NAT_ROWS * GRID_W
NAT_SPAN = 3
NAT_KEYS = NAT_SPAN * NAT_CHUNK
N_SCORE_BUFS = 3
LOG2E = 1.4426950408889634


def _cparams(sem, flags=None):
    return pltpu.CompilerParams(dimension_semantics=sem, vmem_limit_bytes=VMEM_LIMIT, flags=flags)


def _mod_kernel(c_ref, w_ref, b_ref, o_ref):
    c = c_ref[...]
    s = c * jax.nn.sigmoid(c)
    o_ref[0] = jnp.dot(s, w_ref[0], precision=lax.Precision.HIGHEST,
                       preferred_element_type=F32) + b_ref[0]


def _modulation(c_rows, w_mod, b_mod):
    depth, d, n = w_mod.shape
    rows = c_rows.shape[0]
    tn = 1024
    return pl.pallas_call(
        _mod_kernel,
        grid=(depth, n // tn),
        in_specs=[
            pl.BlockSpec((rows, d), lambda l, j: (0, 0)),
            pl.BlockSpec((1, d, tn), lambda l, j: (l, 0, j)),
            pl.BlockSpec((1, 1, tn), lambda l, j: (l, 0, j)),
        ],
        out_specs=pl.BlockSpec((1, rows, tn), lambda l, j: (l, 0, j)),
        out_shape=jax.ShapeDtypeStruct((depth, rows, n), F32),
        compiler_params=_cparams(("arbitrary", "arbitrary")),
        name="adaln_mod",
    )(c_rows, w_mod, b_mod.reshape(depth, 1, n))


def _adaln(x, ng, sc, sh):
    ms = jnp.mean(x * x, axis=-1, keepdims=True)
    y = x * lax.rsqrt(ms + NORM_EPS) * ng
    return y * (1.0 + sc) + sh


def _silu(z):
    return z * jax.nn.sigmoid(z)


def _head_norm_rope(blk, gain, bd, cos_t, sin_up, sin_dn):
    sq = blk * blk
    hi = sq.astype(BF16)
    lo = (sq - hi.astype(F32)).astype(BF16)
    ssum = (jnp.dot(hi, bd, preferred_element_type=F32) + jnp.dot(lo, bd, preferred_element_type=F32))
    n = blk * lax.rsqrt(ssum * (1.0 / HEAD_DIM) + NORM_EPS) * gain
    up = pltpu.roll(n, LANES - ROPE_HALF, 1)
    dn = pltpu.roll(n, ROPE_HALF, 1)
    return n * cos_t + up * sin_up + dn * sin_dn


def _inproj_a_kernel(x_ref, ng_ref, sc_ref, sh_ref, w_ref, qg_ref, kg_ref, bd_ref, cos_ref, sup_ref, sdn_ref,
                     q_ref, k_ref, v_ref, g_ref):
    h = _adaln(x_ref[0], ng_ref[...], sc_ref[0, 0], sh_ref[0, 0])
    p = jnp.dot(h.astype(BF16), w_ref[...], preferred_element_type=F32)
    bd = bd_ref[...]
    cos_t, sup, sdn = cos_ref[...], sup_ref[...], sdn_ref[...]
    qg, kg = qg_ref[...], kg_ref[...]
    for j in range(A_WIDTH // LANES):
        blk = p[:, j * LANES:(j + 1) * LANES]
        q_ref[0, :, j * LANES:(j + 1) * LANES] = _head_norm_rope(blk, qg, bd, cos_t, sup, sdn).astype(BF16)
    for j in range(A_KV_WIDTH // LANES):
        blk = p[:, A_WIDTH + j * LANES:A_WIDTH + (j + 1) * LANES]
        k_ref[0, :, j * LANES:(j + 1) * LANES] = _head_norm_rope(blk, kg, bd, cos_t, sup, sdn).astype(BF16)
    v_ref[0, 0] = p[:, A_WIDTH + A_KV_WIDTH:A_WIDTH + 2 * A_KV_WIDTH].T.astype(BF16)
    g_ref[0] = _silu(p[:, A_WIDTH + 2 * A_KV_WIDTH:]).astype(BF16)


def _inproj_a(x, ng, mods, w, qg, kg, bd, cos_t, sup, sdn, tm):
    b, r, d = x.shape
    n = w.shape[1]
    shared = mods.shape[0] == 1
    mod_idx = (lambda i, j, c: (0, c, 0, 0)) if shared else (lambda i, j, c: (i, c, 0, 0))
    row = lambda i, j: (i, j, 0)
    const2 = lambda i, j: (0, 0)
    return pl.pallas_call(
        _inproj_a_kernel,
        grid=(b, r // tm),
        in_specs=[
            pl.BlockSpec((1, tm, d), row),
            pl.BlockSpec((1, d), const2),
            pl.BlockSpec((1, 1, 1, d), lambda i, j: mod_idx(i, j, 1)),
            pl.BlockSpec((1, 1, 1, d), lambda i, j: mod_idx(i, j, 0)),
            pl.BlockSpec((d, n), const2),
            pl.BlockSpec((1, LANES), const2),
            pl.BlockSpec((1, LANES), const2),
            pl.BlockSpec((LANES, LANES), const2),
            pl.BlockSpec((tm, LANES), lambda i, j: (j, 0)),
            pl.BlockSpec((tm, LANES), lambda i, j: (j, 0)),
            pl.BlockSpec((tm, LANES), lambda i, j: (j, 0)),
        ],
        out_specs=[
            pl.BlockSpec((1, tm, A_WIDTH), row),
            pl.BlockSpec((1, tm, A_KV_WIDTH), row),
            pl.BlockSpec((1, 1, A_KV_WIDTH, tm), lambda i, j: (i, j, 0, 0)),
            pl.BlockSpec((1, tm, A_WIDTH), row),
        ],
        out_shape=[
            jax.ShapeDtypeStruct((b, r, A_WIDTH), BF16),
            jax.ShapeDtypeStruct((b, r, A_KV_WIDTH), BF16),
            jax.ShapeDtypeStruct((b, r // tm, A_KV_WIDTH, tm), BF16),
            jax.ShapeDtypeStruct((b, r, A_WIDTH), BF16),
        ],
        compiler_params=_cparams(("parallel", "parallel")),
        name="inproj_a",
    )(x, ng, mods, mods, w, qg, kg, bd, cos_t, sup, sdn)


def _stack_qt(q2, half):
    qt = q2.astype(F32).T
    zero = jnp.zeros((HEAD_DIM, qt.shape[1]), F32)
    first = half == 0

    def place(h_t):
        return jnp.where(first, jnp.concatenate([h_t, zero], axis=0), jnp.concatenate([zero, h_t], axis=0))

    return jnp.concatenate([place(qt[:HEAD_DIM]), place(qt[HEAD_DIM:])], axis=1).astype(BF16)


def _softmax_cols(s_scr, p_scr, nk, m):
    m_out, a_out = [], []
    for j in range(s_scr.shape[1] // LANES):
        sl = slice(j * LANES, (j + 1) * LANES)
        mx = s_scr[0:MAX_ROWS, sl]
        for r in range(1, nk // MAX_ROWS):
            mx = jnp.maximum(mx, s_scr[r * MAX_ROWS:(r + 1) * MAX_ROWS, sl])
        m_new = jnp.maximum(m[:, sl], jnp.max(mx, axis=0, keepdims=True))
        for r in range(nk // EXP_ROWS):
            rows = slice(r * EXP_ROWS, (r + 1) * EXP_ROWS)
            p_scr[rows, sl] = jnp.exp2(s_scr[rows, sl] - m_new).astype(BF16)
        m_out.append(m_new)
        a_out.append(jnp.exp2(m[:, sl] - m_new))
    cat = lambda xs: jnp.concatenate(xs, axis=1)
    return cat(m_out), cat(a_out)


def _attn_a_kernel(*refs, tq, tk, n_x_chunks):
    n_in = 10 if n_x_chunks else 8
    if n_x_chunks:
        q_ref, kx_ref, vxt_ref, kc_ref, vct_ref, gate_ref, x_ref, gx_ref, w_ref, out_ref = refs[:n_in]
    else:
        q_ref, kc_ref, vct_ref, gate_ref, x_ref, gx_ref, w_ref, out_ref = refs[:n_in]
    o_scr = refs[n_in]
    s_bufs = refs[n_in + 1:n_in + 1 + N_SCORE_BUFS]
    p_bufs = refs[n_in + 1 + N_SCORE_BUFS:]
    hp = pl.program_id(2)
    kv_head = hp // 2
    vrow = pl.multiple_of(kv_head * HEAD_DIM, HEAD_DIM)
    a_t = _stack_qt(q_ref[0], kv_head % 2)
    cols = 2 * tq
    lc = kc_ref.shape[1]

    n_chunks = 1 + n_x_chunks
    chunk_rows = lambda i: lc if i == 0 else tk
    k_blk = lambda i: kc_ref[0] if i == 0 else kx_ref[0, (i - 1) * tk:i * tk, :]
    v_blk = lambda i: (vct_ref[0, 0, pl.ds(vrow, HEAD_DIM), :] if i == 0
                       else vxt_ref[0, i - 1, pl.ds(vrow, HEAD_DIM), :])

    def scores(i):
        nk = chunk_rows(i)
        s_bufs[i % N_SCORE_BUFS][0:nk, :] = jnp.dot(k_blk(i), a_t, preferred_element_type=F32)

    def consume(i, carry):
        m, acc = carry
        nk = chunk_rows(i)
        p_scr = p_bufs[i % len(p_bufs)]
        m, alpha = _softmax_cols(s_bufs[i % N_SCORE_BUFS], p_scr, nk, m)
        v_ext = jnp.concatenate([v_blk(i), jnp.ones((SUM_ROWS, nk), BF16)], axis=0)
        acc = alpha * acc + jnp.dot(v_ext, p_scr[0:nk, :], preferred_element_type=F32)
        return m, acc

    carry = (jnp.full((1, cols), NEG_BIG, F32), jnp.zeros((HEAD_DIM + SUM_ROWS, cols), F32))
    scores(0)
    for i in range(n_chunks):
        if i + 1 < n_chunks:
            scores(i + 1)
        carry = consume(i, carry)
    _, acc = carry
    o_t = acc[:HEAD_DIM] / acc[HEAD_DIM:HEAD_DIM + 1]
    o_scr[hp] = jnp.concatenate([o_t[:, :tq], o_t[:, tq:]], axis=0).T

    @pl.when(hp == N_PAIRS - 1)
    def _():
        o_full = jnp.concatenate([o_scr[j] for j in range(N_PAIRS)], axis=1)
        u = (o_full * gate_ref[0].astype(F32)).astype(BF16)
        y = jnp.dot(u, w_ref[...], preferred_element_type=F32)
        out_ref[0] = x_ref[0] + gx_ref[0, 0] * y


def _attn_a(q, kx, vxt, kc, vct, gate, x, mods, w_out, tq):
    b, r, d = x.shape
    has_x = kx is not None
    shared = mods.shape[0] == 1
    qrow = lambda i, j, h: (i, j, 0)
    kvp = lambda i, j, h: (i, 0, h // 4)
    whole = lambda i, j, h: (i, 0, 0, 0)
    in_specs = [pl.BlockSpec((1, tq, LANES), lambda i, j, h: (i, j, h))]
    args = [q]
    lc = kc.shape[1]
    tk = lc
    n_x_chunks = 0
    if has_x:
        t = kx.shape[1]
        n_x_chunks, tk = vxt.shape[1], vxt.shape[3]
        in_specs += [pl.BlockSpec((1, t, LANES), kvp), pl.BlockSpec((1,) + vxt.shape[1:], whole)]
        args += [kx, vxt]
    in_specs += [
        pl.BlockSpec((1, lc, LANES), kvp),
        pl.BlockSpec((1,) + vct.shape[1:], whole),
        pl.BlockSpec((1, tq, d), qrow),
        pl.BlockSpec((1, tq, d), qrow),
        pl.BlockSpec((1, 1, 1, d), (lambda i, j, h: (0, 2, 0, 0)) if shared else (lambda i, j, h: (i, 2, 0, 0))),
        pl.BlockSpec((d, d), lambda i, j, h: (0, 0)),
    ]
    args += [kc, vct, gate, x, mods, w_out]
    return pl.pallas_call(
        functools.partial(_attn_a_kernel, tq=tq, tk=tk, n_x_chunks=n_x_chunks),
        grid=(b, r // tq, N_PAIRS),
        in_specs=in_specs,
        out_specs=pl.BlockSpec((1, tq, d), qrow),
        out_shape=jax.ShapeDtypeStruct((b, r, d), F32),
        scratch_shapes=[
            pltpu.VMEM((N_PAIRS, tq, LANES), F32),
            *[pltpu.VMEM((max(tk, lc), 2 * tq), F32)] * N_SCORE_BUFS,
            *[pltpu.VMEM((max(tk, lc), 2 * tq), BF16)] * 2,
        ],
        compiler_params=_cparams(("parallel", "parallel", "arbitrary")),
        name="attn_a_x" if has_x else "attn_a_ctx",
    )(*args)


def _store_vt_chunks(v_ref, v):
    for c in range(v.shape[0] // NAT_CHUNK):
        v_ref[0, c] = v[c * NAT_CHUNK:(c + 1) * NAT_CHUNK].T.astype(BF16)


def _inproj_b_kernel(x_ref, ng_ref, sc_ref, sh_ref, w_ref, *out_refs, kv_only):
    h = _adaln(x_ref[0], ng_ref[...], sc_ref[0, 0], sh_ref[0, 0])
    p = jnp.dot(h.astype(BF16), w_ref[...], preferred_element_type=F32)
    if kv_only:
        k_ref, v_ref = out_refs
        k_ref[0] = p[:, :B_WIDTH].astype(BF16)
        _store_vt_chunks(v_ref, p[:, B_WIDTH:])
    else:
        q_ref, k_ref, v_ref, g_ref = out_refs
        q_ref[0] = (p[:, :B_WIDTH] * (ATTN_SCALE * LOG2E)).astype(BF16)
        k_ref[0] = p[:, B_WIDTH:2 * B_WIDTH].astype(BF16)
        _store_vt_chunks(v_ref, p[:, 2 * B_WIDTH:3 * B_WIDTH])
        g_ref[0] = _silu(p[:, 3 * B_WIDTH:]).astype(BF16)


def _inproj_b(x, ng, mods, w, tm, kv_only):
    b, r, d = x.shape
    n = w.shape[1]
    shared = mods.shape[0] == 1
    mod_idx = (lambda i, c: (0, c, 0, 0)) if shared else (lambda i, c: (i, c, 0, 0))
    row = lambda i, j: (i, j, 0)
    row_spec = pl.BlockSpec((1, tm, B_WIDTH), row)
    row_shape = jax.ShapeDtypeStruct((b, r, B_WIDTH), BF16)
    vt_spec = pl.BlockSpec((1, tm // NAT_CHUNK, B_WIDTH, NAT_CHUNK), lambda i, j: (i, j, 0, 0))
    vt_shape = jax.ShapeDtypeStruct((b, r // NAT_CHUNK, B_WIDTH, NAT_CHUNK), BF16)
    if kv_only:
        out_specs, out_shape = [row_spec, vt_spec], [row_shape, vt_shape]
    else:
        out_specs, out_shape = [row_spec, row_spec, vt_spec, row_spec], [row_shape, row_shape, vt_shape, row_shape]
    return pl.pallas_call(
        functools.partial(_inproj_b_kernel, kv_only=kv_only),
        grid=(b, r // tm),
        in_specs=[
            pl.BlockSpec((1, tm, d), row),
            pl.BlockSpec((1, d), lambda i, j: (0, 0)),
            pl.BlockSpec((1, 1, 1, d), lambda i, j: mod_idx(i, 1)),
            pl.BlockSpec((1, 1, 1, d), lambda i, j: mod_idx(i, 0)),
            pl.BlockSpec((d, n), lambda i, j: (0, 0)),
        ],
        out_specs=out_specs,
        out_shape=out_shape,
        compiler_params=_cparams(("parallel", "parallel")),
        name="inproj_b_ctx" if kv_only else "inproj_b_x",
    )(x, ng, mods, mods, w)


def _nat_scores(g, n_groups, q_ref, k_ref, kc_ref, tab_ref, s_scr):
    g = jnp.asarray(g, jnp.int32)
    c0 = jnp.clip(g - 1, 0, n_groups - NAT_SPAN)
    kind = (g > 0).astype(jnp.int32) + (g == n_groups - 1).astype(jnp.int32)
    qs = pl.multiple_of(g * NAT_CHUNK, NAT_CHUNK)
    ks = pl.multiple_of(c0 * NAT_CHUNK, NAT_CHUNK)
    qt = q_ref[0, pl.ds(qs, NAT_CHUNK), :].astype(F32).T
    top = lax.broadcasted_iota(jnp.int32, qt.shape, 0) < HEAD_DIM
    a_t = jnp.concatenate([jnp.where(top, qt, 0.0), jnp.where(top, 0.0, qt)], axis=1).astype(BF16)
    s_span = jnp.dot(k_ref[0, pl.ds(ks, NAT_KEYS), :], a_t, preferred_element_type=F32) + tab_ref[0, kind]
    s_ctx = jnp.dot(kc_ref[0], a_t, preferred_element_type=F32)
    s_scr[0:NAT_KEYS, :] = s_span
    s_scr[NAT_KEYS:, :] = s_ctx
    return jnp.maximum(jnp.max(s_span, axis=0, keepdims=True), jnp.max(s_ctx, axis=0, keepdims=True))


def _nat_softmax(s_scr, p_scr, m, row0):
    for j in range(s_scr.shape[1] // LANES):
        sl = slice(j * LANES, (j + 1) * LANES)
        for r in range(s_scr.shape[0] // EXP_ROWS):
            src_rows = pl.ds(pl.multiple_of(row0 + r * EXP_ROWS, EXP_ROWS), EXP_ROWS)
            p_scr[r * EXP_ROWS:(r + 1) * EXP_ROWS, sl] = jnp.exp2(s_scr[src_rows, sl] - m[:, sl]).astype(BF16)


def _nat_output(g, n_groups, vt_ref, vct_ref, p_scr, o_ref):
    nk = p_scr.shape[0]
    c0 = jnp.clip(g - 1, 0, n_groups - NAT_SPAN)
    qs = pl.multiple_of(g * NAT_CHUNK, NAT_CHUNK)
    v_t = jnp.concatenate([vt_ref[0, c0 + c] for c in range(NAT_SPAN)] + [vct_ref[0, 0]], axis=1)
    v_ext = jnp.concatenate([v_t, jnp.ones((SUM_ROWS, nk), BF16)], axis=0)
    acc = jnp.dot(v_ext, p_scr[...], preferred_element_type=F32)
    o_t = acc[:LANES] / acc[LANES:LANES + 1]
    both = jnp.concatenate([o_t[:HEAD_DIM, :NAT_CHUNK], o_t[HEAD_DIM:, NAT_CHUNK:]], axis=0)
    o_ref[0, pl.ds(qs, NAT_CHUNK), :] = both.T.astype(BF16)


def _natten_kernel(q_ref, k_ref, vt_ref, kc_ref, vct_ref, tab_ref, o_ref, s0, s1, p0, p1, *, n_groups):
    row0 = pl.program_id(0) // N_PAIRS

    def body(i, carry):
        for u, (s_scr, p_scr) in enumerate(((s0, p0), (s1, p1))):
            g = 2 * i + u
            m = _nat_scores(g, n_groups, q_ref, k_ref, kc_ref, tab_ref, s_scr)
            _nat_softmax(s_scr, p_scr, m, row0)
            _nat_output(g, n_groups, vt_ref, vct_ref, p_scr, o_ref)
        return carry

    lax.fori_loop(0, n_groups // 2, body, 0)


def _natten(q, k, vt, kc, vct, tab):
    b, t, d = q.shape
    lc = kc.shape[1]
    n_groups = t // NAT_CHUNK
    nk = NAT_KEYS + lc
    blk = lambda h, i: (i, 0, h)
    return pl.pallas_call(
        functools.partial(_natten_kernel, n_groups=n_groups),
        grid=(N_PAIRS, b),
        in_specs=[
            pl.BlockSpec((1, t, LANES), blk),
            pl.BlockSpec((1, t, LANES), blk),
            pl.BlockSpec((1, n_groups, LANES, NAT_CHUNK), lambda h, i: (i, 0, h, 0)),
            pl.BlockSpec((1, lc, LANES), blk),
            pl.BlockSpec((1, 1, LANES, lc), lambda h, i: (i, 0, h, 0)),
            pl.BlockSpec((1,) + tab.shape[1:], lambda h, i: (h, 0, 0, 0)),
        ],
        out_specs=pl.BlockSpec((1, t, LANES), blk),
        out_shape=jax.ShapeDtypeStruct((b, t, d), BF16),
        scratch_shapes=[pltpu.VMEM((nk, 2 * NAT_CHUNK), F32)] * 2 + [pltpu.VMEM((nk, 2 * NAT_CHUNK), BF16)] * 2,
        compiler_params=_cparams(("parallel", "parallel")),
        name="natten_b",
    )(q, k, vt, kc, vct, tab)


def _outproj_final_kernel(o_ref, gate_ref, x_ref, gx_ref, w_ref, fg_ref, out_ref):
    u = (o_ref[0].astype(F32) * gate_ref[0].astype(F32)).astype(BF16)
    y = jnp.dot(u, w_ref[...], preferred_element_type=F32)
    x2 = x_ref[0] + gx_ref[0, 0] * y
    ms = jnp.mean(x2 * x2, axis=-1, keepdims=True)
    out_ref[0] = x2 * lax.rsqrt(ms + NORM_EPS) * fg_ref[...]


def _outproj_final(o, gate, x, mods, w_out, fg, tm):
    b, t, d = x.shape
    row = lambda i, j: (i, j, 0)
    return pl.pallas_call(
        _outproj_final_kernel,
        grid=(b, t // tm),
        in_specs=[
            pl.BlockSpec((1, tm, d), row),
            pl.BlockSpec((1, tm, d), row),
            pl.BlockSpec((1, tm, d), row),
            pl.BlockSpec((1, 1, 1, d), lambda i, j: (i, 2, 0, 0)),
            pl.BlockSpec((d, d), lambda i, j: (0, 0)),
            pl.BlockSpec((1, d), lambda i, j: (0, 0)),
        ],
        out_specs=pl.BlockSpec((1, tm, d), row),
        out_shape=jax.ShapeDtypeStruct((b, t, d), F32),
        compiler_params=_cparams(("parallel", "parallel")),
        name="outproj_final",
    )(o, gate, x, mods, w_out, fg)


def _rope_tables(t_len):
    pos = jnp.arange(t_len, dtype=jnp.int32)
    row = (pos // GRID_W).astype(F32)
    col = (pos % GRID_W).astype(F32)
    inv = ROPE_THETA ** (-jnp.arange(0, ROPE_AXIS_DIM, 2, dtype=F32) / ROPE_AXIS_DIM)
    ang_r = row[:, None] * inv
    ang_c = col[:, None] * inv
    zero = jnp.zeros_like(ang_r)
    cos_h = jnp.concatenate([jnp.cos(ang_r), jnp.cos(ang_r), jnp.cos(ang_c), jnp.cos(ang_c)], axis=1)
    sup_h = jnp.concatenate([-jnp.sin(ang_r), zero, -jnp.sin(ang_c), zero], axis=1)
    sdn_h = jnp.concatenate([zero, jnp.sin(ang_r), zero, jnp.sin(ang_c)], axis=1)
    two = lambda a: jnp.concatenate([a, a], axis=1)
    return two(cos_h), two(sup_h), two(sdn_h)


def _natten_tables(rpb, rows):
    h = rpb.shape[0]
    qcol = np.arange(GRID_W)
    c0 = np.clip(qcol - WIN_C // 2, 0, GRID_W - WIN_C)
    kcol = np.arange(GRID_W)
    valid = (kcol[None, :] >= c0[:, None]) & (kcol[None, :] < c0[:, None] + WIN_C)
    pad = GRID_W - WIN_C
    padded = jnp.pad(rpb.astype(F32) * LOG2E, ((0, 0), (0, 0), (pad, pad)))
    toep = jnp.stack([padded[:, :, GRID_W - 1 - j:2 * GRID_W - 1 - j] for j in range(GRID_W)], axis=2)
    toep = jnp.where(jnp.asarray(valid)[None, None], toep, NEG_BIG)
    toep_t = jnp.swapaxes(toep, 2, 3)
    neg = jnp.full((h, GRID_W, GRID_W), NEG_BIG, F32)
    n_groups = rows // NAT_ROWS
    span_rows = NAT_SPAN * NAT_ROWS

    def plan(g):
        ks = int(np.clip(g - 1, 0, n_groups - NAT_SPAN)) * NAT_ROWS
        out = []
        for s in range(span_rows):
            for i in range(NAT_ROWS):
                rq, rk = NAT_ROWS * g + i, ks + s
                r0 = int(np.clip(rq - WIN_R // 2, 0, rows - WIN_R))
                out.append(rk - rq + WIN_R - 1 if r0 <= rk < r0 + WIN_R else None)
        return out

    kinds = [plan(0), plan(1), plan(n_groups - 1)]
    assert all(plan(g) == kinds[1] for g in range(1, n_groups - 1))
    blocks = jnp.stack([jnp.stack([neg if dr is None else toep_t[:, dr] for dr in kind]) for kind in kinds])
    blocks = blocks.reshape(3, span_rows, NAT_ROWS, h // 2, 2, GRID_W, GRID_W).transpose(3, 0, 1, 5, 4, 2, 6)
    return blocks.reshape(h // 2, 3, span_rows * GRID_W, 2 * NAT_ROWS * GRID_W)


def kernel(x, c, ctx, c_ctx, norm_g, w_mod, b_mod, a_w_in, a_q_norm_g, a_k_norm_g, a_w_out,
           b_w_in, b_rpb, b_w_out, final_norm_g):
    bsz, t, d = x.shape
    lc = ctx.shape[1]

    n_rows = ((bsz + 1 + 7) // 8) * 8
    c_rows = jnp.zeros((n_rows, d), F32).at[:bsz].set(c).at[bsz].set(c_ctx)
    mods = _modulation(c_rows, w_mod, b_mod).reshape(w_mod.shape[0], n_rows, 3, 1, d)

    cos_t, sup, sdn = _rope_tables(t)
    one_t = jnp.ones((lc, LANES), F32)
    zero_t = jnp.zeros((lc, LANES), F32)
    head_block = np.kron(np.eye(LANES // HEAD_DIM), np.ones((HEAD_DIM, HEAD_DIM)))
    bd = jnp.asarray(head_block, BF16)
    two = lambda g: jnp.concatenate([g, g]).reshape(1, LANES).astype(F32)

    mx, mc = mods[0, :bsz], mods[0, bsz:bsz + 1]
    ng = norm_g[0].reshape(1, d)
    w_in = a_w_in[0].astype(BF16)
    qg, kg = two(a_q_norm_g[0]) * (ATTN_SCALE * LOG2E), two(a_k_norm_g[0])
    q, k, vt, gate = _inproj_a(x, ng, mx, w_in, qg, kg, bd, cos_t, sup, sdn, tm=512)
    qc, kc, vct, gate_c = _inproj_a(ctx, ng, mc, w_in, qg, kg, bd, one_t, zero_t, zero_t, tm=lc)
    w_out = a_w_out[0].astype(BF16)
    x1 = _attn_a(q, k, vt, kc, vct, gate, x, mx, w_out, tq=512)
    ctx1 = _attn_a(qc, None, None, kc, vct, gate_c, ctx, mc, w_out, tq=lc)

    mx, mc = mods[1, :bsz], mods[1, bsz:bsz + 1]
    ng = norm_g[1].reshape(1, d)
    w_in = b_w_in[0].astype(BF16)
    q, k, vt, gate = _inproj_b(x1, ng, mx, w_in, tm=256, kv_only=False)
    kc, vct = _inproj_b(ctx1, ng, mc, w_in[:, B_WIDTH:3 * B_WIDTH], tm=lc, kv_only=True)
    o = _natten(q, k, vt, kc, vct, _natten_tables(b_rpb[0], t // GRID_W))
    return _outproj_final(o, gate, x1, mx, b_w_out[0].astype(BF16), final_norm_g.reshape(1, d), tm=512)
```

```python
import functools

import jax
import jax.numpy as jnp
import numpy as np
from jax import lax
from jax.experimental import pallas as pl
from jax.experimental.pallas import tpu as pltpu

F32 = jnp.float32
BF16 = jnp.bfloat16

LANES = 128
SUBLANES = 8
VMEM_LIMIT = 56 * 1024 * 1024

D_MODEL = 1024
GRID_W = 64
HEAD_DIM = 64
NORM_EPS = 1e-6
ATTN_SCALE = HEAD_DIM ** -0.5
A_HEADS = 16
A_KV_HEADS = 4
A_WIDTH = A_HEADS * HEAD_DIM
A_KV_WIDTH = A_KV_HEADS * HEAD_DIM
ROPE_THETA = 10000.0
ROPE_AXIS_DIM = HEAD_DIM // 2
ROPE_HALF = ROPE_AXIS_DIM // 2
B_HEADS = 16
B_WIDTH = B_HEADS * HEAD_DIM
WIN_R = 8
WIN_C = 16
N_PAIRS = D_MODEL // LANES
NEG_BIG = -1e30
EXP_ROWS = 64
SUM_ROWS = 16
NAT_ROWS = 4
NAT_CHUNK = NAT_ROWS * GRID_W
NAT_SPAN = 3
NAT_KEYS = NAT_SPAN * NAT_CHUNK
N_SCORE_BUFS = 3
LOG2E = 1.4426950408889634


def _cparams(sem, flags=None):
    return pltpu.CompilerParams(dimension_semantics=sem, vmem_limit_bytes=VMEM_LIMIT, flags=flags)


def _mod_kernel(c_ref, w_ref, b_ref, o_ref):
    c = c_ref[...]
    s = c * jax.nn.sigmoid(c)
    o_ref[0] = jnp.dot(s, w_ref[0], precision=lax.Precision.HIGHEST,
                       preferred_element_type=F32) + b_ref[0]


def _modulation(c_rows, w_mod, b_mod):
    depth, d, n = w_mod.shape
    rows = c_rows.shape[0]
    tn = 1024
    return pl.pallas_call(
        _mod_kernel,
        grid=(depth, n // tn),
        in_specs=[
            pl.BlockSpec((rows, d), lambda l, j: (0, 0)),
            pl.BlockSpec((1, d, tn), lambda l, j: (l, 0, j)),
            pl.BlockSpec((1, 1, tn), lambda l, j: (l, 0, j)),
        ],
        out_specs=pl.BlockSpec((1, rows, tn), lambda l, j: (l, 0, j)),
        out_shape=jax.ShapeDtypeStruct((depth, rows, n), F32),
        compiler_params=_cparams(("arbitrary", "arbitrary")),
        name="adaln_mod",
    )(c_rows, w_mod, b_mod.reshape(depth, 1, n))


def _adaln(x, ng, sc, sh):
    ms = jnp.mean(x * x, axis=-1, keepdims=True)
    y = x * lax.rsqrt(ms + NORM_EPS) * ng
    return y * (1.0 + sc) + sh


def _silu(z):
    return z * jax.nn.sigmoid(z)


def _head_norm_rope(blk, gain, bd, cos_t, sin_up, sin_dn):
    sq = blk * blk
    hi = sq.astype(BF16)
    lo = (sq - hi.astype(F32)).astype(BF16)
    ssum = (jnp.dot(hi, bd, preferred_element_type=F32) + jnp.dot(lo, bd, preferred_element_type=F32))
    n = blk * lax.rsqrt(ssum * (1.0 / HEAD_DIM) + NORM_EPS) * gain
    up = pltpu.roll(n, LANES - ROPE_HALF, 1)
    dn = pltpu.roll(n, ROPE_HALF, 1)
    return n * cos_t + up * sin_up + dn * sin_dn


def _inproj_a_kernel(x_ref, ng_ref, sc_ref, sh_ref, w_ref, qg_ref, kg_ref, bd_ref, cos_ref, sup_ref, sdn_ref,
                     q_ref, k_ref, v_ref, g_ref):
    h = _adaln(x_ref[0], ng_ref[...], sc_ref[0, 0], sh_ref[0, 0])
    p = jnp.dot(h.astype(BF16), w_ref[...], preferred_element_type=F32)
    bd = bd_ref[...]
    cos_t, sup, sdn = cos_ref[...], sup_ref[...], sdn_ref[...]
    qg, kg = qg_ref[...], kg_ref[...]
    for j in range(A_WIDTH // LANES):
        blk = p[:, j * LANES:(j + 1) * LANES]
        q_ref[0, :, j * LANES:(j + 1) * LANES] = _head_norm_rope(blk, qg, bd, cos_t, sup, sdn).astype(BF16)
    for j in range(A_KV_WIDTH // LANES):
        blk = p[:, A_WIDTH + j * LANES:A_WIDTH + (j + 1) * LANES]
        k_ref[0, :, j * LANES:(j + 1) * LANES] = _head_norm_rope(blk, kg, bd, cos_t, sup, sdn).astype(BF16)
    v_ref[0, 0] = p[:, A_WIDTH + A_KV_WIDTH:A_WIDTH + 2 * A_KV_WIDTH].T.astype(BF16)
    g_ref[0] = _silu(p[:, A_WIDTH + 2 * A_KV_WIDTH:]).astype(BF16)


def _inproj_a(x, ng, mods, w, qg, kg, bd, cos_t, sup, sdn, tm):
    b, r, d = x.shape
    n = w.shape[1]
    shared = mods.shape[0] == 1
    mod_idx = (lambda i, j, c: (0, c, 0, 0)) if shared else (lambda i, j, c: (i, c, 0, 0))
    row = lambda i, j: (i, j, 0)
    const2 = lambda i, j: (0, 0)
    return pl.pallas_call(
        _inproj_a_kernel,
        grid=(b, r // tm),
        in_specs=[
            pl.BlockSpec((1, tm, d), row),
            pl.BlockSpec((1, d), const2),
            pl.BlockSpec((1, 1, 1, d), lambda i, j: mod_idx(i, j, 1)),
            pl.BlockSpec((1, 1, 1, d), lambda i, j: mod_idx(i, j, 0)),
            pl.BlockSpec((d, n), const2),
            pl.BlockSpec((1, LANES), const2),
            pl.BlockSpec((1, LANES), const2),
            pl.BlockSpec((LANES, LANES), const2),
            pl.BlockSpec((tm, LANES), lambda i, j: (j, 0)),
            pl.BlockSpec((tm, LANES), lambda i, j: (j, 0)),
            pl.BlockSpec((tm, LANES), lambda i, j: (j, 0)),
        ],
        out_specs=[
            pl.BlockSpec((1, tm, A_WIDTH), row),
            pl.BlockSpec((1, tm, A_KV_WIDTH), row),
            pl.BlockSpec((1, 1, A_KV_WIDTH, tm), lambda i, j: (i, j, 0, 0)),
            pl.BlockSpec((1, tm, A_WIDTH), row),
        ],
        out_shape=[
            jax.ShapeDtypeStruct((b, r, A_WIDTH), BF16),
            jax.ShapeDtypeStruct((b, r, A_KV_WIDTH), BF16),
            jax.ShapeDtypeStruct((b, r // tm, A_KV_WIDTH, tm), BF16),
            jax.ShapeDtypeStruct((b, r, A_WIDTH), BF16),
        ],
        compiler_params=_cparams(("parallel", "parallel")),
        name="inproj_a",
    )(x, ng, mods, mods, w, qg, kg, bd, cos_t, sup, sdn)


def _stack_qt(q2, half):
    qt = q2.astype(F32).T
    zero = jnp.zeros((HEAD_DIM, qt.shape[1]), F32)
    first = half == 0

    def place(h_t):
        return jnp.where(first, jnp.concatenate([h_t, zero], axis=0), jnp.concatenate([zero, h_t], axis=0))

    return jnp.concatenate([place(qt[:HEAD_DIM]), place(qt[HEAD_DIM:])], axis=1).astype(BF16)


def _run_time_zero(step):
    return pl.multiple_of((step // N_PAIRS) * SUBLANES, SUBLANES)


def _reread_barrier(s_scr, s, row0):
    s_scr[pl.ds(row0, SUBLANES), 0:LANES] = s[0:SUBLANES, 0:LANES]


def _softmax_cols(s_scr, p_scr, nk, m, chunk_max):
    m_new = jnp.maximum(m, chunk_max)
    for j in range(s_scr.shape[1] // LANES):
        sl = slice(j * LANES, (j + 1) * LANES)
        for r in range(nk // EXP_ROWS):
            rows = slice(r * EXP_ROWS, (r + 1) * EXP_ROWS)
            p_scr[rows, sl] = jnp.exp2(s_scr[rows, sl] - m_new[:, sl]).astype(BF16)
    return m_new, jnp.exp2(m - m_new)


def _attn_a_kernel(*refs, tq, tk, n_x_chunks):
    n_in = 10 if n_x_chunks else 8
    if n_x_chunks:
        q_ref, kx_ref, vxt_ref, kc_ref, vct_ref, gate_ref, x_ref, gx_ref, w_ref, out_ref = refs[:n_in]
    else:
        q_ref, kc_ref, vct_ref, gate_ref, x_ref, gx_ref, w_ref, out_ref = refs[:n_in]
    o_scr = refs[n_in]
    s_bufs = refs[n_in + 1:n_in + 1 + N_SCORE_BUFS]
    p_bufs = refs[n_in + 1 + N_SCORE_BUFS:]
    hp = pl.program_id(2)
    row0 = _run_time_zero(hp)
    kv_head = hp // 2
    vrow = pl.multiple_of(kv_head * HEAD_DIM, HEAD_DIM)
    a_t = _stack_qt(q_ref[0], kv_head % 2)
    cols = 2 * tq
    lc = kc_ref.shape[1]

    n_chunks = 1 + n_x_chunks
    chunk_rows = lambda i: lc if i == 0 else tk
    k_blk = lambda i: kc_ref[0] if i == 0 else kx_ref[0, (i - 1) * tk:i * tk, :]
    v_blk = lambda i: (vct_ref[0, 0, pl.ds(vrow, HEAD_DIM), :] if i == 0
                       else vxt_ref[0, i - 1, pl.ds(vrow, HEAD_DIM), :])

    def scores(i):
        nk = chunk_rows(i)
        s_scr = s_bufs[i % N_SCORE_BUFS]
        s = jnp.dot(k_blk(i), a_t, preferred_element_type=F32)
        s_scr[0:nk, :] = s
        _reread_barrier(s_scr, s, row0)
        return jnp.max(s, axis=0, keepdims=True)

    def consume(i, carry):
        m, acc = carry
        nk = chunk_rows(i)
        p_scr = p_bufs[i % len(p_bufs)]
        m, alpha = _softmax_cols(s_bufs[i % N_SCORE_BUFS], p_scr, nk, m, cmax[i])
        v_ext = jnp.concatenate([v_blk(i), jnp.ones((SUM_ROWS, nk), BF16)], axis=0)
        acc = alpha * acc + jnp.dot(v_ext, p_scr[0:nk, :], preferred_element_type=F32)
        return m, acc

    carry = (jnp.full((1, cols), NEG_BIG, F32), jnp.zeros((HEAD_DIM + SUM_ROWS, cols), F32))
    cmax = {0: scores(0)}
    for i in range(n_chunks):
        if i + 1 < n_chunks:
            cmax[i + 1] = scores(i + 1)
        carry = consume(i, carry)
    _, acc = carry
    o_t = acc[:HEAD_DIM] / acc[HEAD_DIM:HEAD_DIM + 1]
    o_scr[hp] = jnp.concatenate([o_t[:, :tq], o_t[:, tq:]], axis=0).T

    @pl.when(hp == N_PAIRS - 1)
    def _():
        o_full = jnp.concatenate([o_scr[j] for j in range(N_PAIRS)], axis=1)
        u = (o_full * gate_ref[0].astype(F32)).astype(BF16)
        y = jnp.dot(u, w_ref[...], preferred_element_type=F32)
        out_ref[0] = x_ref[0] + gx_ref[0, 0] * y


def _attn_a(q, kx, vxt, kc, vct, gate, x, mods, w_out, tq):
    b, r, d = x.shape
    has_x = kx is not None
    shared = mods.shape[0] == 1
    qrow = lambda i, j, h: (i, j, 0)
    kvp = lambda i, j, h: (i, 0, h // 4)
    whole = lambda i, j, h: (i, 0, 0, 0)
    in_specs = [pl.BlockSpec((1, tq, LANES), lambda i, j, h: (i, j, h))]
    args = [q]
    lc = kc.shape[1]
    tk = lc
    n_x_chunks = 0
    if has_x:
        t = kx.shape[1]
        n_x_chunks, tk = vxt.shape[1], vxt.shape[3]
        in_specs += [pl.BlockSpec((1, t, LANES), kvp), pl.BlockSpec((1,) + vxt.shape[1:], whole)]
        args += [kx, vxt]
    in_specs += [
        pl.BlockSpec((1, lc, LANES), kvp),
        pl.BlockSpec((1,) + vct.shape[1:], whole),
        pl.BlockSpec((1, tq, d), qrow),
        pl.BlockSpec((1, tq, d), qrow),
        pl.BlockSpec((1, 1, 1, d), (lambda i, j, h: (0, 2, 0, 0)) if shared else (lambda i, j, h: (i, 2, 0, 0))),
        pl.BlockSpec((d, d), lambda i, j, h: (0, 0)),
    ]
    args += [kc, vct, gate, x, mods, w_out]
    return pl.pallas_call(
        functools.partial(_attn_a_kernel, tq=tq, tk=tk, n_x_chunks=n_x_chunks),
        grid=(b, r // tq, N_PAIRS),
        in_specs=in_specs,
        out_specs=pl.BlockSpec((1, tq, d), qrow),
        out_shape=jax.ShapeDtypeStruct((b, r, d), F32),
        scratch_shapes=[
            pltpu.VMEM((N_PAIRS, tq, LANES), F32),
            *[pltpu.VMEM((max(tk, lc), 2 * tq), F32)] * N_SCORE_BUFS,
            *[pltpu.VMEM((max(tk, lc), 2 * tq), BF16)] * 2,
        ],
        compiler_params=_cparams(("parallel", "parallel", "arbitrary")),
        name="attn_a_x" if has_x else "attn_a_ctx",
    )(*args)


def _store_vt_chunks(v_ref, v):
    for c in range(v.shape[0] // NAT_CHUNK):
        v_ref[0, c] = v[c * NAT_CHUNK:(c + 1) * NAT_CHUNK].T.astype(BF16)


def _inproj_b_kernel(x_ref, ng_ref, sc_ref, sh_ref, w_ref, *out_refs, kv_only):
    h = _adaln(x_ref[0], ng_ref[...], sc_ref[0, 0], sh_ref[0, 0])
    p = jnp.dot(h.astype(BF16), w_ref[...], preferred_element_type=F32)
    if kv_only:
        k_ref, v_ref = out_refs
        k_ref[0] = p[:, :B_WIDTH].astype(BF16)
        _store_vt_chunks(v_ref, p[:, B_WIDTH:])
    else:
        q_ref, k_ref, v_ref, g_ref = out_refs
        q_ref[0] = (p[:, :B_WIDTH] * (ATTN_SCALE * LOG2E)).astype(BF16)
        k_ref[0] = p[:, B_WIDTH:2 * B_WIDTH].astype(BF16)
        _store_vt_chunks(v_ref, p[:, 2 * B_WIDTH:3 * B_WIDTH])
        g_ref[0] = _silu(p[:, 3 * B_WIDTH:]).astype(BF16)


def _inproj_b(x, ng, mods, w, tm, kv_only):
    b, r, d = x.shape
    n = w.shape[1]
    shared = mods.shape[0] == 1
    mod_idx = (lambda i, c: (0, c, 0, 0)) if shared else (lambda i, c: (i, c, 0, 0))
    row = lambda i, j: (i, j, 0)
    row_spec = pl.BlockSpec((1, tm, B_WIDTH), row)
    row_shape = jax.ShapeDtypeStruct((b, r, B_WIDTH), BF16)
    vt_spec = pl.BlockSpec((1, tm // NAT_CHUNK, B_WIDTH, NAT_CHUNK), lambda i, j: (i, j, 0, 0))
    vt_shape = jax.ShapeDtypeStruct((b, r // NAT_CHUNK, B_WIDTH, NAT_CHUNK), BF16)
    if kv_only:
        out_specs, out_shape = [row_spec, vt_spec], [row_shape, vt_shape]
    else:
        out_specs, out_shape = [row_spec, row_spec, vt_spec, row_spec], [row_shape, row_shape, vt_shape, row_shape]
    return pl.pallas_call(
        functools.partial(_inproj_b_kernel, kv_only=kv_only),
        grid=(b, r // tm),
        in_specs=[
            pl.BlockSpec((1, tm, d), row),
            pl.BlockSpec((1, d), lambda i, j: (0, 0)),
            pl.BlockSpec((1, 1, 1, d), lambda i, j: mod_idx(i, 1)),
            pl.BlockSpec((1, 1, 1, d), lambda i, j: mod_idx(i, 0)),
            pl.BlockSpec((d, n), lambda i, j: (0, 0)),
        ],
        out_specs=out_specs,
        out_shape=out_shape,
        compiler_params=_cparams(("parallel", "parallel")),
        name="inproj_b_ctx" if kv_only else "inproj_b_x",
    )(x, ng, mods, mods, w)


def _nat_scores(g, n_groups, q_ref, k_ref, kc_ref, tab_ref, s_scr, row0):
    g = jnp.asarray(g, jnp.int32)
    c0 = jnp.clip(g - 1, 0, n_groups - NAT_SPAN)
    kind = (g > 0).astype(jnp.int32) + (g == n_groups - 1).astype(jnp.int32)
    qs = pl.multiple_of(g * NAT_CHUNK, NAT_CHUNK)
    ks = pl.multiple_of(c0 * NAT_CHUNK, NAT_CHUNK)
    qt = q_ref[0, pl.ds(qs, NAT_CHUNK), :].astype(F32).T
    top = lax.broadcasted_iota(jnp.int32, qt.shape, 0) < HEAD_DIM
    a_t = jnp.concatenate([jnp.where(top, qt, 0.0), jnp.where(top, 0.0, qt)], axis=1).astype(BF16)
    s_span = jnp.dot(k_ref[0, pl.ds(ks, NAT_KEYS), :], a_t, preferred_element_type=F32) + tab_ref[0, kind]
    s_ctx = jnp.dot(kc_ref[0], a_t, preferred_element_type=F32)
    s_scr[0:NAT_KEYS, :] = s_span
    s_scr[NAT_KEYS:, :] = s_ctx
    _reread_barrier(s_scr, s_span, row0)
    return jnp.maximum(jnp.max(s_span, axis=0, keepdims=True), jnp.max(s_ctx, axis=0, keepdims=True))


def _nat_softmax(s_scr, p_scr, m):
    for j in range(s_scr.shape[1] // LANES):
        sl = slice(j * LANES, (j + 1) * LANES)
        for r in range(s_scr.shape[0] // EXP_ROWS):
            rows = slice(r * EXP_ROWS, (r + 1) * EXP_ROWS)
            p_scr[rows, sl] = jnp.exp2(s_scr[rows, sl] - m[:, sl]).astype(BF16)


def _nat_output(g, n_groups, vt_ref, vct_ref, p_scr, o_ref):
    nk = p_scr.shape[0]
    c0 = jnp.clip(g - 1, 0, n_groups - NAT_SPAN)
    qs = pl.multiple_of(g * NAT_CHUNK, NAT_CHUNK)
    v_t = jnp.concatenate([vt_ref[0, c0 + c] for c in range(NAT_SPAN)] + [vct_ref[0, 0]], axis=1)
    v_ext = jnp.concatenate([v_t, jnp.ones((SUM_ROWS, nk), BF16)], axis=0)
    acc = jnp.dot(v_ext, p_scr[...], preferred_element_type=F32)
    o_t = acc[:LANES] / acc[LANES:LANES + 1]
    both = jnp.concatenate([o_t[:HEAD_DIM, :NAT_CHUNK], o_t[HEAD_DIM:, NAT_CHUNK:]], axis=0)
    o_ref[0, pl.ds(qs, NAT_CHUNK), :] = both.T.astype(BF16)


def _natten_kernel(q_ref, k_ref, vt_ref, kc_ref, vct_ref, tab_ref, o_ref, s0, s1, p0, p1, *, n_groups):
    row0 = _run_time_zero(pl.program_id(0))

    def body(i, carry):
        for u, (s_scr, p_scr) in enumerate(((s0, p0), (s1, p1))):
            g = 2 * i + u
            m = _nat_scores(g, n_groups, q_ref, k_ref, kc_ref, tab_ref, s_scr, row0)
            _nat_softmax(s_scr, p_scr, m)
            _nat_output(g, n_groups, vt_ref, vct_ref, p_scr, o_ref)
        return carry

    lax.fori_loop(0, n_groups // 2, body, 0)


def _natten(q, k, vt, kc, vct, tab):
    b, t, d = q.shape
    lc = kc.shape[1]
    n_groups = t // NAT_CHUNK
    nk = NAT_KEYS + lc
    blk = lambda h, i: (i, 0, h)
    return pl.pallas_call(
        functools.partial(_natten_kernel, n_groups=n_groups),
        grid=(N_PAIRS, b),
        in_specs=[
            pl.BlockSpec((1, t, LANES), blk),
            pl.BlockSpec((1, t, LANES), blk),
            pl.BlockSpec((1, n_groups, LANES, NAT_CHUNK), lambda h, i: (i, 0, h, 0)),
            pl.BlockSpec((1, lc, LANES), blk),
            pl.BlockSpec((1, 1, LANES, lc), lambda h, i: (i, 0, h, 0)),
            pl.BlockSpec((1,) + tab.shape[1:], lambda h, i: (h, 0, 0, 0)),
        ],
        out_specs=pl.BlockSpec((1, t, LANES), blk),
        out_shape=jax.ShapeDtypeStruct((b, t, d), BF16),
        scratch_shapes=[pltpu.VMEM((nk, 2 * NAT_CHUNK), F32)] * 2 + [pltpu.VMEM((nk, 2 * NAT_CHUNK), BF16)] * 2,
        compiler_params=_cparams(("parallel", "parallel")),
        name="natten_b",
    )(q, k, vt, kc, vct, tab)


def _outproj_final_kernel(o_ref, gate_ref, x_ref, gx_ref, w_ref, fg_ref, out_ref):
    u = (o_ref[0].astype(F32) * gate_ref[0].astype(F32)).astype(BF16)
    y = jnp.dot(u, w_ref[...], preferred_element_type=F32)
    x2 = x_ref[0] + gx_ref[0, 0] * y
    ms = jnp.mean(x2 * x2, axis=-1, keepdims=True)
    out_ref[0] = x2 * lax.rsqrt(ms + NORM_EPS) * fg_ref[...]


def _outproj_final(o, gate, x, mods, w_out, fg, tm):
    b, t, d = x.shape
    row = lambda i, j: (i, j, 0)
    return pl.pallas_call(
        _outproj_final_kernel,
        grid=(b, t // tm),
        in_specs=[
            pl.BlockSpec((1, tm, d), row),
            pl.BlockSpec((1, tm, d), row),
            pl.BlockSpec((1, tm, d), row),
            pl.BlockSpec((1, 1, 1, d), lambda i, j: (i, 2, 0, 0)),
            pl.BlockSpec((d, d), lambda i, j: (0, 0)),
            pl.BlockSpec((1, d), lambda i, j: (0, 0)),
        ],
        out_specs=pl.BlockSpec((1, tm, d), row),
        out_shape=jax.ShapeDtypeStruct((b, t, d), F32),
        compiler_params=_cparams(("parallel", "parallel")),
        name="outproj_final",
    )(o, gate, x, mods, w_out, fg)


def _rope_tables(t_len):
    pos = jnp.arange(t_len, dtype=jnp.int32)
    row = (pos // GRID_W).astype(F32)
    col = (pos % GRID_W).astype(F32)
    inv = ROPE_THETA ** (-jnp.arange(0, ROPE_AXIS_DIM, 2, dtype=F32) / ROPE_AXIS_DIM)
    ang_r = row[:, None] * inv
    ang_c = col[:, None] * inv
    zero = jnp.zeros_like(ang_r)
    cos_h = jnp.concatenate([jnp.cos(ang_r), jnp.cos(ang_r), jnp.cos(ang_c), jnp.cos(ang_c)], axis=1)
    sup_h = jnp.concatenate([-jnp.sin(ang_r), zero, -jnp.sin(ang_c), zero], axis=1)
    sdn_h = jnp.concatenate([zero, jnp.sin(ang_r), zero, jnp.sin(ang_c)], axis=1)
    two = lambda a: jnp.concatenate([a, a], axis=1)
    return two(cos_h), two(sup_h), two(sdn_h)


def _natten_tables(rpb, rows):
    h = rpb.shape[0]
    qcol = np.arange(GRID_W)
    c0 = np.clip(qcol - WIN_C // 2, 0, GRID_W - WIN_C)
    kcol = np.arange(GRID_W)
    valid = (kcol[None, :] >= c0[:, None]) & (kcol[None, :] < c0[:, None] + WIN_C)
    pad = GRID_W - WIN_C
    padded = jnp.pad(rpb.astype(F32) * LOG2E, ((0, 0), (0, 0), (pad, pad)))
    toep = jnp.stack([padded[:, :, GRID_W - 1 - j:2 * GRID_W - 1 - j] for j in range(GRID_W)], axis=2)
    toep = jnp.where(jnp.asarray(valid)[None, None], toep, NEG_BIG)
    toep_t = jnp.swapaxes(toep, 2, 3)
    neg = jnp.full((h, GRID_W, GRID_W), NEG_BIG, F32)
    n_groups = rows // NAT_ROWS
    span_rows = NAT_SPAN * NAT_ROWS

    def plan(g):
        ks = int(np.clip(g - 1, 0, n_groups - NAT_SPAN)) * NAT_ROWS
        out = []
        for s in range(span_rows):
            for i in range(NAT_ROWS):
                rq, rk = NAT_ROWS * g + i, ks + s
                r0 = int(np.clip(rq - WIN_R // 2, 0, rows - WIN_R))
                out.append(rk - rq + WIN_R - 1 if r0 <= rk < r0 + WIN_R else None)
        return out

    kinds = [plan(0), plan(1), plan(n_groups - 1)]
    assert all(plan(g) == kinds[1] for g in range(1, n_groups - 1))
    blocks = jnp.stack([jnp.stack([neg if dr is None else toep_t[:, dr] for dr in kind]) for kind in kinds])
    blocks = blocks.reshape(3, span_rows, NAT_ROWS, h // 2, 2, GRID_W, GRID_W).transpose(3, 0, 1, 5, 4, 2, 6)
    return blocks.reshape(h // 2, 3, span_rows * GRID_W, 2 * NAT_ROWS * GRID_W)


def kernel(x, c, ctx, c_ctx, norm_g, w_mod, b_mod, a_w_in, a_q_norm_g, a_k_norm_g, a_w_out,
           b_w_in, b_rpb, b_w_out, final_norm_g):
    bsz, t, d = x.shape
    lc = ctx.shape[1]

    n_rows = ((bsz + 1 + 7) // 8) * 8
    c_rows = jnp.zeros((n_rows, d), F32).at[:bsz].set(c).at[bsz].set(c_ctx)
    mods = _modulation(c_rows, w_mod, b_mod).reshape(w_mod.shape[0], n_rows, 3, 1, d)

    cos_t, sup, sdn = _rope_tables(t)
    one_t = jnp.ones((lc, LANES), F32)
    zero_t = jnp.zeros((lc, LANES), F32)
    head_block = np.kron(np.eye(LANES // HEAD_DIM), np.ones((HEAD_DIM, HEAD_DIM)))
    bd = jnp.asarray(head_block, BF16)
    two = lambda g: jnp.concatenate([g, g]).reshape(1, LANES).astype(F32)

    mx, mc = mods[0, :bsz], mods[0, bsz:bsz + 1]
    ng = norm_g[0].reshape(1, d)
    w_in = a_w_in[0].astype(BF16)
    qg, kg = two(a_q_norm_g[0]) * (ATTN_SCALE * LOG2E), two(a_k_norm_g[0])
    q, k, vt, gate = _inproj_a(x, ng, mx, w_in, qg, kg, bd, cos_t, sup, sdn, tm=512)
    qc, kc, vct, gate_c = _inproj_a(ctx, ng, mc, w_in, qg, kg, bd, one_t, zero_t, zero_t, tm=lc)
    w_out = a_w_out[0].astype(BF16)
    x1 = _attn_a(q, k, vt, kc, vct, gate, x, mx, w_out, tq=512)
    ctx1 = _attn_a(qc, None, None, kc, vct, gate_c, ctx, mc, w_out, tq=lc)

    mx, mc = mods[1, :bsz], mods[1, bsz:bsz + 1]
    ng = norm_g[1].reshape(1, d)
    w_in = b_w_in[0].astype(BF16)
    q, k, vt, gate = _inproj_b(x1, ng, mx, w_in, tm=256, kv_only=False)
    kc, vct = _inproj_b(ctx1, ng, mc, w_in[:, B_WIDTH:3 * B_WIDTH], tm=lc, kv_only=True)
    o = _natten(q, k, vt, kc, vct, _natten_tables(b_rpb[0], t // GRID_W))
    return _outproj_final(o, gate, x1, mx, b_w_out[0].astype(BF16), final_norm_g.reshape(1, d), tm=512)
```

```python
import functools

import jax
import jax.numpy as jnp
import numpy as np
from jax import lax
from jax.experimental import pallas as pl
from jax.experimental.pallas import tpu as pltpu

F32 = jnp.float32
BF16 = jnp.bfloat16

LANES = 128
SUBLANES = 8
VMEM_LIMIT = 56 * 1024 * 1024

D_MODEL = 1024
GRID_W = 64
HEAD_DIM = 64
NORM_EPS = 1e-6
ATTN_SCALE = HEAD_DIM ** -0.5
A_HEADS = 16
A_KV_HEADS = 4
A_WIDTH = A_HEADS * HEAD_DIM
A_KV_WIDTH = A_KV_HEADS * HEAD_DIM
ROPE_THETA = 10000.0
ROPE_AXIS_DIM = HEAD_DIM // 2
ROPE_HALF = ROPE_AXIS_DIM // 2
B_HEADS = 16
B_WIDTH = B_HEADS * HEAD_DIM
WIN_R = 8
WIN_C = 16
N_PAIRS = D_MODEL // LANES
NEG_BIG = -1e30
EXP_ROWS = 64
SUM_ROWS = 16
NAT_ROWS = 4
NAT_CHUNK = NAT_ROWS * GRID_W
NAT_SPAN = 3
NAT_KEYS = NAT_SPAN * NAT_CHUNK
N_SCORE_BUFS = 3
LOG2E = 1.4426950408889634


def _cparams(sem, flags=None):
    return pltpu.CompilerParams(dimension_semantics=sem, vmem_limit_bytes=VMEM_LIMIT, flags=flags)


def _mod_kernel(c_ref, w_ref, b_ref, o_ref):
    c = c_ref[...]
    s = c * jax.nn.sigmoid(c)
    o_ref[0] = jnp.dot(s, w_ref[0], precision=lax.Precision.HIGHEST,
                       preferred_element_type=F32) + b_ref[0]


def _modulation(c_rows, w_mod, b_mod):
    depth, d, n = w_mod.shape
    rows = c_rows.shape[0]
    tn = 1024
    return pl.pallas_call(
        _mod_kernel,
        grid=(depth, n // tn),
        in_specs=[
            pl.BlockSpec((rows, d), lambda l, j: (0, 0)),
            pl.BlockSpec((1, d, tn), lambda l, j: (l, 0, j)),
            pl.BlockSpec((1, 1, tn), lambda l, j: (l, 0, j)),
        ],
        out_specs=pl.BlockSpec((1, rows, tn), lambda l, j: (l, 0, j)),
        out_shape=jax.ShapeDtypeStruct((depth, rows, n), F32),
        compiler_params=_cparams(("arbitrary", "arbitrary")),
        name="adaln_mod",
    )(c_rows, w_mod, b_mod.reshape(depth, 1, n))


def _adaln(x, ng, sc, sh):
    ms = jnp.mean(x * x, axis=-1, keepdims=True)
    y = x * lax.rsqrt(ms + NORM_EPS) * ng
    return y * (1.0 + sc) + sh


def _silu(z):
    return z * jax.nn.sigmoid(z)


def _head_norm_rope(blk, gain, bd, cos_t, sin_up, sin_dn):
    sq = blk * blk
    hi = sq.astype(BF16)
    lo = (sq - hi.astype(F32)).astype(BF16)
    ssum = (jnp.dot(hi, bd, preferred_element_type=F32) + jnp.dot(lo, bd, preferred_element_type=F32))
    n = blk * lax.rsqrt(ssum * (1.0 / HEAD_DIM) + NORM_EPS) * gain
    up = pltpu.roll(n, LANES - ROPE_HALF, 1)
    dn = pltpu.roll(n, ROPE_HALF, 1)
    return n * cos_t + up * sin_up + dn * sin_dn


def _inproj_a_kernel(x_ref, ng_ref, sc_ref, sh_ref, w_ref, qg_ref, kg_ref, bd_ref, cos_ref, sup_ref, sdn_ref,
                     q_ref, k_ref, v_ref, g_ref):
    h = _adaln(x_ref[0], ng_ref[...], sc_ref[0, 0], sh_ref[0, 0])
    p = jnp.dot(h.astype(BF16), w_ref[...], preferred_element_type=F32)
    bd = bd_ref[...]
    cos_t, sup, sdn = cos_ref[...], sup_ref[...], sdn_ref[...]
    qg, kg = qg_ref[...], kg_ref[...]
    for j in range(A_WIDTH // LANES):
        blk = p[:, j * LANES:(j + 1) * LANES]
        q_ref[0, :, j * LANES:(j + 1) * LANES] = _head_norm_rope(blk, qg, bd, cos_t, sup, sdn).astype(BF16)
    for j in range(A_KV_WIDTH // LANES):
        blk = p[:, A_WIDTH + j * LANES:A_WIDTH + (j + 1) * LANES]
        k_ref[0, :, j * LANES:(j + 1) * LANES] = _head_norm_rope(blk, kg, bd, cos_t, sup, sdn).astype(BF16)
    v_ref[0, 0] = p[:, A_WIDTH + A_KV_WIDTH:A_WIDTH + 2 * A_KV_WIDTH].T.astype(BF16)
    g_ref[0] = _silu(p[:, A_WIDTH + 2 * A_KV_WIDTH:]).astype(BF16)


def _inproj_a(x, ng, mods, w, qg, kg, bd, cos_t, sup, sdn, tm):
    b, r, d = x.shape
    n = w.shape[1]
    shared = mods.shape[0] == 1
    mod_idx = (lambda i, j, c: (0, c, 0, 0)) if shared else (lambda i, j, c: (i, c, 0, 0))
    row = lambda i, j: (i, j, 0)
    const2 = lambda i, j: (0, 0)
    return pl.pallas_call(
        _inproj_a_kernel,
        grid=(b, r // tm),
        in_specs=[
            pl.BlockSpec((1, tm, d), row),
            pl.BlockSpec((1, d), const2),
            pl.BlockSpec((1, 1, 1, d), lambda i, j: mod_idx(i, j, 1)),
            pl.BlockSpec((1, 1, 1, d), lambda i, j: mod_idx(i, j, 0)),
            pl.BlockSpec((d, n), const2),
            pl.BlockSpec((1, LANES), const2),
            pl.BlockSpec((1, LANES), const2),
            pl.BlockSpec((LANES, LANES), const2),
            pl.BlockSpec((tm, LANES), lambda i, j: (j, 0)),
            pl.BlockSpec((tm, LANES), lambda i, j: (j, 0)),
            pl.BlockSpec((tm, LANES), lambda i, j: (j, 0)),
        ],
        out_specs=[
            pl.BlockSpec((1, tm, A_WIDTH), row),
            pl.BlockSpec((1, tm, A_KV_WIDTH), row),
            pl.BlockSpec((1, 1, A_KV_WIDTH, tm), lambda i, j: (i, j, 0, 0)),
            pl.BlockSpec((1, tm, A_WIDTH), row),
        ],
        out_shape=[
            jax.ShapeDtypeStruct((b, r, A_WIDTH), BF16),
            jax.ShapeDtypeStruct((b, r, A_KV_WIDTH), BF16),
            jax.ShapeDtypeStruct((b, r // tm, A_KV_WIDTH, tm), BF16),
            jax.ShapeDtypeStruct((b, r, A_WIDTH), BF16),
        ],
        compiler_params=_cparams(("parallel", "parallel")),
        name="inproj_a",
    )(x, ng, mods, mods, w, qg, kg, bd, cos_t, sup, sdn)


def _stack_qt(q2, half):
    qt = q2.astype(F32).T
    zero = jnp.zeros((HEAD_DIM, qt.shape[1]), F32)
    first = half == 0

    def place(h_t):
        return jnp.where(first, jnp.concatenate([h_t, zero], axis=0), jnp.concatenate([zero, h_t], axis=0))

    return jnp.concatenate([place(qt[:HEAD_DIM]), place(qt[HEAD_DIM:])], axis=1).astype(BF16)


def _run_time_zero(step):
    return pl.multiple_of((step // N_PAIRS) * SUBLANES, SUBLANES)


def _store_cols(scr, row_start, val):
    n = val.shape[0]
    for j in range(scr.shape[0]):
        scr[j, row_start:row_start + n, :] = val[:, j * LANES:(j + 1) * LANES]


def _load_cols(scr, nk):
    return jnp.concatenate([scr[j, 0:nk, :] for j in range(scr.shape[0])], axis=1)


def _reread_barrier(s_scr, s, row0):
    s_scr[0, pl.ds(row0, SUBLANES), :] = s[0:SUBLANES, 0:LANES]


def _softmax_cols(s_scr, p_scr, nk, m, chunk_max):
    m_new = jnp.maximum(m, chunk_max)
    for j in range(s_scr.shape[0]):
        sl = slice(j * LANES, (j + 1) * LANES)
        for r in range(nk // EXP_ROWS):
            rows = slice(r * EXP_ROWS, (r + 1) * EXP_ROWS)
            p_scr[j, rows, :] = jnp.exp2(s_scr[j, rows, :] - m_new[:, sl]).astype(BF16)
    return m_new, jnp.exp2(m - m_new)


def _attn_a_kernel(*refs, tq, tk, n_x_chunks):
    n_in = 10 if n_x_chunks else 8
    if n_x_chunks:
        q_ref, kx_ref, vxt_ref, kc_ref, vct_ref, gate_ref, x_ref, gx_ref, w_ref, out_ref = refs[:n_in]
    else:
        q_ref, kc_ref, vct_ref, gate_ref, x_ref, gx_ref, w_ref, out_ref = refs[:n_in]
    o_scr = refs[n_in]
    s_bufs = refs[n_in + 1:n_in + 1 + N_SCORE_BUFS]
    p_bufs = refs[n_in + 1 + N_SCORE_BUFS:]
    hp = pl.program_id(2)
    row0 = _run_time_zero(hp)
    kv_head = hp // 2
    vrow = pl.multiple_of(kv_head * HEAD_DIM, HEAD_DIM)
    a_t = _stack_qt(q_ref[0], kv_head % 2)
    cols = 2 * tq
    lc = kc_ref.shape[1]

    n_chunks = 1 + n_x_chunks
    chunk_rows = lambda i: lc if i == 0 else tk
    k_blk = lambda i: kc_ref[0] if i == 0 else kx_ref[0, (i - 1) * tk:i * tk, :]
    v_blk = lambda i: (vct_ref[0, 0, pl.ds(vrow, HEAD_DIM), :] if i == 0
                       else vxt_ref[0, i - 1, pl.ds(vrow, HEAD_DIM), :])

    def scores(i):
        nk = chunk_rows(i)
        s_scr = s_bufs[i % N_SCORE_BUFS]
        s = jnp.dot(k_blk(i), a_t, preferred_element_type=F32)
        _store_cols(s_scr, 0, s)
        _reread_barrier(s_scr, s, row0)
        return jnp.max(s, axis=0, keepdims=True)

    def consume(i, carry):
        m, acc = carry
        nk = chunk_rows(i)
        p_scr = p_bufs[i % len(p_bufs)]
        m, alpha = _softmax_cols(s_bufs[i % N_SCORE_BUFS], p_scr, nk, m, cmax[i])
        v_ext = jnp.concatenate([v_blk(i), jnp.ones((SUM_ROWS, nk), BF16)], axis=0)
        acc = alpha * acc + jnp.dot(v_ext, _load_cols(p_scr, nk), preferred_element_type=F32)
        return m, acc

    carry = (jnp.full((1, cols), NEG_BIG, F32), jnp.zeros((HEAD_DIM + SUM_ROWS, cols), F32))
    cmax = {0: scores(0)}
    for i in range(n_chunks):
        if i + 1 < n_chunks:
            cmax[i + 1] = scores(i + 1)
        carry = consume(i, carry)
    _, acc = carry
    o_t = acc[:HEAD_DIM] / acc[HEAD_DIM:HEAD_DIM + 1]
    o_scr[hp] = jnp.concatenate([o_t[:, :tq], o_t[:, tq:]], axis=0).T

    @pl.when(hp == N_PAIRS - 1)
    def _():
        o_full = jnp.concatenate([o_scr[j] for j in range(N_PAIRS)], axis=1)
        u = (o_full * gate_ref[0].astype(F32)).astype(BF16)
        y = jnp.dot(u, w_ref[...], preferred_element_type=F32)
        out_ref[0] = x_ref[0] + gx_ref[0, 0] * y


def _attn_a(q, kx, vxt, kc, vct, gate, x, mods, w_out, tq):
    b, r, d = x.shape
    has_x = kx is not None
    shared = mods.shape[0] == 1
    qrow = lambda i, j, h: (i, j, 0)
    kvp = lambda i, j, h: (i, 0, h // 4)
    whole = lambda i, j, h: (i, 0, 0, 0)
    in_specs = [pl.BlockSpec((1, tq, LANES), lambda i, j, h: (i, j, h))]
    args = [q]
    lc = kc.shape[1]
    tk = lc
    n_x_chunks = 0
    if has_x:
        t = kx.shape[1]
        n_x_chunks, tk = vxt.shape[1], vxt.shape[3]
        in_specs += [pl.BlockSpec((1, t, LANES), kvp), pl.BlockSpec((1,) + vxt.shape[1:], whole)]
        args += [kx, vxt]
    in_specs += [
        pl.BlockSpec((1, lc, LANES), kvp),
        pl.BlockSpec((1,) + vct.shape[1:], whole),
        pl.BlockSpec((1, tq, d), qrow),
        pl.BlockSpec((1, tq, d), qrow),
        pl.BlockSpec((1, 1, 1, d), (lambda i, j, h: (0, 2, 0, 0)) if shared else (lambda i, j, h: (i, 2, 0, 0))),
        pl.BlockSpec((d, d), lambda i, j, h: (0, 0)),
    ]
    args += [kc, vct, gate, x, mods, w_out]
    return pl.pallas_call(
        functools.partial(_attn_a_kernel, tq=tq, tk=tk, n_x_chunks=n_x_chunks),
        grid=(b, r // tq, N_PAIRS),
        in_specs=in_specs,
        out_specs=pl.BlockSpec((1, tq, d), qrow),
        out_shape=jax.ShapeDtypeStruct((b, r, d), F32),
        scratch_shapes=[
            pltpu.VMEM((N_PAIRS, tq, LANES), F32),
            *[pltpu.VMEM((2 * tq // LANES, max(tk, lc), LANES), F32)] * N_SCORE_BUFS,
            *[pltpu.VMEM((2 * tq // LANES, max(tk, lc), LANES), BF16)] * 2,
        ],
        compiler_params=_cparams(("parallel", "parallel", "arbitrary")),
        name="attn_a_x" if has_x else "attn_a_ctx",
    )(*args)


def _store_vt_chunks(v_ref, v):
    for c in range(v.shape[0] // NAT_CHUNK):
        v_ref[0, c] = v[c * NAT_CHUNK:(c + 1) * NAT_CHUNK].T.astype(BF16)


def _inproj_b_kernel(x_ref, ng_ref, sc_ref, sh_ref, w_ref, *out_refs, kv_only):
    h = _adaln(x_ref[0], ng_ref[...], sc_ref[0, 0], sh_ref[0, 0])
    p = jnp.dot(h.astype(BF16), w_ref[...], preferred_element_type=F32)
    if kv_only:
        k_ref, v_ref = out_refs
        k_ref[0] = p[:, :B_WIDTH].astype(BF16)
        _store_vt_chunks(v_ref, p[:, B_WIDTH:])
    else:
        q_ref, k_ref, v_ref, g_ref = out_refs
        q_ref[0] = (p[:, :B_WIDTH] * (ATTN_SCALE * LOG2E)).astype(BF16)
        k_ref[0] = p[:, B_WIDTH:2 * B_WIDTH].astype(BF16)
        _store_vt_chunks(v_ref, p[:, 2 * B_WIDTH:3 * B_WIDTH])
        g_ref[0] = _silu(p[:, 3 * B_WIDTH:]).astype(BF16)


def _inproj_b(x, ng, mods, w, tm, kv_only):
    b, r, d = x.shape
    n = w.shape[1]
    shared = mods.shape[0] == 1
    mod_idx = (lambda i, c: (0, c, 0, 0)) if shared else (lambda i, c: (i, c, 0, 0))
    row = lambda i, j: (i, j, 0)
    row_spec = pl.BlockSpec((1, tm, B_WIDTH), row)
    row_shape = jax.ShapeDtypeStruct((b, r, B_WIDTH), BF16)
    vt_spec = pl.BlockSpec((1, tm // NAT_CHUNK, B_WIDTH, NAT_CHUNK), lambda i, j: (i, j, 0, 0))
    vt_shape = jax.ShapeDtypeStruct((b, r // NAT_CHUNK, B_WIDTH, NAT_CHUNK), BF16)
    if kv_only:
        out_specs, out_shape = [row_spec, vt_spec], [row_shape, vt_shape]
    else:
        out_specs, out_shape = [row_spec, row_spec, vt_spec, row_spec], [row_shape, row_shape, vt_shape, row_shape]
    return pl.pallas_call(
        functools.partial(_inproj_b_kernel, kv_only=kv_only),
        grid=(b, r // tm),
        in_specs=[
            pl.BlockSpec((1, tm, d), row),
            pl.BlockSpec((1, d), lambda i, j: (0, 0)),
            pl.BlockSpec((1, 1, 1, d), lambda i, j: mod_idx(i, 1)),
            pl.BlockSpec((1, 1, 1, d), lambda i, j: mod_idx(i, 0)),
            pl.BlockSpec((d, n), lambda i, j: (0, 0)),
        ],
        out_specs=out_specs,
        out_shape=out_shape,
        compiler_params=_cparams(("parallel", "parallel")),
        name="inproj_b_ctx" if kv_only else "inproj_b_x",
    )(x, ng, mods, mods, w)


def _nat_scores(g, n_groups, q_ref, k_ref, kc_ref, tab_ref, s_scr, row0):
    g = jnp.asarray(g, jnp.int32)
    c0 = jnp.clip(g - 1, 0, n_groups - NAT_SPAN)
    kind = (g > 0).astype(jnp.int32) + (g == n_groups - 1).astype(jnp.int32)
    qs = pl.multiple_of(g * NAT_CHUNK, NAT_CHUNK)
    ks = pl.multiple_of(c0 * NAT_CHUNK, NAT_CHUNK)
    qt = q_ref[0, pl.ds(qs, NAT_CHUNK), :].astype(F32).T
    top = lax.broadcasted_iota(jnp.int32, qt.shape, 0) < HEAD_DIM
    a_t = jnp.concatenate([jnp.where(top, qt, 0.0), jnp.where(top, 0.0, qt)], axis=1).astype(BF16)
    s_span = jnp.dot(k_ref[0, pl.ds(ks, NAT_KEYS), :], a_t, preferred_element_type=F32) + tab_ref[0, kind]
    s_ctx = jnp.dot(kc_ref[0], a_t, preferred_element_type=F32)
    _store_cols(s_scr, 0, s_span)
    _store_cols(s_scr, NAT_KEYS, s_ctx)
    _reread_barrier(s_scr, s_span, row0)
    return jnp.maximum(jnp.max(s_span, axis=0, keepdims=True), jnp.max(s_ctx, axis=0, keepdims=True))


def _nat_softmax(s_scr, p_scr, m):
    for j in range(s_scr.shape[0]):
        sl = slice(j * LANES, (j + 1) * LANES)
        for r in range(s_scr.shape[1] // EXP_ROWS):
            rows = slice(r * EXP_ROWS, (r + 1) * EXP_ROWS)
            p_scr[j, rows, :] = jnp.exp2(s_scr[j, rows, :] - m[:, sl]).astype(BF16)


def _nat_output(g, n_groups, vt_ref, vct_ref, p_scr, o_ref):
    nk = p_scr.shape[1]
    c0 = jnp.clip(g - 1, 0, n_groups - NAT_SPAN)
    qs = pl.multiple_of(g * NAT_CHUNK, NAT_CHUNK)
    v_t = jnp.concatenate([vt_ref[0, c0 + c] for c in range(NAT_SPAN)] + [vct_ref[0, 0]], axis=1)
    v_ext = jnp.concatenate([v_t, jnp.ones((SUM_ROWS, nk), BF16)], axis=0)
    acc = jnp.dot(v_ext, _load_cols(p_scr, nk), preferred_element_type=F32)
    o_t = acc[:LANES] / acc[LANES:LANES + 1]
    both = jnp.concatenate([o_t[:HEAD_DIM, :NAT_CHUNK], o_t[HEAD_DIM:, NAT_CHUNK:]], axis=0)
    o_ref[0, pl.ds(qs, NAT_CHUNK), :] = both.T.astype(BF16)


def _natten_kernel(q_ref, k_ref, vt_ref, kc_ref, vct_ref, tab_ref, o_ref, s0, s1, p0, p1, *, n_groups):
    row0 = _run_time_zero(pl.program_id(0))

    def body(i, carry):
        for u, (s_scr, p_scr) in enumerate(((s0, p0), (s1, p1))):
            g = 2 * i + u
            m = _nat_scores(g, n_groups, q_ref, k_ref, kc_ref, tab_ref, s_scr, row0)
            _nat_softmax(s_scr, p_scr, m)
            _nat_output(g, n_groups, vt_ref, vct_ref, p_scr, o_ref)
        return carry

    lax.fori_loop(0, n_groups // 2, body, 0)


def _natten(q, k, vt, kc, vct, tab):
    b, t, d = q.shape
    lc = kc.shape[1]
    n_groups = t // NAT_CHUNK
    nk = NAT_KEYS + lc
    blk = lambda h, i: (i, 0, h)
    return pl.pallas_call(
        functools.partial(_natten_kernel, n_groups=n_groups),
        grid=(N_PAIRS, b),
        in_specs=[
            pl.BlockSpec((1, t, LANES), blk),
            pl.BlockSpec((1, t, LANES), blk),
            pl.BlockSpec((1, n_groups, LANES, NAT_CHUNK), lambda h, i: (i, 0, h, 0)),
            pl.BlockSpec((1, lc, LANES), blk),
            pl.BlockSpec((1, 1, LANES, lc), lambda h, i: (i, 0, h, 0)),
            pl.BlockSpec((1,) + tab.shape[1:], lambda h, i: (h, 0, 0, 0)),
        ],
        out_specs=pl.BlockSpec((1, t, LANES), blk),
        out_shape=jax.ShapeDtypeStruct((b, t, d), BF16),
        scratch_shapes=[pltpu.VMEM((2 * NAT_CHUNK // LANES, nk, LANES), F32)] * 2
        + [pltpu.VMEM((2 * NAT_CHUNK // LANES, nk, LANES), BF16)] * 2,
        compiler_params=_cparams(("parallel", "parallel")),
        name="natten_b",
    )(q, k, vt, kc, vct, tab)


def _outproj_final_kernel(o_ref, gate_ref, x_ref, gx_ref, w_ref, fg_ref, out_ref):
    u = (o_ref[0].astype(F32) * gate_ref[0].astype(F32)).astype(BF16)
    y = jnp.dot(u, w_ref[...], preferred_element_type=F32)
    x2 = x_ref[0] + gx_ref[0, 0] * y
    ms = jnp.mean(x2 * x2, axis=-1, keepdims=True)
    out_ref[0] = x2 * lax.rsqrt(ms + NORM_EPS) * fg_ref[...]


def _outproj_final(o, gate, x, mods, w_out, fg, tm):
    b, t, d = x.shape
    row = lambda i, j: (i, j, 0)
    return pl.pallas_call(
        _outproj_final_kernel,
        grid=(b, t // tm),
        in_specs=[
            pl.BlockSpec((1, tm, d), row),
            pl.BlockSpec((1, tm, d), row),
            pl.BlockSpec((1, tm, d), row),
            pl.BlockSpec((1, 1, 1, d), lambda i, j: (i, 2, 0, 0)),
            pl.BlockSpec((d, d), lambda i, j: (0, 0)),
            pl.BlockSpec((1, d), lambda i, j: (0, 0)),
        ],
        out_specs=pl.BlockSpec((1, tm, d), row),
        out_shape=jax.ShapeDtypeStruct((b, t, d), F32),
        compiler_params=_cparams(("parallel", "parallel")),
        name="outproj_final",
    )(o, gate, x, mods, w_out, fg)


def _rope_tables(t_len):
    pos = jnp.arange(t_len, dtype=jnp.int32)
    row = (pos // GRID_W).astype(F32)
    col = (pos % GRID_W).astype(F32)
    inv = ROPE_THETA ** (-jnp.arange(0, ROPE_AXIS_DIM, 2, dtype=F32) / ROPE_AXIS_DIM)
    ang_r = row[:, None] * inv
    ang_c = col[:, None] * inv
    zero = jnp.zeros_like(ang_r)
    cos_h = jnp.concatenate([jnp.cos(ang_r), jnp.cos(ang_r), jnp.cos(ang_c), jnp.cos(ang_c)], axis=1)
    sup_h = jnp.concatenate([-jnp.sin(ang_r), zero, -jnp.sin(ang_c), zero], axis=1)
    sdn_h = jnp.concatenate([zero, jnp.sin(ang_r), zero, jnp.sin(ang_c)], axis=1)
    two = lambda a: jnp.concatenate([a, a], axis=1)
    return two(cos_h), two(sup_h), two(sdn_h)


def _natten_tables(rpb, rows):
    h = rpb.shape[0]
    qcol = np.arange(GRID_W)
    c0 = np.clip(qcol - WIN_C // 2, 0, GRID_W - WIN_C)
    kcol = np.arange(GRID_W)
    valid = (kcol[None, :] >= c0[:, None]) & (kcol[None, :] < c0[:, None] + WIN_C)
    pad = GRID_W - WIN_C
    padded = jnp.pad(rpb.astype(F32) * LOG2E, ((0, 0), (0, 0), (pad, pad)))
    toep = jnp.stack([padded[:, :, GRID_W - 1 - j:2 * GRID_W - 1 - j] for j in range(GRID_W)], axis=2)
    toep = jnp.where(jnp.asarray(valid)[None, None], toep, NEG_BIG)
    toep_t = jnp.swapaxes(toep, 2, 3)
    neg = jnp.full((h, GRID_W, GRID_W), NEG_BIG, F32)
    n_groups = rows // NAT_ROWS
    span_rows = NAT_SPAN * NAT_ROWS

    def plan(g):
        ks = int(np.clip(g - 1, 0, n_groups - NAT_SPAN)) * NAT_ROWS
        out = []
        for s in range(span_rows):
            for i in range(NAT_ROWS):
                rq, rk = NAT_ROWS * g + i, ks + s
                r0 = int(np.clip(rq - WIN_R // 2, 0, rows - WIN_R))
                out.append(rk - rq + WIN_R - 1 if r0 <= rk < r0 + WIN_R else None)
        return out

    kinds = [plan(0), plan(1), plan(n_groups - 1)]
    assert all(plan(g) == kinds[1] for g in range(1, n_groups - 1))
    blocks = jnp.stack([jnp.stack([neg if dr is None else toep_t[:, dr] for dr in kind]) for kind in kinds])
    blocks = blocks.reshape(3, span_rows, NAT_ROWS, h // 2, 2, GRID_W, GRID_W).transpose(3, 0, 1, 5, 4, 2, 6)
    return blocks.reshape(h // 2, 3, span_rows * GRID_W, 2 * NAT_ROWS * GRID_W)


def kernel(x, c, ctx, c_ctx, norm_g, w_mod, b_mod, a_w_in, a_q_norm_g, a_k_norm_g, a_w_out,
           b_w_in, b_rpb, b_w_out, final_norm_g):
    bsz, t, d = x.shape
    lc = ctx.shape[1]

    n_rows = ((bsz + 1 + 7) // 8) * 8
    c_rows = jnp.zeros((n_rows, d), F32).at[:bsz].set(c).at[bsz].set(c_ctx)
    mods = _modulation(c_rows, w_mod, b_mod).reshape(w_mod.shape[0], n_rows, 3, 1, d)

    cos_t, sup, sdn = _rope_tables(t)
    one_t = jnp.ones((lc, LANES), F32)
    zero_t = jnp.zeros((lc, LANES), F32)
    head_block = np.kron(np.eye(LANES // HEAD_DIM), np.ones((HEAD_DIM, HEAD_DIM)))
    bd = jnp.asarray(head_block, BF16)
    two = lambda g: jnp.concatenate([g, g]).reshape(1, LANES).astype(F32)

    mx, mc = mods[0, :bsz], mods[0, bsz:bsz + 1]
    ng = norm_g[0].reshape(1, d)
    w_in = a_w_in[0].astype(BF16)
    qg, kg = two(a_q_norm_g[0]) * (ATTN_SCALE * LOG2E), two(a_k_norm_g[0])
    q, k, vt, gate = _inproj_a(x, ng, mx, w_in, qg, kg, bd, cos_t, sup, sdn, tm=512)
    qc, kc, vct, gate_c = _inproj_a(ctx, ng, mc, w_in, qg, kg, bd, one_t, zero_t, zero_t, tm=lc)
    w_out = a_w_out[0].astype(BF16)
    x1 = _attn_a(q, k, vt, kc, vct, gate, x, mx, w_out, tq=512)
    ctx1 = _attn_a(qc, None, None, kc, vct, gate_c, ctx, mc, w_out, tq=lc)

    mx, mc = mods[1, :bsz], mods[1, bsz:bsz + 1]
    ng = norm_g[1].reshape(1, d)
    w_in = b_w_in[0].astype(BF16)
    q, k, vt, gate = _inproj_b(x1, ng, mx, w_in, tm=256, kv_only=False)
    kc, vct = _inproj_b(ctx1, ng, mc, w_in[:, B_WIDTH:3 * B_WIDTH], tm=lc, kv_only=True)
    o = _natten(q, k, vt, kc, vct, _natten_tables(b_rpb[0], t // GRID_W))
    return _outproj_final(o, gate, x1, mx, b_w_out[0].astype(BF16), final_norm_g.reshape(1, d), tm=512)
```

```python
import functools

import jax
import jax.numpy as jnp
import numpy as np
from jax import lax
from jax.experimental import pallas as pl
from jax.experimental.pallas import tpu as pltpu

F32 = jnp.float32
BF16 = jnp.bfloat16

LANES = 128
SUBLANES = 8
VMEM_LIMIT = 56 * 1024 * 1024

D_MODEL = 1024
GRID_W = 64
HEAD_DIM = 64
NORM_EPS = 1e-6
ATTN_SCALE = HEAD_DIM ** -0.5
A_HEADS = 16
A_KV_HEADS = 4
A_WIDTH = A_HEADS * HEAD_DIM
A_KV_WIDTH = A_KV_HEADS * HEAD_DIM
ROPE_THETA = 10000.0
ROPE_AXIS_DIM = HEAD_DIM // 2
ROPE_HALF = ROPE_AXIS_DIM // 2
B_HEADS = 16
B_WIDTH = B_HEADS * HEAD_DIM
WIN_R = 8
WIN_C = 16
N_PAIRS = D_MODEL // LANES
NEG_BIG = -1e30
MAX_ROWS = 128
EXP_ROWS = 64
SUM_ROWS = 16
NAT_ROWS = 4
NAT_CHUNK = NAT_ROWS * GRID_W
NAT_SPAN = 3
NAT_KEYS = NAT_SPAN * NAT_CHUNK
N_SCORE_BUFS = 3
LOG2E = 1.4426950408889634


def _cparams(sem, flags=None):
    return pltpu.CompilerParams(dimension_semantics=sem, vmem_limit_bytes=VMEM_LIMIT, flags=flags)


def _mod_kernel(c_ref, w_ref, b_ref, o_ref):
    c = c_ref[...]
    s = c * jax.nn.sigmoid(c)
    o_ref[0] = jnp.dot(s, w_ref[0], precision=lax.Precision.HIGHEST,
                       preferred_element_type=F32) + b_ref[0]


def _modulation(c_rows, w_mod, b_mod):
    depth, d, n = w_mod.shape
    rows = c_rows.shape[0]
    tn = 1024
    return pl.pallas_call(
        _mod_kernel,
        grid=(depth, n // tn),
        in_specs=[
            pl.BlockSpec((rows, d), lambda l, j: (0, 0)),
            pl.BlockSpec((1, d, tn), lambda l, j: (l, 0, j)),
            pl.BlockSpec((1, 1, tn), lambda l, j: (l, 0, j)),
        ],
        out_specs=pl.BlockSpec((1, rows, tn), lambda l, j: (l, 0, j)),
        out_shape=jax.ShapeDtypeStruct((depth, rows, n), F32),
        compiler_params=_cparams(("arbitrary", "arbitrary")),
        name="adaln_mod",
    )(c_rows, w_mod, b_mod.reshape(depth, 1, n))


def _adaln(x, ng, sc, sh):
    ms = jnp.mean(x * x, axis=-1, keepdims=True)
    y = x * lax.rsqrt(ms + NORM_EPS) * ng
    return y * (1.0 + sc) + sh


def _silu(z):
    return z * jax.nn.sigmoid(z)


def _head_norm_rope(blk, gain, bd, cos_t, sin_up, sin_dn):
    sq = blk * blk
    hi = sq.astype(BF16)
    lo = (sq - hi.astype(F32)).astype(BF16)
    ssum = (jnp.dot(hi, bd, preferred_element_type=F32) + jnp.dot(lo, bd, preferred_element_type=F32))
    n = blk * lax.rsqrt(ssum * (1.0 / HEAD_DIM) + NORM_EPS) * gain
    up = pltpu.roll(n, LANES - ROPE_HALF, 1)
    dn = pltpu.roll(n, ROPE_HALF, 1)
    return n * cos_t + up * sin_up + dn * sin_dn


def _inproj_a_kernel(x_ref, ng_ref, sc_ref, sh_ref, w_ref, qg_ref, kg_ref, bd_ref, cos_ref, sup_ref, sdn_ref,
                     q_ref, k_ref, v_ref, g_ref):
    h = _adaln(x_ref[0], ng_ref[...], sc_ref[0, 0], sh_ref[0, 0])
    p = jnp.dot(h.astype(BF16), w_ref[...], preferred_element_type=F32)
    bd = bd_ref[...]
    cos_t, sup, sdn = cos_ref[...], sup_ref[...], sdn_ref[...]
    qg, kg = qg_ref[...], kg_ref[...]
    for j in range(A_WIDTH // LANES):
        blk = p[:, j * LANES:(j + 1) * LANES]
        q_ref[0, :, j * LANES:(j + 1) * LANES] = _head_norm_rope(blk, qg, bd, cos_t, sup, sdn).astype(BF16)
    for j in range(A_KV_WIDTH // LANES):
        blk = p[:, A_WIDTH + j * LANES:A_WIDTH + (j + 1) * LANES]
        k_ref[0, :, j * LANES:(j + 1) * LANES] = _head_norm_rope(blk, kg, bd, cos_t, sup, sdn).astype(BF16)
    v_ref[0, 0] = p[:, A_WIDTH + A_KV_WIDTH:A_WIDTH + 2 * A_KV_WIDTH].T.astype(BF16)
    g_ref[0] = _silu(p[:, A_WIDTH + 2 * A_KV_WIDTH:]).astype(BF16)


def _inproj_a(x, ng, mods, w, qg, kg, bd, cos_t, sup, sdn, tm):
    b, r, d = x.shape
    n = w.shape[1]
    shared = mods.shape[0] == 1
    mod_idx = (lambda i, j, c: (0, c, 0, 0)) if shared else (lambda i, j, c: (i, c, 0, 0))
    row = lambda i, j: (i, j, 0)
    const2 = lambda i, j: (0, 0)
    return pl.pallas_call(
        _inproj_a_kernel,
        grid=(b, r // tm),
        in_specs=[
            pl.BlockSpec((1, tm, d), row),
            pl.BlockSpec((1, d), const2),
            pl.BlockSpec((1, 1, 1, d), lambda i, j: mod_idx(i, j, 1)),
            pl.BlockSpec((1, 1, 1, d), lambda i, j: mod_idx(i, j, 0)),
            pl.BlockSpec((d, n), const2),
            pl.BlockSpec((1, LANES), const2),
            pl.BlockSpec((1, LANES), const2),
            pl.BlockSpec((LANES, LANES), const2),
            pl.BlockSpec((tm, LANES), lambda i, j: (j, 0)),
            pl.BlockSpec((tm, LANES), lambda i, j: (j, 0)),
            pl.BlockSpec((tm, LANES), lambda i, j: (j, 0)),
        ],
        out_specs=[
            pl.BlockSpec((1, tm, A_WIDTH), row),
            pl.BlockSpec((1, tm, A_KV_WIDTH), row),
            pl.BlockSpec((1, 1, A_KV_WIDTH, tm), lambda i, j: (i, j, 0, 0)),
            pl.BlockSpec((1, tm, A_WIDTH), row),
        ],
        out_shape=[
            jax.ShapeDtypeStruct((b, r, A_WIDTH), BF16),
            jax.ShapeDtypeStruct((b, r, A_KV_WIDTH), BF16),
            jax.ShapeDtypeStruct((b, r // tm, A_KV_WIDTH, tm), BF16),
            jax.ShapeDtypeStruct((b, r, A_WIDTH), BF16),
        ],
        compiler_params=_cparams(("parallel", "parallel")),
        name="inproj_a",
    )(x, ng, mods, mods, w, qg, kg, bd, cos_t, sup, sdn)


def _stack_qt(q2, half):
    qt = q2.astype(F32).T
    zero = jnp.zeros((HEAD_DIM, qt.shape[1]), F32)
    first = half == 0

    def place(h_t):
        return jnp.where(first, jnp.concatenate([h_t, zero], axis=0), jnp.concatenate([zero, h_t], axis=0))

    return jnp.concatenate([place(qt[:HEAD_DIM]), place(qt[HEAD_DIM:])], axis=1).astype(BF16)


def _run_time_zero(step):
    return pl.multiple_of((step // N_PAIRS) * SUBLANES, SUBLANES)


def _store_cols(scr, row_start, val):
    n = val.shape[0]
    for j in range(scr.shape[0]):
        scr[j, row_start:row_start + n, :] = val[:, j * LANES:(j + 1) * LANES]


def _load_cols(scr, nk):
    return jnp.concatenate([scr[j, 0:nk, :] for j in range(scr.shape[0])], axis=1)


def _reread_barrier(s_scr, s, row0):
    s_scr[0, pl.ds(row0, SUBLANES), :] = s[0:SUBLANES, 0:LANES]


def _softmax_cols(s_scr, p_scr, nk, m):
    m_out, a_out = [], []
    for j in range(s_scr.shape[0]):
        sl = slice(j * LANES, (j + 1) * LANES)
        mx = s_scr[j, 0:MAX_ROWS, :]
        for r in range(1, nk // MAX_ROWS):
            mx = jnp.maximum(mx, s_scr[j, r * MAX_ROWS:(r + 1) * MAX_ROWS, :])
        m_new = jnp.maximum(m[:, sl], jnp.max(mx, axis=0, keepdims=True))
        for r in range(nk // EXP_ROWS):
            rows = slice(r * EXP_ROWS, (r + 1) * EXP_ROWS)
            p_scr[j, rows, :] = jnp.exp2(s_scr[j, rows, :] - m_new).astype(BF16)
        m_out.append(m_new)
        a_out.append(jnp.exp2(m[:, sl] - m_new))
    cat = lambda xs: jnp.concatenate(xs, axis=1)
    return cat(m_out), cat(a_out)


def _attn_a_kernel(*refs, tq, tk, n_x_chunks):
    n_in = 10 if n_x_chunks else 8
    if n_x_chunks:
        q_ref, kx_ref, vxt_ref, kc_ref, vct_ref, gate_ref, x_ref, gx_ref, w_ref, out_ref = refs[:n_in]
    else:
        q_ref, kc_ref, vct_ref, gate_ref, x_ref, gx_ref, w_ref, out_ref = refs[:n_in]
    o_scr = refs[n_in]
    s_bufs = refs[n_in + 1:n_in + 1 + N_SCORE_BUFS]
    p_bufs = refs[n_in + 1 + N_SCORE_BUFS:]
    hp = pl.program_id(2)
    kv_head = hp // 2
    vrow = pl.multiple_of(kv_head * HEAD_DIM, HEAD_DIM)
    a_t = _stack_qt(q_ref[0], kv_head % 2)
    cols = 2 * tq
    lc = kc_ref.shape[1]

    n_chunks = 1 + n_x_chunks
    chunk_rows = lambda i: lc if i == 0 else tk
    k_blk = lambda i: kc_ref[0] if i == 0 else kx_ref[0, (i - 1) * tk:i * tk, :]
    v_blk = lambda i: (vct_ref[0, 0, pl.ds(vrow, HEAD_DIM), :] if i == 0
                       else vxt_ref[0, i - 1, pl.ds(vrow, HEAD_DIM), :])

    def scores(i):
        _store_cols(s_bufs[i % N_SCORE_BUFS], 0, jnp.dot(k_blk(i), a_t, preferred_element_type=F32))

    def consume(i, carry):
        m, acc = carry
        nk = chunk_rows(i)
        p_scr = p_bufs[i % len(p_bufs)]
        m, alpha = _softmax_cols(s_bufs[i % N_SCORE_BUFS], p_scr, nk, m)
        v_ext = jnp.concatenate([v_blk(i), jnp.ones((SUM_ROWS, nk), BF16)], axis=0)
        acc = alpha * acc + jnp.dot(v_ext, _load_cols(p_scr, nk), preferred_element_type=F32)
        return m, acc

    carry = (jnp.full((1, cols), NEG_BIG, F32), jnp.zeros((HEAD_DIM + SUM_ROWS, cols), F32))
    scores(0)
    for i in range(n_chunks):
        if i + 1 < n_chunks:
            scores(i + 1)
        carry = consume(i, carry)
    _, acc = carry
    o_t = acc[:HEAD_DIM] / acc[HEAD_DIM:HEAD_DIM + 1]
    o_scr[hp] = jnp.concatenate([o_t[:, :tq], o_t[:, tq:]], axis=0).T

    @pl.when(hp == N_PAIRS - 1)
    def _():
        o_full = jnp.concatenate([o_scr[j] for j in range(N_PAIRS)], axis=1)
        u = (o_full * gate_ref[0].astype(F32)).astype(BF16)
        y = jnp.dot(u, w_ref[...], preferred_element_type=F32)
        out_ref[0] = x_ref[0] + gx_ref[0, 0] * y


def _attn_a(q, kx, vxt, kc, vct, gate, x, mods, w_out, tq):
    b, r, d = x.shape
    has_x = kx is not None
    shared = mods.shape[0] == 1
    qrow = lambda i, j, h: (i, j, 0)
    kvp = lambda i, j, h: (i, 0, h // 4)
    whole = lambda i, j, h: (i, 0, 0, 0)
    in_specs = [pl.BlockSpec((1, tq, LANES), lambda i, j, h: (i, j, h))]
    args = [q]
    lc = kc.shape[1]
    tk = lc
    n_x_chunks = 0
    if has_x:
        t = kx.shape[1]
        n_x_chunks, tk = vxt.shape[1], vxt.shape[3]
        in_specs += [pl.BlockSpec((1, t, LANES), kvp), pl.BlockSpec((1,) + vxt.shape[1:], whole)]
        args += [kx, vxt]
    in_specs += [
        pl.BlockSpec((1, lc, LANES), kvp),
        pl.BlockSpec((1,) + vct.shape[1:], whole),
        pl.BlockSpec((1, tq, d), qrow),
        pl.BlockSpec((1, tq, d), qrow),
        pl.BlockSpec((1, 1, 1, d), (lambda i, j, h: (0, 2, 0, 0)) if shared else (lambda i, j, h: (i, 2, 0, 0))),
        pl.BlockSpec((d, d), lambda i, j, h: (0, 0)),
    ]
    args += [kc, vct, gate, x, mods, w_out]
    return pl.pallas_call(
        functools.partial(_attn_a_kernel, tq=tq, tk=tk, n_x_chunks=n_x_chunks),
        grid=(b, r // tq, N_PAIRS),
        in_specs=in_specs,
        out_specs=pl.BlockSpec((1, tq, d), qrow),
        out_shape=jax.ShapeDtypeStruct((b, r, d), F32),
        scratch_shapes=[
            pltpu.VMEM((N_PAIRS, tq, LANES), F32),
            *[pltpu.VMEM((2 * tq // LANES, max(tk, lc), LANES), F32)] * N_SCORE_BUFS,
            *[pltpu.VMEM((2 * tq // LANES, max(tk, lc), LANES), BF16)] * 2,
        ],
        compiler_params=_cparams(("parallel", "parallel", "arbitrary")),
        name="attn_a_x" if has_x else "attn_a_ctx",
    )(*args)


def _store_vt_chunks(v_ref, v):
    for c in range(v.shape[0] // NAT_CHUNK):
        v_ref[0, c] = v[c * NAT_CHUNK:(c + 1) * NAT_CHUNK].T.astype(BF16)


def _inproj_b_kernel(x_ref, ng_ref, sc_ref, sh_ref, w_ref, *out_refs, kv_only):
    h = _adaln(x_ref[0], ng_ref[...], sc_ref[0, 0], sh_ref[0, 0])
    p = jnp.dot(h.astype(BF16), w_ref[...], preferred_element_type=F32)
    if kv_only:
        k_ref, v_ref = out_refs
        k_ref[0] = p[:, :B_WIDTH].astype(BF16)
        _store_vt_chunks(v_ref, p[:, B_WIDTH:])
    else:
        q_ref, k_ref, v_ref, g_ref = out_refs
        q_ref[0] = (p[:, :B_WIDTH] * (ATTN_SCALE * LOG2E)).astype(BF16)
        k_ref[0] = p[:, B_WIDTH:2 * B_WIDTH].astype(BF16)
        _store_vt_chunks(v_ref, p[:, 2 * B_WIDTH:3 * B_WIDTH])
        g_ref[0] = _silu(p[:, 3 * B_WIDTH:]).astype(BF16)


def _inproj_b(x, ng, mods, w, tm, kv_only):
    b, r, d = x.shape
    n = w.shape[1]
    shared = mods.shape[0] == 1
    mod_idx = (lambda i, c: (0, c, 0, 0)) if shared else (lambda i, c: (i, c, 0, 0))
    row = lambda i, j: (i, j, 0)
    row_spec = pl.BlockSpec((1, tm, B_WIDTH), row)
    row_shape = jax.ShapeDtypeStruct((b, r, B_WIDTH), BF16)
    vt_spec = pl.BlockSpec((1, tm // NAT_CHUNK, B_WIDTH, NAT_CHUNK), lambda i, j: (i, j, 0, 0))
    vt_shape = jax.ShapeDtypeStruct((b, r // NAT_CHUNK, B_WIDTH, NAT_CHUNK), BF16)
    if kv_only:
        out_specs, out_shape = [row_spec, vt_spec], [row_shape, vt_shape]
    else:
        out_specs, out_shape = [row_spec, row_spec, vt_spec, row_spec], [row_shape, row_shape, vt_shape, row_shape]
    return pl.pallas_call(
        functools.partial(_inproj_b_kernel, kv_only=kv_only),
        grid=(b, r // tm),
        in_specs=[
            pl.BlockSpec((1, tm, d), row),
            pl.BlockSpec((1, d), lambda i, j: (0, 0)),
            pl.BlockSpec((1, 1, 1, d), lambda i, j: mod_idx(i, 1)),
            pl.BlockSpec((1, 1, 1, d), lambda i, j: mod_idx(i, 0)),
            pl.BlockSpec((d, n), lambda i, j: (0, 0)),
        ],
        out_specs=out_specs,
        out_shape=out_shape,
        compiler_params=_cparams(("parallel", "parallel")),
        name="inproj_b_ctx" if kv_only else "inproj_b_x",
    )(x, ng, mods, mods, w)


def _nat_scores(g, n_groups, q_ref, k_ref, kc_ref, tab_ref, s_scr, row0):
    g = jnp.asarray(g, jnp.int32)
    c0 = jnp.clip(g - 1, 0, n_groups - NAT_SPAN)
    kind = (g > 0).astype(jnp.int32) + (g == n_groups - 1).astype(jnp.int32)
    qs = pl.multiple_of(g * NAT_CHUNK, NAT_CHUNK)
    ks = pl.multiple_of(c0 * NAT_CHUNK, NAT_CHUNK)
    qt = q_ref[0, pl.ds(qs, NAT_CHUNK), :].astype(F32).T
    top = lax.broadcasted_iota(jnp.int32, qt.shape, 0) < HEAD_DIM
    a_t = jnp.concatenate([jnp.where(top, qt, 0.0), jnp.where(top, 0.0, qt)], axis=1).astype(BF16)
    s_span = jnp.dot(k_ref[0, pl.ds(ks, NAT_KEYS), :], a_t, preferred_element_type=F32) + tab_ref[0, kind]
    s_ctx = jnp.dot(kc_ref[0], a_t, preferred_element_type=F32)
    _store_cols(s_scr, 0, s_span)
    _store_cols(s_scr, NAT_KEYS, s_ctx)
    _reread_barrier(s_scr, s_span, row0)
    return jnp.maximum(jnp.max(s_span, axis=0, keepdims=True), jnp.max(s_ctx, axis=0, keepdims=True))


def _nat_softmax(s_scr, p_scr, m):
    for j in range(s_scr.shape[0]):
        sl = slice(j * LANES, (j + 1) * LANES)
        for r in range(s_scr.shape[1] // EXP_ROWS):
            rows = slice(r * EXP_ROWS, (r + 1) * EXP_ROWS)
            p_scr[j, rows, :] = jnp.exp2(s_scr[j, rows, :] - m[:, sl]).astype(BF16)


def _nat_output(g, n_groups, vt_ref, vct_ref, p_scr, o_ref):
    nk = p_scr.shape[1]
    c0 = jnp.clip(g - 1, 0, n_groups - NAT_SPAN)
    qs = pl.multiple_of(g * NAT_CHUNK, NAT_CHUNK)
    v_t = jnp.concatenate([vt_ref[0, c0 + c] for c in range(NAT_SPAN)] + [vct_ref[0, 0]], axis=1)
    v_ext = jnp.concatenate([v_t, jnp.ones((SUM_ROWS, nk), BF16)], axis=0)
    acc = jnp.dot(v_ext, _load_cols(p_scr, nk), preferred_element_type=F32)
    o_t = acc[:LANES] / acc[LANES:LANES + 1]
    both = jnp.concatenate([o_t[:HEAD_DIM, :NAT_CHUNK], o_t[HEAD_DIM:, NAT_CHUNK:]], axis=0)
    o_ref[0, pl.ds(qs, NAT_CHUNK), :] = both.T.astype(BF16)


def _natten_kernel(q_ref, k_ref, vt_ref, kc_ref, vct_ref, tab_ref, o_ref, s0, s1, p0, p1, *, n_groups):
    row0 = _run_time_zero(pl.program_id(0))

    def body(i, carry):
        for u, (s_scr, p_scr) in enumerate(((s0, p0), (s1, p1))):
            g = 2 * i + u
            m = _nat_scores(g, n_groups, q_ref, k_ref, kc_ref, tab_ref, s_scr, row0)
            _nat_softmax(s_scr, p_scr, m)
            _nat_output(g, n_groups, vt_ref, vct_ref, p_scr, o_ref)
        return carry

    lax.fori_loop(0, n_groups // 2, body, 0)


def _natten(q, k, vt, kc, vct, tab):
    b, t, d = q.shape
    lc = kc.shape[1]
    n_groups = t // NAT_CHUNK
    nk = NAT_KEYS + lc
    blk = lambda h, i: (i, 0, h)
    return pl.pallas_call(
        functools.partial(_natten_kernel, n_groups=n_groups),
        grid=(N_PAIRS, b),
        in_specs=[
            pl.BlockSpec((1, t, LANES), blk),
            pl.BlockSpec((1, t, LANES), blk),
            pl.BlockSpec((1, n_groups, LANES, NAT_CHUNK), lambda h, i: (i, 0, h, 0)),
            pl.BlockSpec((1, lc, LANES), blk),
            pl.BlockSpec((1, 1, LANES, lc), lambda h, i: (i, 0, h, 0)),
            pl.BlockSpec((1,) + tab.shape[1:], lambda h, i: (h, 0, 0, 0)),
        ],
        out_specs=pl.BlockSpec((1, t, LANES), blk),
        out_shape=jax.ShapeDtypeStruct((b, t, d), BF16),
        scratch_shapes=[pltpu.VMEM((2 * NAT_CHUNK // LANES, nk, LANES), F32)] * 2
        + [pltpu.VMEM((2 * NAT_CHUNK // LANES, nk, LANES), BF16)] * 2,
        compiler_params=_cparams(("parallel", "parallel")),
        name="natten_b",
    )(q, k, vt, kc, vct, tab)


def _outproj_final_kernel(o_ref, gate_ref, x_ref, gx_ref, w_ref, fg_ref, out_ref):
    u = (o_ref[0].astype(F32) * gate_ref[0].astype(F32)).astype(BF16)
    y = jnp.dot(u, w_ref[...], preferred_element_type=F32)
    x2 = x_ref[0] + gx_ref[0, 0] * y
    ms = jnp.mean(x2 * x2, axis=-1, keepdims=True)
    out_ref[0] = x2 * lax.rsqrt(ms + NORM_EPS) * fg_ref[...]


def _outproj_final(o, gate, x, mods, w_out, fg, tm):
    b, t, d = x.shape
    row = lambda i, j: (i, j, 0)
    return pl.pallas_call(
        _outproj_final_kernel,
        grid=(b, t // tm),
        in_specs=[
            pl.BlockSpec((1, tm, d), row),
            pl.BlockSpec((1, tm, d), row),
            pl.BlockSpec((1, tm, d), row),
            pl.BlockSpec((1, 1, 1, d), lambda i, j: (i, 2, 0, 0)),
            pl.BlockSpec((d, d), lambda i, j: (0, 0)),
            pl.BlockSpec((1, d), lambda i, j: (0, 0)),
        ],
        out_specs=pl.BlockSpec((1, tm, d), row),
        out_shape=jax.ShapeDtypeStruct((b, t, d), F32),
        compiler_params=_cparams(("parallel", "parallel")),
        name="outproj_final",
    )(o, gate, x, mods, w_out, fg)


def _rope_tables(t_len):
    pos = jnp.arange(t_len, dtype=jnp.int32)
    row = (pos // GRID_W).astype(F32)
    col = (pos % GRID_W).astype(F32)
    inv = ROPE_THETA ** (-jnp.arange(0, ROPE_AXIS_DIM, 2, dtype=F32) / ROPE_AXIS_DIM)
    ang_r = row[:, None] * inv
    ang_c = col[:, None] * inv
    zero = jnp.zeros_like(ang_r)
    cos_h = jnp.concatenate([jnp.cos(ang_r), jnp.cos(ang_r), jnp.cos(ang_c), jnp.cos(ang_c)], axis=1)
    sup_h = jnp.concatenate([-jnp.sin(ang_r), zero, -jnp.sin(ang_c), zero], axis=1)
    sdn_h = jnp.concatenate([zero, jnp.sin(ang_r), zero, jnp.sin(ang_c)], axis=1)
    two = lambda a: jnp.concatenate([a, a], axis=1)
    return two(cos_h), two(sup_h), two(sdn_h)


def _natten_tables(rpb, rows):
    h = rpb.shape[0]
    qcol = np.arange(GRID_W)
    c0 = np.clip(qcol - WIN_C // 2, 0, GRID_W - WIN_C)
    kcol = np.arange(GRID_W)
    valid = (kcol[None, :] >= c0[:, None]) & (kcol[None, :] < c0[:, None] + WIN_C)
    pad = GRID_W - WIN_C
    padded = jnp.pad(rpb.astype(F32) * LOG2E, ((0, 0), (0, 0), (pad, pad)))
    toep = jnp.stack([padded[:, :, GRID_W - 1 - j:2 * GRID_W - 1 - j] for j in range(GRID_W)], axis=2)
    toep = jnp.where(jnp.asarray(valid)[None, None], toep, NEG_BIG)
    toep_t = jnp.swapaxes(toep, 2, 3)
    neg = jnp.full((h, GRID_W, GRID_W), NEG_BIG, F32)
    n_groups = rows // NAT_ROWS
    span_rows = NAT_SPAN * NAT_ROWS

    def plan(g):
        ks = int(np.clip(g - 1, 0, n_groups - NAT_SPAN)) * NAT_ROWS
        out = []
        for s in range(span_rows):
            for i in range(NAT_ROWS):
                rq, rk = NAT_ROWS * g + i, ks + s
                r0 = int(np.clip(rq - WIN_R // 2, 0, rows - WIN_R))
                out.append(rk - rq + WIN_R - 1 if r0 <= rk < r0 + WIN_R else None)
        return out

    kinds = [plan(0), plan(1), plan(n_groups - 1)]
    assert all(plan(g) == kinds[1] for g in range(1, n_groups - 1))
    blocks = jnp.stack([jnp.stack([neg if dr is None else toep_t[:, dr] for dr in kind]) for kind in kinds])
    blocks = blocks.reshape(3, span_rows, NAT_ROWS, h // 2, 2, GRID_W, GRID_W).transpose(3, 0, 1, 5, 4, 2, 6)
    return blocks.reshape(h // 2, 3, span_rows * GRID_W, 2 * NAT_ROWS * GRID_W)


def kernel(x, c, ctx, c_ctx, norm_g, w_mod, b_mod, a_w_in, a_q_norm_g, a_k_norm_g, a_w_out,
           b_w_in, b_rpb, b_w_out, final_norm_g):
    bsz, t, d = x.shape
    lc = ctx.shape[1]

    n_rows = ((bsz + 1 + 7) // 8) * 8
    c_rows = jnp.zeros((n_rows, d), F32).at[:bsz].set(c).at[bsz].set(c_ctx)
    mods = _modulation(c_rows, w_mod, b_mod).reshape(w_mod.shape[0], n_rows, 3, 1, d)

    cos_t, sup, sdn = _rope_tables(t)
    one_t = jnp.ones((lc, LANES), F32)
    zero_t = jnp.zeros((lc, LANES), F32)
    head_block = np.kron(np.eye(LANES // HEAD_DIM), np.ones((HEAD_DIM, HEAD_DIM)))
    bd = jnp.asarray(head_block, BF16)
    two = lambda g: jnp.concatenate([g, g]).reshape(1, LANES).astype(F32)

    mx, mc = mods[0, :bsz], mods[0, bsz:bsz + 1]
    ng = norm_g[0].reshape(1, d)
    w_in = a_w_in[0].astype(BF16)
    qg, kg = two(a_q_norm_g[0]) * (ATTN_SCALE * LOG2E), two(a_k_norm_g[0])
    q, k, vt, gate = _inproj_a(x, ng, mx, w_in, qg, kg, bd, cos_t, sup, sdn, tm=512)
    qc, kc, vct, gate_c = _inproj_a(ctx, ng, mc, w_in, qg, kg, bd, one_t, zero_t, zero_t, tm=lc)
    w_out = a_w_out[0].astype(BF16)
    x1 = _attn_a(q, k, vt, kc, vct, gate, x, mx, w_out, tq=512)
    ctx1 = _attn_a(qc, None, None, kc, vct, gate_c, ctx, mc, w_out, tq=lc)

    mx, mc = mods[1, :bsz], mods[1, bsz:bsz + 1]
    ng = norm_g[1].reshape(1, d)
    w_in = b_w_in[0].astype(BF16)
    q, k, vt, gate = _inproj_b(x1, ng, mx, w_in, tm=256, kv_only=False)
    kc, vct = _inproj_b(ctx1, ng, mc, w_in[:, B_WIDTH:3 * B_WIDTH], tm=lc, kv_only=True)
    o = _natten(q, k, vt, kc, vct, _natten_tables(b_rpb[0], t // GRID_W))
    return _outproj_final(o, gate, x1, mx, b_w_out[0].astype(BF16), final_norm_g.reshape(1, d), tm=512)
```

```python
import functools

import jax
import jax.numpy as jnp
import numpy as np
from jax import lax
from jax.experimental import pallas as pl
from jax.experimental.pallas import tpu as pltpu

F32 = jnp.float32
BF16 = jnp.bfloat16

LANES = 128
VMEM_LIMIT = 56 * 1024 * 1024

D_MODEL = 1024
GRID_W = 64
HEAD_DIM = 64
NORM_EPS = 1e-6
ATTN_SCALE = HEAD_DIM ** -0.5
A_HEADS = 16
A_KV_HEADS = 4
A_WIDTH = A_HEADS * HEAD_DIM
A_KV_WIDTH = A_KV_HEADS * HEAD_DIM
A_GROUP = A_HEADS // A_KV_HEADS
ROPE_THETA = 10000.0
ROPE_AXIS_DIM = HEAD_DIM // 2
ROPE_HALF = ROPE_AXIS_DIM // 2
B_HEADS = 16
B_WIDTH = B_HEADS * HEAD_DIM
WIN_R = 8
WIN_C = 16
N_PAIRS = D_MODEL // LANES
NEG_BIG = -1e30
MAX_ROWS = 128
EXP_ROWS = 64
SUM_ROWS = 16
NAT_ROWS = 4
NAT_CHUNK = NAT_ROWS * GRID_W
NAT_SPAN = 3
NAT_KEYS = NAT_SPAN * NAT_CHUNK
N_SCORE_BUFS = 3
LOG2E = 1.4426950408889634


def _cparams(sem, flags=None):
    return pltpu.CompilerParams(dimension_semantics=sem, vmem_limit_bytes=VMEM_LIMIT, flags=flags)


def _mod_kernel(c_ref, w_ref, b_ref, o_ref):
    c = c_ref[...]
    s = c * jax.nn.sigmoid(c)
    o_ref[0] = jnp.dot(s, w_ref[0], precision=lax.Precision.HIGHEST,
                       preferred_element_type=F32) + b_ref[0]


def _modulation(c_rows, w_mod, b_mod):
    depth, d, n = w_mod.shape
    rows = c_rows.shape[0]
    tn = 1024
    return pl.pallas_call(
        _mod_kernel,
        grid=(depth, n // tn),
        in_specs=[
            pl.BlockSpec((rows, d), lambda l, j: (0, 0)),
            pl.BlockSpec((1, d, tn), lambda l, j: (l, 0, j)),
            pl.BlockSpec((1, 1, tn), lambda l, j: (l, 0, j)),
        ],
        out_specs=pl.BlockSpec((1, rows, tn), lambda l, j: (l, 0, j)),
        out_shape=jax.ShapeDtypeStruct((depth, rows, n), F32),
        compiler_params=_cparams(("arbitrary", "arbitrary")),
        name="adaln_mod",
    )(c_rows, w_mod, b_mod.reshape(depth, 1, n))


def _adaln(x, ng, sc, sh):
    ms = jnp.mean(x * x, axis=-1, keepdims=True)
    y = x * lax.rsqrt(ms + NORM_EPS) * ng
    return y * (1.0 + sc) + sh


def _silu(z):
    return z * jax.nn.sigmoid(z)


def _head_norm_rope(blk, gain, bd, cos_t, sin_up, sin_dn):
    sq = blk * blk
    hi = sq.astype(BF16)
    lo = (sq - hi.astype(F32)).astype(BF16)
    ssum = (jnp.dot(hi, bd, preferred_element_type=F32) + jnp.dot(lo, bd, preferred_element_type=F32))
    n = blk * lax.rsqrt(ssum * (1.0 / HEAD_DIM) + NORM_EPS) * gain
    up = pltpu.roll(n, LANES - ROPE_HALF, 1)
    dn = pltpu.roll(n, ROPE_HALF, 1)
    return n * cos_t + up * sin_up + dn * sin_dn


def _inproj_a_kernel(x_ref, ng_ref, sc_ref, sh_ref, w_ref, qg_ref, kg_ref, bd_ref, cos_ref, sup_ref, sdn_ref,
                     q_ref, k_ref, v_ref, g_ref):
    h = _adaln(x_ref[0], ng_ref[...], sc_ref[0, 0], sh_ref[0, 0])
    p = jnp.dot(h.astype(BF16), w_ref[...], preferred_element_type=F32)
    bd = bd_ref[...]
    cos_t, sup, sdn = cos_ref[...], sup_ref[...], sdn_ref[...]
    qg, kg = qg_ref[...], kg_ref[...]
    for j in range(A_WIDTH // LANES):
        blk = p[:, j * LANES:(j + 1) * LANES]
        q_ref[0, :, j * LANES:(j + 1) * LANES] = _head_norm_rope(blk, qg, bd, cos_t, sup, sdn).astype(BF16)
    for j in range(A_KV_WIDTH // LANES):
        blk = p[:, A_WIDTH + j * LANES:A_WIDTH + (j + 1) * LANES]
        k_ref[0, :, j * LANES:(j + 1) * LANES] = _head_norm_rope(blk, kg, bd, cos_t, sup, sdn).astype(BF16)
    v_ref[0, 0] = p[:, A_WIDTH + A_KV_WIDTH:A_WIDTH + 2 * A_KV_WIDTH].T.astype(BF16)
    g_ref[0] = _silu(p[:, A_WIDTH + 2 * A_KV_WIDTH:]).astype(BF16)


def _inproj_a(x, ng, mods, w, qg, kg, bd, cos_t, sup, sdn, tm):
    b, r, d = x.shape
    n = w.shape[1]
    shared = mods.shape[0] == 1
    mod_idx = (lambda i, j, c: (0, c, 0, 0)) if shared else (lambda i, j, c: (i, c, 0, 0))
    row = lambda i, j: (i, j, 0)
    const2 = lambda i, j: (0, 0)
    return pl.pallas_call(
        _inproj_a_kernel,
        grid=(b, r // tm),
        in_specs=[
            pl.BlockSpec((1, tm, d), row),
            pl.BlockSpec((1, d), const2),
            pl.BlockSpec((1, 1, 1, d), lambda i, j: mod_idx(i, j, 1)),
            pl.BlockSpec((1, 1, 1, d), lambda i, j: mod_idx(i, j, 0)),
            pl.BlockSpec((d, n), const2),
            pl.BlockSpec((1, LANES), const2),
            pl.BlockSpec((1, LANES), const2),
            pl.BlockSpec((LANES, LANES), const2),
            pl.BlockSpec((tm, LANES), lambda i, j: (j, 0)),
            pl.BlockSpec((tm, LANES), lambda i, j: (j, 0)),
            pl.BlockSpec((tm, LANES), lambda i, j: (j, 0)),
        ],
        out_specs=[
            pl.BlockSpec((1, tm, A_WIDTH), row),
            pl.BlockSpec((1, tm, A_KV_WIDTH), row),
            pl.BlockSpec((1, 1, A_KV_WIDTH, tm), lambda i, j: (i, j, 0, 0)),
            pl.BlockSpec((1, tm, A_WIDTH), row),
        ],
        out_shape=[
            jax.ShapeDtypeStruct((b, r, A_WIDTH), BF16),
            jax.ShapeDtypeStruct((b, r, A_KV_WIDTH), BF16),
            jax.ShapeDtypeStruct((b, r // tm, A_KV_WIDTH, tm), BF16),
            jax.ShapeDtypeStruct((b, r, A_WIDTH), BF16),
        ],
        compiler_params=_cparams(("parallel", "parallel")),
        name="inproj_a",
    )(x, ng, mods, mods, w, qg, kg, bd, cos_t, sup, sdn)


def _stack_qt(q4, half):
    qt = q4.astype(F32).T
    zero = jnp.zeros((HEAD_DIM, qt.shape[1]), F32)
    first = half == 0

    def place(h_t):
        return jnp.where(first, jnp.concatenate([h_t, zero], axis=0), jnp.concatenate([zero, h_t], axis=0))

    heads = [place(qt[h * HEAD_DIM:(h + 1) * HEAD_DIM]) for h in range(A_GROUP)]
    return jnp.concatenate(heads, axis=1).astype(BF16)


def _store_cols(scr, row_start, val):
    n = val.shape[0]
    for j in range(scr.shape[0]):
        scr[j, row_start:row_start + n, :] = val[:, j * LANES:(j + 1) * LANES]


def _load_cols(scr, nk):
    return jnp.concatenate([scr[j, 0:nk, :] for j in range(scr.shape[0])], axis=1)


def _softmax_cols(s_scr, p_scr, nk, m):
    m_out, a_out = [], []
    for j in range(s_scr.shape[0]):
        sl = slice(j * LANES, (j + 1) * LANES)
        mx = s_scr[j, 0:MAX_ROWS, :]
        for r in range(1, nk // MAX_ROWS):
            mx = jnp.maximum(mx, s_scr[j, r * MAX_ROWS:(r + 1) * MAX_ROWS, :])
        m_new = jnp.maximum(m[:, sl], jnp.max(mx, axis=0, keepdims=True))
        for r in range(nk // EXP_ROWS):
            rows = slice(r * EXP_ROWS, (r + 1) * EXP_ROWS)
            p_scr[j, rows, :] = jnp.exp2(s_scr[j, rows, :] - m_new).astype(BF16)
        m_out.append(m_new)
        a_out.append(jnp.exp2(m[:, sl] - m_new))
    cat = lambda xs: jnp.concatenate(xs, axis=1)
    return cat(m_out), cat(a_out)


def _attn_a_kernel(*refs, tq, tk, n_x_chunks):
    n_in = 10 if n_x_chunks else 8
    if n_x_chunks:
        q_ref, kx_ref, vxt_ref, kc_ref, vct_ref, gate_ref, x_ref, gx_ref, w_ref, out_ref = refs[:n_in]
    else:
        q_ref, kc_ref, vct_ref, gate_ref, x_ref, gx_ref, w_ref, out_ref = refs[:n_in]
    o_scr = refs[n_in]
    s_bufs = refs[n_in + 1:n_in + 1 + N_SCORE_BUFS]
    p_bufs = refs[n_in + 1 + N_SCORE_BUFS:]
    kv_head = pl.program_id(2)
    vrow = pl.multiple_of(kv_head * HEAD_DIM, HEAD_DIM)
    a_t = _stack_qt(q_ref[0], kv_head % 2)
    cols = A_GROUP * tq
    lc = kc_ref.shape[1]

    n_chunks = 1 + n_x_chunks
    chunk_rows = lambda i: lc if i == 0 else tk
    k_blk = lambda i: kc_ref[0] if i == 0 else kx_ref[0, (i - 1) * tk:i * tk, :]
    v_blk = lambda i: (vct_ref[0, 0, pl.ds(vrow, HEAD_DIM), :] if i == 0
                       else vxt_ref[0, i - 1, pl.ds(vrow, HEAD_DIM), :])

    def scores(i):
        _store_cols(s_bufs[i % N_SCORE_BUFS], 0, jnp.dot(k_blk(i), a_t, preferred_element_type=F32))

    def consume(i, carry):
        m, acc = carry
        nk = chunk_rows(i)
        p_scr = p_bufs[i % len(p_bufs)]
        m, alpha = _softmax_cols(s_bufs[i % N_SCORE_BUFS], p_scr, nk, m)
        v_ext = jnp.concatenate([v_blk(i), jnp.ones((SUM_ROWS, nk), BF16)], axis=0)
        acc = alpha * acc + jnp.dot(v_ext, _load_cols(p_scr, nk), preferred_element_type=F32)
        return m, acc

    carry = (jnp.full((1, cols), NEG_BIG, F32), jnp.zeros((HEAD_DIM + SUM_ROWS, cols), F32))
    scores(0)
    for i in range(n_chunks):
        if i + 1 < n_chunks:
            scores(i + 1)
        carry = consume(i, carry)
    _, acc = carry
    o_t = acc[:HEAD_DIM] / acc[HEAD_DIM:HEAD_DIM + 1]
    o_scr[kv_head] = jnp.concatenate([o_t[:, h * tq:(h + 1) * tq] for h in range(A_GROUP)], axis=0).T

    @pl.when(kv_head == A_KV_HEADS - 1)
    def _():
        o_full = jnp.concatenate([o_scr[j] for j in range(A_KV_HEADS)], axis=1)
        u = (o_full * gate_ref[0].astype(F32)).astype(BF16)
        y = jnp.dot(u, w_ref[...], preferred_element_type=F32)
        out_ref[0] = x_ref[0] + gx_ref[0, 0] * y


def _attn_a(q, kx, vxt, kc, vct, gate, x, mods, w_out, tq):
    b, r, d = x.shape
    has_x = kx is not None
    shared = mods.shape[0] == 1
    qrow = lambda i, j, h: (i, j, 0)
    kvp = lambda i, j, h: (i, 0, h // 2)
    whole = lambda i, j, h: (i, 0, 0, 0)
    in_specs = [pl.BlockSpec((1, tq, A_GROUP * HEAD_DIM), lambda i, j, h: (i, j, h))]
    args = [q]
    lc = kc.shape[1]
    tk = lc
    n_x_chunks = 0
    if has_x:
        t = kx.shape[1]
        n_x_chunks, tk = vxt.shape[1], vxt.shape[3]
        in_specs += [pl.BlockSpec((1, t, LANES), kvp), pl.BlockSpec((1,) + vxt.shape[1:], whole)]
        args += [kx, vxt]
    in_specs += [
        pl.BlockSpec((1, lc, LANES), kvp),
        pl.BlockSpec((1,) + vct.shape[1:], whole),
        pl.BlockSpec((1, tq, d), qrow),
        pl.BlockSpec((1, tq, d), qrow),
        pl.BlockSpec((1, 1, 1, d), (lambda i, j, h: (0, 2, 0, 0)) if shared else (lambda i, j, h: (i, 2, 0, 0))),
        pl.BlockSpec((d, d), lambda i, j, h: (0, 0)),
    ]
    args += [kc, vct, gate, x, mods, w_out]
    return pl.pallas_call(
        functools.partial(_attn_a_kernel, tq=tq, tk=tk, n_x_chunks=n_x_chunks),
        grid=(b, r // tq, A_KV_HEADS),
        in_specs=in_specs,
        out_specs=pl.BlockSpec((1, tq, d), qrow),
        out_shape=jax.ShapeDtypeStruct((b, r, d), F32),
        scratch_shapes=[
            pltpu.VMEM((A_KV_HEADS, tq, A_GROUP * HEAD_DIM), F32),
            *[pltpu.VMEM((A_GROUP * tq // LANES, max(tk, lc), LANES), F32)] * N_SCORE_BUFS,
            *[pltpu.VMEM((A_GROUP * tq // LANES, max(tk, lc), LANES), BF16)] * 2,
        ],
        compiler_params=_cparams(("parallel", "parallel", "arbitrary")),
        name="attn_a_x" if has_x else "attn_a_ctx",
    )(*args)


def _store_vt_chunks(v_ref, v):
    for c in range(v.shape[0] // NAT_CHUNK):
        v_ref[0, c] = v[c * NAT_CHUNK:(c + 1) * NAT_CHUNK].T.astype(BF16)


def _inproj_b_kernel(x_ref, ng_ref, sc_ref, sh_ref, w_ref, *out_refs, kv_only):
    h = _adaln(x_ref[0], ng_ref[...], sc_ref[0, 0], sh_ref[0, 0])
    p = jnp.dot(h.astype(BF16), w_ref[...], preferred_element_type=F32)
    if kv_only:
        k_ref, v_ref = out_refs
        k_ref[0] = p[:, :B_WIDTH].astype(BF16)
        _store_vt_chunks(v_ref, p[:, B_WIDTH:])
    else:
        q_ref, k_ref, v_ref, g_ref = out_refs
        q_ref[0] = (p[:, :B_WIDTH] * (ATTN_SCALE * LOG2E)).astype(BF16)
        k_ref[0] = p[:, B_WIDTH:2 * B_WIDTH].astype(BF16)
        _store_vt_chunks(v_ref, p[:, 2 * B_WIDTH:3 * B_WIDTH])
        g_ref[0] = _silu(p[:, 3 * B_WIDTH:]).astype(BF16)


def _inproj_b(x, ng, mods, w, tm, kv_only):
    b, r, d = x.shape
    n = w.shape[1]
    shared = mods.shape[0] == 1
    mod_idx = (lambda i, c: (0, c, 0, 0)) if shared else (lambda i, c: (i, c, 0, 0))
    row = lambda i, j: (i, j, 0)
    row_spec = pl.BlockSpec((1, tm, B_WIDTH), row)
    row_shape = jax.ShapeDtypeStruct((b, r, B_WIDTH), BF16)
    vt_spec = pl.BlockSpec((1, tm // NAT_CHUNK, B_WIDTH, NAT_CHUNK), lambda i, j: (i, j, 0, 0))
    vt_shape = jax.ShapeDtypeStruct((b, r // NAT_CHUNK, B_WIDTH, NAT_CHUNK), BF16)
    if kv_only:
        out_specs, out_shape = [row_spec, vt_spec], [row_shape, vt_shape]
    else:
        out_specs, out_shape = [row_spec, row_spec, vt_spec, row_spec], [row_shape, row_shape, vt_shape, row_shape]
    return pl.pallas_call(
        functools.partial(_inproj_b_kernel, kv_only=kv_only),
        grid=(b, r // tm),
        in_specs=[
            pl.BlockSpec((1, tm, d), row),
            pl.BlockSpec((1, d), lambda i, j: (0, 0)),
            pl.BlockSpec((1, 1, 1, d), lambda i, j: mod_idx(i, 1)),
            pl.BlockSpec((1, 1, 1, d), lambda i, j: mod_idx(i, 0)),
            pl.BlockSpec((d, n), lambda i, j: (0, 0)),
        ],
        out_specs=out_specs,
        out_shape=out_shape,
        compiler_params=_cparams(("parallel", "parallel")),
        name="inproj_b_ctx" if kv_only else "inproj_b_x",
    )(x, ng, mods, mods, w)


def _nat_scores(g, n_groups, q_ref, k_ref, kc_ref, tab_ref, s_scr):
    g = jnp.asarray(g, jnp.int32)
    c0 = jnp.clip(g - 1, 0, n_groups - NAT_SPAN)
    kind = (g > 0).astype(jnp.int32) + (g == n_groups - 1).astype(jnp.int32)
    qs = pl.multiple_of(g * NAT_CHUNK, NAT_CHUNK)
    ks = pl.multiple_of(c0 * NAT_CHUNK, NAT_CHUNK)
    qt = q_ref[0, pl.ds(qs, NAT_CHUNK), :].astype(F32).T
    top = lax.broadcasted_iota(jnp.int32, qt.shape, 0) < HEAD_DIM
    a_t = jnp.concatenate([jnp.where(top, qt, 0.0), jnp.where(top, 0.0, qt)], axis=1).astype(BF16)
    s_span = jnp.dot(k_ref[0, pl.ds(ks, NAT_KEYS), :], a_t, preferred_element_type=F32)
    for j in range(s_scr.shape[0]):
        s_scr[j, 0:NAT_KEYS, :] = s_span[:, j * LANES:(j + 1) * LANES] + tab_ref[0, kind, j]
    _store_cols(s_scr, NAT_KEYS, jnp.dot(kc_ref[0], a_t, preferred_element_type=F32))


def _nat_softmax(s_scr, p_scr):
    nk = s_scr.shape[1]
    for j in range(s_scr.shape[0]):
        mx = s_scr[j, 0:MAX_ROWS, :]
        for r in range(1, nk // MAX_ROWS):
            mx = jnp.maximum(mx, s_scr[j, r * MAX_ROWS:(r + 1) * MAX_ROWS, :])
        m = jnp.max(mx, axis=0, keepdims=True)
        for r in range(nk // EXP_ROWS):
            rows = slice(r * EXP_ROWS, (r + 1) * EXP_ROWS)
            p_scr[j, rows, :] = jnp.exp2(s_scr[j, rows, :] - m).astype(BF16)


def _nat_output(g, n_groups, vt_ref, vct_ref, p_scr, o_ref):
    nk = p_scr.shape[1]
    c0 = jnp.clip(g - 1, 0, n_groups - NAT_SPAN)
    qs = pl.multiple_of(g * NAT_CHUNK, NAT_CHUNK)
    v_t = jnp.concatenate([vt_ref[0, c0 + c] for c in range(NAT_SPAN)] + [vct_ref[0, 0]], axis=1)
    v_ext = jnp.concatenate([v_t, jnp.ones((SUM_ROWS, nk), BF16)], axis=0)
    acc = jnp.dot(v_ext, _load_cols(p_scr, nk), preferred_element_type=F32)
    o_t = acc[:LANES] / acc[LANES:LANES + 1]
    both = jnp.concatenate([o_t[:HEAD_DIM, :NAT_CHUNK], o_t[HEAD_DIM:, NAT_CHUNK:]], axis=0)
    o_ref[0, pl.ds(qs, NAT_CHUNK), :] = both.T.astype(BF16)


def _natten_kernel(q_ref, k_ref, vt_ref, kc_ref, vct_ref, tab_ref, o_ref, s0, s1, p0, p1, *, n_groups):
    def body(i, carry):
        for u, (s_scr, p_scr) in enumerate(((s0, p0), (s1, p1))):
            g = 2 * i + u
            _nat_scores(g, n_groups, q_ref, k_ref, kc_ref, tab_ref, s_scr)
            _nat_softmax(s_scr, p_scr)
            _nat_output(g, n_groups, vt_ref, vct_ref, p_scr, o_ref)
        return carry

    lax.fori_loop(0, n_groups // 2, body, 0)


def _natten(q, k, vt, kc, vct, tab):
    b, t, d = q.shape
    lc = kc.shape[1]
    n_groups = t // NAT_CHUNK
    nk = NAT_KEYS + lc
    blk = lambda h, i: (i, 0, h)
    return pl.pallas_call(
        functools.partial(_natten_kernel, n_groups=n_groups),
        grid=(N_PAIRS, b),
        in_specs=[
            pl.BlockSpec((1, t, LANES), blk),
            pl.BlockSpec((1, t, LANES), blk),
            pl.BlockSpec((1, n_groups, LANES, NAT_CHUNK), lambda h, i: (i, 0, h, 0)),
            pl.BlockSpec((1, lc, LANES), blk),
            pl.BlockSpec((1, 1, LANES, lc), lambda h, i: (i, 0, h, 0)),
            pl.BlockSpec((1,) + tab.shape[1:], lambda h, i: (h, 0, 0, 0, 0)),
        ],
        out_specs=pl.BlockSpec((1, t, LANES), blk),
        out_shape=jax.ShapeDtypeStruct((b, t, d), BF16),
        scratch_shapes=[pltpu.VMEM((2 * NAT_CHUNK // LANES, nk, LANES), F32)] * 2
        + [pltpu.VMEM((2 * NAT_CHUNK // LANES, nk, LANES), BF16)] * 2,
        compiler_params=_cparams(("parallel", "parallel")),
        name="natten_b",
    )(q, k, vt, kc, vct, tab)


def _outproj_final_kernel(o_ref, gate_ref, x_ref, gx_ref, w_ref, fg_ref, out_ref):
    u = (o_ref[0].astype(F32) * gate_ref[0].astype(F32)).astype(BF16)
    y = jnp.dot(u, w_ref[...], preferred_element_type=F32)
    x2 = x_ref[0] + gx_ref[0, 0] * y
    ms = jnp.mean(x2 * x2, axis=-1, keepdims=True)
    out_ref[0] = x2 * lax.rsqrt(ms + NORM_EPS) * fg_ref[...]


def _outproj_final(o, gate, x, mods, w_out, fg, tm):
    b, t, d = x.shape
    row = lambda i, j: (i, j, 0)
    return pl.pallas_call(
        _outproj_final_kernel,
        grid=(b, t // tm),
        in_specs=[
            pl.BlockSpec((1, tm, d), row),
            pl.BlockSpec((1, tm, d), row),
            pl.BlockSpec((1, tm, d), row),
            pl.BlockSpec((1, 1, 1, d), lambda i, j: (i, 2, 0, 0)),
            pl.BlockSpec((d, d), lambda i, j: (0, 0)),
            pl.BlockSpec((1, d), lambda i, j: (0, 0)),
        ],
        out_specs=pl.BlockSpec((1, tm, d), row),
        out_shape=jax.ShapeDtypeStruct((b, t, d), F32),
        compiler_params=_cparams(("parallel", "parallel")),
        name="outproj_final",
    )(o, gate, x, mods, w_out, fg)


def _rope_tables(t_len):
    pos = jnp.arange(t_len, dtype=jnp.int32)
    row = (pos // GRID_W).astype(F32)
    col = (pos % GRID_W).astype(F32)
    inv = ROPE_THETA ** (-jnp.arange(0, ROPE_AXIS_DIM, 2, dtype=F32) / ROPE_AXIS_DIM)
    ang_r = row[:, None] * inv
    ang_c = col[:, None] * inv
    zero = jnp.zeros_like(ang_r)
    cos_h = jnp.concatenate([jnp.cos(ang_r), jnp.cos(ang_r), jnp.cos(ang_c), jnp.cos(ang_c)], axis=1)
    sup_h = jnp.concatenate([-jnp.sin(ang_r), zero, -jnp.sin(ang_c), zero], axis=1)
    sdn_h = jnp.concatenate([zero, jnp.sin(ang_r), zero, jnp.sin(ang_c)], axis=1)
    two = lambda a: jnp.concatenate([a, a], axis=1)
    return two(cos_h), two(sup_h), two(sdn_h)


def _nat_table_kernel(t2_ref, o_ref, *, plans):
    low = lax.broadcasted_iota(jnp.int32, (GRID_W, LANES), 1) < GRID_W
    neg = jnp.full((GRID_W, LANES), NEG_BIG, F32)
    half_rows = NAT_ROWS // 2
    for kind, plan in enumerate(plans):
        for head in range(2):
            for i2 in range(half_rows):
                for s in range(NAT_SPAN * NAT_ROWS):
                    d0, d1 = plan[s * NAT_ROWS + 2 * i2], plan[s * NAT_ROWS + 2 * i2 + 1]
                    b0 = neg if d0 is None else t2_ref[0, head, d0]
                    b1 = neg if d1 is None else t2_ref[0, head, d1]
                    o_ref[0, kind, head * half_rows + i2, s * GRID_W:(s + 1) * GRID_W, :] = jnp.where(low, b0, b1)


def _natten_tables(rpb, rows):
    h = rpb.shape[0]
    qcol = np.arange(GRID_W)
    c0 = np.clip(qcol - WIN_C // 2, 0, GRID_W - WIN_C)
    kcol = np.arange(GRID_W)
    valid = (kcol[:, None] >= c0[None, :]) & (kcol[:, None] < c0[None, :] + WIN_C)
    pad = GRID_W - WIN_C
    padded = jnp.pad(rpb.astype(F32) * LOG2E, ((0, 0), (0, 0), (pad, pad)))
    toep_t = jnp.stack([padded[:, :, GRID_W - 1 - j:2 * GRID_W - 1 - j] for j in range(GRID_W)], axis=3)
    toep_t = jnp.where(jnp.asarray(valid)[None, None], toep_t, NEG_BIG)
    t2 = jnp.concatenate([toep_t, toep_t], axis=3).reshape(h // 2, 2, 2 * WIN_R - 1, GRID_W, LANES)
    n_groups = rows // NAT_ROWS
    span_rows = NAT_SPAN * NAT_ROWS

    def plan(g):
        ks = int(np.clip(g - 1, 0, n_groups - NAT_SPAN)) * NAT_ROWS
        out = []
        for s in range(span_rows):
            for i in range(NAT_ROWS):
                rq, rk = NAT_ROWS * g + i, ks + s
                r0 = int(np.clip(rq - WIN_R // 2, 0, rows - WIN_R))
                out.append(rk - rq + WIN_R - 1 if r0 <= rk < r0 + WIN_R else None)
        return tuple(out)

    plans = (plan(0), plan(1), plan(n_groups - 1))
    assert all(plan(g) == plans[1] for g in range(1, n_groups - 1))
    n_col = 2 * NAT_ROWS * GRID_W // LANES
    return pl.pallas_call(
        functools.partial(_nat_table_kernel, plans=plans),
        grid=(h // 2,),
        in_specs=[pl.BlockSpec((1,) + t2.shape[1:], lambda p: (p, 0, 0, 0, 0))],
        out_specs=pl.BlockSpec((1, 3, n_col, span_rows * GRID_W, LANES), lambda p: (p, 0, 0, 0, 0)),
        out_shape=jax.ShapeDtypeStruct((h // 2, 3, n_col, span_rows * GRID_W, LANES), F32),
        compiler_params=_cparams(("parallel",)),
        name="natten_table",
    )(t2)


def kernel(x, c, ctx, c_ctx, norm_g, w_mod, b_mod, a_w_in, a_q_norm_g, a_k_norm_g, a_w_out,
           b_w_in, b_rpb, b_w_out, final_norm_g):
    bsz, t, d = x.shape
    lc = ctx.shape[1]

    n_rows = ((bsz + 1 + 7) // 8) * 8
    c_rows = jnp.zeros((n_rows, d), F32).at[:bsz].set(c).at[bsz].set(c_ctx)
    mods = _modulation(c_rows, w_mod, b_mod).reshape(w_mod.shape[0], n_rows, 3, 1, d)

    cos_t, sup, sdn = _rope_tables(t)
    one_t = jnp.ones((lc, LANES), F32)
    zero_t = jnp.zeros((lc, LANES), F32)
    head_block = np.kron(np.eye(LANES // HEAD_DIM), np.ones((HEAD_DIM, HEAD_DIM)))
    bd = jnp.asarray(head_block, BF16)
    two = lambda g: jnp.concatenate([g, g]).reshape(1, LANES).astype(F32)

    mx, mc = mods[0, :bsz], mods[0, bsz:bsz + 1]
    ng = norm_g[0].reshape(1, d)
    w_in = a_w_in[0].astype(BF16)
    qg, kg = two(a_q_norm_g[0]) * (ATTN_SCALE * LOG2E), two(a_k_norm_g[0])
    q, k, vt, gate = _inproj_a(x, ng, mx, w_in, qg, kg, bd, cos_t, sup, sdn, tm=512)
    qc, kc, vct, gate_c = _inproj_a(ctx, ng, mc, w_in, qg, kg, bd, one_t, zero_t, zero_t, tm=lc)
    w_out = a_w_out[0].astype(BF16)
    x1 = _attn_a(q, k, vt, kc, vct, gate, x, mx, w_out, tq=512)
    ctx1 = _attn_a(qc, None, None, kc, vct, gate_c, ctx, mc, w_out, tq=lc)

    mx, mc = mods[1, :bsz], mods[1, bsz:bsz + 1]
    ng = norm_g[1].reshape(1, d)
    w_in = b_w_in[0].astype(BF16)
    q, k, vt, gate = _inproj_b(x1, ng, mx, w_in, tm=256, kv_only=False)
    kc, vct = _inproj_b(ctx1, ng, mc, w_in[:, B_WIDTH:3 * B_WIDTH], tm=lc, kv_only=True)
    o = _natten(q, k, vt, kc, vct, _natten_tables(b_rpb[0], t // GRID_W))
    return _outproj_final(o, gate, x1, mx, b_w_out[0].astype(BF16), final_norm_g.reshape(1, d), tm=512)
```

```python
import functools

import jax
import jax.numpy as jnp
import numpy as np
from jax import lax
from jax.experimental import pallas as pl
from jax.experimental.pallas import tpu as pltpu

F32 = jnp.float32
BF16 = jnp.bfloat16

LANES = 128
VMEM_LIMIT = 56 * 1024 * 1024

D_MODEL = 1024
GRID_W = 64
HEAD_DIM = 64
NORM_EPS = 1e-6
ATTN_SCALE = HEAD_DIM ** -0.5
A_HEADS = 16
A_KV_HEADS = 4
A_WIDTH = A_HEADS * HEAD_DIM
A_KV_WIDTH = A_KV_HEADS * HEAD_DIM
A_GROUP = A_HEADS // A_KV_HEADS
ROPE_THETA = 10000.0
ROPE_AXIS_DIM = HEAD_DIM // 2
ROPE_HALF = ROPE_AXIS_DIM // 2
B_HEADS = 16
B_WIDTH = B_HEADS * HEAD_DIM
WIN_R = 8
WIN_C = 16
N_PAIRS = D_MODEL // LANES
NEG_BIG = -1e30
MAX_ROWS = 128
EXP_ROWS = 64
SUM_ROWS = 16
NAT_ROWS = 4
NAT_CHUNK = NAT_ROWS * GRID_W
NAT_SPAN = 3
NAT_KEYS = NAT_SPAN * NAT_CHUNK
N_SCORE_BUFS = 3
LOG2E = 1.4426950408889634


def _cparams(sem, flags=None):
    return pltpu.CompilerParams(dimension_semantics=sem, vmem_limit_bytes=VMEM_LIMIT, flags=flags)


def _mod_kernel(c_ref, w_ref, b_ref, o_ref):
    c = c_ref[...]
    s = c * jax.nn.sigmoid(c)
    o_ref[0] = jnp.dot(s, w_ref[0], precision=lax.Precision.HIGHEST,
                       preferred_element_type=F32) + b_ref[0]


def _modulation(c_rows, w_mod, b_mod):
    depth, d, n = w_mod.shape
    rows = c_rows.shape[0]
    tn = 1024
    return pl.pallas_call(
        _mod_kernel,
        grid=(depth, n // tn),
        in_specs=[
            pl.BlockSpec((rows, d), lambda l, j: (0, 0)),
            pl.BlockSpec((1, d, tn), lambda l, j: (l, 0, j)),
            pl.BlockSpec((1, 1, tn), lambda l, j: (l, 0, j)),
        ],
        out_specs=pl.BlockSpec((1, rows, tn), lambda l, j: (l, 0, j)),
        out_shape=jax.ShapeDtypeStruct((depth, rows, n), F32),
        compiler_params=_cparams(("arbitrary", "arbitrary")),
        name="adaln_mod",
    )(c_rows, w_mod, b_mod.reshape(depth, 1, n))


def _adaln(x, ng, sc, sh):
    ms = jnp.mean(x * x, axis=-1, keepdims=True)
    y = x * lax.rsqrt(ms + NORM_EPS) * ng
    return y * (1.0 + sc) + sh


def _silu(z):
    return z * jax.nn.sigmoid(z)


def _head_norm_rope(blk, gain, bd, cos_t, sin_up, sin_dn):
    sq = blk * blk
    hi = sq.astype(BF16)
    lo = (sq - hi.astype(F32)).astype(BF16)
    ssum = (jnp.dot(hi, bd, preferred_element_type=F32) + jnp.dot(lo, bd, preferred_element_type=F32))
    n = blk * lax.rsqrt(ssum * (1.0 / HEAD_DIM) + NORM_EPS) * gain
    up = pltpu.roll(n, LANES - ROPE_HALF, 1)
    dn = pltpu.roll(n, ROPE_HALF, 1)
    return n * cos_t + up * sin_up + dn * sin_dn


def _inproj_a_kernel(x_ref, ng_ref, sc_ref, sh_ref, w_ref, qg_ref, kg_ref, bd_ref, cos_ref, sup_ref, sdn_ref,
                     q_ref, k_ref, v_ref, g_ref):
    h = _adaln(x_ref[0], ng_ref[...], sc_ref[0, 0], sh_ref[0, 0])
    p = jnp.dot(h.astype(BF16), w_ref[...], preferred_element_type=F32)
    bd = bd_ref[...]
    cos_t, sup, sdn = cos_ref[...], sup_ref[...], sdn_ref[...]
    qg, kg = qg_ref[...], kg_ref[...]
    for j in range(A_WIDTH // LANES):
        blk = p[:, j * LANES:(j + 1) * LANES]
        q_ref[0, :, j * LANES:(j + 1) * LANES] = _head_norm_rope(blk, qg, bd, cos_t, sup, sdn).astype(BF16)
    for j in range(A_KV_WIDTH // LANES):
        blk = p[:, A_WIDTH + j * LANES:A_WIDTH + (j + 1) * LANES]
        k_ref[0, :, j * LANES:(j + 1) * LANES] = _head_norm_rope(blk, kg, bd, cos_t, sup, sdn).astype(BF16)
    v = p[:, A_WIDTH + A_KV_WIDTH:A_WIDTH + 2 * A_KV_WIDTH]
    tk = v_ref.shape[3]
    for c in range(v.shape[0] // tk):
        v_ref[0, c] = v[c * tk:(c + 1) * tk].T.astype(BF16)
    g_ref[0] = _silu(p[:, A_WIDTH + 2 * A_KV_WIDTH:]).astype(BF16)


def _inproj_a(x, ng, mods, w, qg, kg, bd, cos_t, sup, sdn, tm, tk):
    b, r, d = x.shape
    n = w.shape[1]
    shared = mods.shape[0] == 1
    mod_idx = (lambda i, j, c: (0, c, 0, 0)) if shared else (lambda i, j, c: (i, c, 0, 0))
    row = lambda i, j: (i, j, 0)
    const2 = lambda i, j: (0, 0)
    return pl.pallas_call(
        _inproj_a_kernel,
        grid=(b, r // tm),
        in_specs=[
            pl.BlockSpec((1, tm, d), row),
            pl.BlockSpec((1, d), const2),
            pl.BlockSpec((1, 1, 1, d), lambda i, j: mod_idx(i, j, 1)),
            pl.BlockSpec((1, 1, 1, d), lambda i, j: mod_idx(i, j, 0)),
            pl.BlockSpec((d, n), const2),
            pl.BlockSpec((1, LANES), const2),
            pl.BlockSpec((1, LANES), const2),
            pl.BlockSpec((LANES, LANES), const2),
            pl.BlockSpec((tm, LANES), lambda i, j: (j, 0)),
            pl.BlockSpec((tm, LANES), lambda i, j: (j, 0)),
            pl.BlockSpec((tm, LANES), lambda i, j: (j, 0)),
        ],
        out_specs=[
            pl.BlockSpec((1, tm, A_WIDTH), row),
            pl.BlockSpec((1, tm, A_KV_WIDTH), row),
            pl.BlockSpec((1, tm // tk, A_KV_WIDTH, tk), lambda i, j: (i, j, 0, 0)),
            pl.BlockSpec((1, tm, A_WIDTH), row),
        ],
        out_shape=[
            jax.ShapeDtypeStruct((b, r, A_WIDTH), BF16),
            jax.ShapeDtypeStruct((b, r, A_KV_WIDTH), BF16),
            jax.ShapeDtypeStruct((b, r // tk, A_KV_WIDTH, tk), BF16),
            jax.ShapeDtypeStruct((b, r, A_WIDTH), BF16),
        ],
        compiler_params=_cparams(("parallel", "parallel")),
        name="inproj_a",
    )(x, ng, mods, mods, w, qg, kg, bd, cos_t, sup, sdn)


def _stack_qt(q4, half):
    qt = q4.astype(F32).T
    zero = jnp.zeros((HEAD_DIM, qt.shape[1]), F32)
    first = half == 0

    def place(h_t):
        return jnp.where(first, jnp.concatenate([h_t, zero], axis=0), jnp.concatenate([zero, h_t], axis=0))

    heads = [place(qt[h * HEAD_DIM:(h + 1) * HEAD_DIM]) for h in range(A_GROUP)]
    return jnp.concatenate(heads, axis=1).astype(BF16)


def _store_cols(scr, row_start, val):
    n = val.shape[0]
    for j in range(scr.shape[0]):
        scr[j, row_start:row_start + n, :] = val[:, j * LANES:(j + 1) * LANES]


def _load_cols(scr, nk):
    return jnp.concatenate([scr[j, 0:nk, :] for j in range(scr.shape[0])], axis=1)


def _softmax_cols(s_scr, p_scr, nk, m):
    m_out, a_out = [], []
    for j in range(s_scr.shape[0]):
        sl = slice(j * LANES, (j + 1) * LANES)
        mx = s_scr[j, 0:MAX_ROWS, :]
        for r in range(1, nk // MAX_ROWS):
            mx = jnp.maximum(mx, s_scr[j, r * MAX_ROWS:(r + 1) * MAX_ROWS, :])
        m_new = jnp.maximum(m[:, sl], jnp.max(mx, axis=0, keepdims=True))
        for r in range(nk // EXP_ROWS):
            rows = slice(r * EXP_ROWS, (r + 1) * EXP_ROWS)
            p_scr[j, rows, :] = jnp.exp2(s_scr[j, rows, :] - m_new).astype(BF16)
        m_out.append(m_new)
        a_out.append(jnp.exp2(m[:, sl] - m_new))
    cat = lambda xs: jnp.concatenate(xs, axis=1)
    return cat(m_out), cat(a_out)


def _attn_a_kernel(*refs, tq, tk, n_x_chunks):
    n_in = 10 if n_x_chunks else 8
    if n_x_chunks:
        q_ref, kx_ref, vxt_ref, kc_ref, vct_ref, gate_ref, x_ref, gx_ref, w_ref, out_ref = refs[:n_in]
    else:
        q_ref, kc_ref, vct_ref, gate_ref, x_ref, gx_ref, w_ref, out_ref = refs[:n_in]
    o_scr = refs[n_in]
    s_bufs = refs[n_in + 1:n_in + 1 + N_SCORE_BUFS]
    p_bufs = refs[n_in + 1 + N_SCORE_BUFS:]
    kv_head = pl.program_id(2)
    vrow = pl.multiple_of(kv_head * HEAD_DIM, HEAD_DIM)
    a_t = _stack_qt(q_ref[0], kv_head % 2)
    cols = A_GROUP * tq
    lc = kc_ref.shape[1]

    n_chunks = 1 + n_x_chunks
    chunk_rows = lambda i: lc if i == 0 else tk
    k_blk = lambda i: kc_ref[0] if i == 0 else kx_ref[0, (i - 1) * tk:i * tk, :]
    v_blk = lambda i: (vct_ref[0, 0, pl.ds(vrow, HEAD_DIM), :] if i == 0
                       else vxt_ref[0, i - 1, pl.ds(vrow, HEAD_DIM), :])

    def scores(i):
        _store_cols(s_bufs[i % N_SCORE_BUFS], 0, jnp.dot(k_blk(i), a_t, preferred_element_type=F32))

    def consume(i, carry):
        m, acc = carry
        nk = chunk_rows(i)
        p_scr = p_bufs[i % len(p_bufs)]
        m, alpha = _softmax_cols(s_bufs[i % N_SCORE_BUFS], p_scr, nk, m)
        v_ext = jnp.concatenate([v_blk(i), jnp.ones((SUM_ROWS, nk), BF16)], axis=0)
        acc = alpha * acc + jnp.dot(v_ext, _load_cols(p_scr, nk), preferred_element_type=F32)
        return m, acc

    carry = (jnp.full((1, cols), NEG_BIG, F32), jnp.zeros((HEAD_DIM + SUM_ROWS, cols), F32))
    scores(0)
    for i in range(n_chunks):
        if i + 1 < n_chunks:
            scores(i + 1)
        carry = consume(i, carry)
    _, acc = carry
    o_t = acc[:HEAD_DIM] / acc[HEAD_DIM:HEAD_DIM + 1]
    o_scr[kv_head] = jnp.concatenate([o_t[:, h * tq:(h + 1) * tq] for h in range(A_GROUP)], axis=0).T

    @pl.when(kv_head == A_KV_HEADS - 1)
    def _():
        o_full = jnp.concatenate([o_scr[j] for j in range(A_KV_HEADS)], axis=1)
        u = (o_full * gate_ref[0].astype(F32)).astype(BF16)
        y = jnp.dot(u, w_ref[...], preferred_element_type=F32)
        out_ref[0] = x_ref[0] + gx_ref[0, 0] * y


def _attn_a(q, kx, vxt, kc, vct, gate, x, mods, w_out, tq):
    b, r, d = x.shape
    has_x = kx is not None
    shared = mods.shape[0] == 1
    qrow = lambda i, j, h: (i, j, 0)
    kvp = lambda i, j, h: (i, 0, h // 2)
    whole = lambda i, j, h: (i, 0, 0, 0)
    in_specs = [pl.BlockSpec((1, tq, A_GROUP * HEAD_DIM), lambda i, j, h: (i, j, h))]
    args = [q]
    lc = kc.shape[1]
    tk = lc
    n_x_chunks = 0
    if has_x:
        t = kx.shape[1]
        n_x_chunks, tk = vxt.shape[1], vxt.shape[3]
        in_specs += [pl.BlockSpec((1, t, LANES), kvp), pl.BlockSpec((1,) + vxt.shape[1:], whole)]
        args += [kx, vxt]
    in_specs += [
        pl.BlockSpec((1, lc, LANES), kvp),
        pl.BlockSpec((1,) + vct.shape[1:], whole),
        pl.BlockSpec((1, tq, d), qrow),
        pl.BlockSpec((1, tq, d), qrow),
        pl.BlockSpec((1, 1, 1, d), (lambda i, j, h: (0, 2, 0, 0)) if shared else (lambda i, j, h: (i, 2, 0, 0))),
        pl.BlockSpec((d, d), lambda i, j, h: (0, 0)),
    ]
    args += [kc, vct, gate, x, mods, w_out]
    return pl.pallas_call(
        functools.partial(_attn_a_kernel, tq=tq, tk=tk, n_x_chunks=n_x_chunks),
        grid=(b, r // tq, A_KV_HEADS),
        in_specs=in_specs,
        out_specs=pl.BlockSpec((1, tq, d), qrow),
        out_shape=jax.ShapeDtypeStruct((b, r, d), F32),
        scratch_shapes=[
            pltpu.VMEM((A_KV_HEADS, tq, A_GROUP * HEAD_DIM), F32),
            *[pltpu.VMEM((A_GROUP * tq // LANES, max(tk, lc), LANES), F32)] * N_SCORE_BUFS,
            *[pltpu.VMEM((A_GROUP * tq // LANES, max(tk, lc), LANES), BF16)] * 2,
        ],
        compiler_params=_cparams(("parallel", "parallel", "arbitrary")),
        name="attn_a_x" if has_x else "attn_a_ctx",
    )(*args)


def _store_vt_chunks(v_ref, v):
    for c in range(v.shape[0] // NAT_CHUNK):
        v_ref[0, c] = v[c * NAT_CHUNK:(c + 1) * NAT_CHUNK].T.astype(BF16)


def _inproj_b_kernel(x_ref, ng_ref, sc_ref, sh_ref, w_ref, *out_refs, kv_only):
    h = _adaln(x_ref[0], ng_ref[...], sc_ref[0, 0], sh_ref[0, 0])
    p = jnp.dot(h.astype(BF16), w_ref[...], preferred_element_type=F32)
    if kv_only:
        k_ref, v_ref = out_refs
        k_ref[0] = p[:, :B_WIDTH].astype(BF16)
        _store_vt_chunks(v_ref, p[:, B_WIDTH:])
    else:
        q_ref, k_ref, v_ref, g_ref = out_refs
        q_ref[0] = (p[:, :B_WIDTH] * (ATTN_SCALE * LOG2E)).astype(BF16)
        k_ref[0] = p[:, B_WIDTH:2 * B_WIDTH].astype(BF16)
        _store_vt_chunks(v_ref, p[:, 2 * B_WIDTH:3 * B_WIDTH])
        g_ref[0] = _silu(p[:, 3 * B_WIDTH:]).astype(BF16)


def _inproj_b(x, ng, mods, w, tm, kv_only):
    b, r, d = x.shape
    n = w.shape[1]
    shared = mods.shape[0] == 1
    mod_idx = (lambda i, c: (0, c, 0, 0)) if shared else (lambda i, c: (i, c, 0, 0))
    row = lambda i, j: (i, j, 0)
    row_spec = pl.BlockSpec((1, tm, B_WIDTH), row)
    row_shape = jax.ShapeDtypeStruct((b, r, B_WIDTH), BF16)
    vt_spec = pl.BlockSpec((1, tm // NAT_CHUNK, B_WIDTH, NAT_CHUNK), lambda i, j: (i, j, 0, 0))
    vt_shape = jax.ShapeDtypeStruct((b, r // NAT_CHUNK, B_WIDTH, NAT_CHUNK), BF16)
    if kv_only:
        out_specs, out_shape = [row_spec, vt_spec], [row_shape, vt_shape]
    else:
        out_specs, out_shape = [row_spec, row_spec, vt_spec, row_spec], [row_shape, row_shape, vt_shape, row_shape]
    return pl.pallas_call(
        functools.partial(_inproj_b_kernel, kv_only=kv_only),
        grid=(b, r // tm),
        in_specs=[
            pl.BlockSpec((1, tm, d), row),
            pl.BlockSpec((1, d), lambda i, j: (0, 0)),
            pl.BlockSpec((1, 1, 1, d), lambda i, j: mod_idx(i, 1)),
            pl.BlockSpec((1, 1, 1, d), lambda i, j: mod_idx(i, 0)),
            pl.BlockSpec((d, n), lambda i, j: (0, 0)),
        ],
        out_specs=out_specs,
        out_shape=out_shape,
        compiler_params=_cparams(("parallel", "parallel")),
        name="inproj_b_ctx" if kv_only else "inproj_b_x",
    )(x, ng, mods, mods, w)


def _nat_scores(g, n_groups, q_ref, k_ref, kc_ref, tab_ref, s_scr):
    g = jnp.asarray(g, jnp.int32)
    c0 = jnp.clip(g - 1, 0, n_groups - NAT_SPAN)
    kind = (g > 0).astype(jnp.int32) + (g == n_groups - 1).astype(jnp.int32)
    qs = pl.multiple_of(g * NAT_CHUNK, NAT_CHUNK)
    ks = pl.multiple_of(c0 * NAT_CHUNK, NAT_CHUNK)
    qt = q_ref[0, pl.ds(qs, NAT_CHUNK), :].astype(F32).T
    top = lax.broadcasted_iota(jnp.int32, qt.shape, 0) < HEAD_DIM
    a_t = jnp.concatenate([jnp.where(top, qt, 0.0), jnp.where(top, 0.0, qt)], axis=1).astype(BF16)
    s_span = jnp.dot(k_ref[0, pl.ds(ks, NAT_KEYS), :], a_t, preferred_element_type=F32)
    for j in range(s_scr.shape[0]):
        s_scr[j, 0:NAT_KEYS, :] = s_span[:, j * LANES:(j + 1) * LANES] + tab_ref[0, kind, j]
    _store_cols(s_scr, NAT_KEYS, jnp.dot(kc_ref[0], a_t, preferred_element_type=F32))


def _nat_softmax(s_scr, p_scr):
    nk = s_scr.shape[1]
    for j in range(s_scr.shape[0]):
        mx = s_scr[j, 0:MAX_ROWS, :]
        for r in range(1, nk // MAX_ROWS):
            mx = jnp.maximum(mx, s_scr[j, r * MAX_ROWS:(r + 1) * MAX_ROWS, :])
        m = jnp.max(mx, axis=0, keepdims=True)
        for r in range(nk // EXP_ROWS):
            rows = slice(r * EXP_ROWS, (r + 1) * EXP_ROWS)
            p_scr[j, rows, :] = jnp.exp2(s_scr[j, rows, :] - m).astype(BF16)


def _nat_output(g, n_groups, vt_ref, vct_ref, p_scr, o_ref):
    nk = p_scr.shape[1]
    c0 = jnp.clip(g - 1, 0, n_groups - NAT_SPAN)
    qs = pl.multiple_of(g * NAT_CHUNK, NAT_CHUNK)
    v_t = jnp.concatenate([vt_ref[0, c0 + c] for c in range(NAT_SPAN)] + [vct_ref[0, 0]], axis=1)
    v_ext = jnp.concatenate([v_t, jnp.ones((SUM_ROWS, nk), BF16)], axis=0)
    acc = jnp.dot(v_ext, _load_cols(p_scr, nk), preferred_element_type=F32)
    o_t = acc[:LANES] / acc[LANES:LANES + 1]
    both = jnp.concatenate([o_t[:HEAD_DIM, :NAT_CHUNK], o_t[HEAD_DIM:, NAT_CHUNK:]], axis=0)
    o_ref[0, pl.ds(qs, NAT_CHUNK), :] = both.T.astype(BF16)


def _natten_kernel(q_ref, k_ref, vt_ref, kc_ref, vct_ref, tab_ref, o_ref, s0, s1, p0, p1, *, n_groups):
    def body(i, carry):
        for u, (s_scr, p_scr) in enumerate(((s0, p0), (s1, p1))):
            g = 2 * i + u
            _nat_scores(g, n_groups, q_ref, k_ref, kc_ref, tab_ref, s_scr)
            _nat_softmax(s_scr, p_scr)
            _nat_output(g, n_groups, vt_ref, vct_ref, p_scr, o_ref)
        return carry

    lax.fori_loop(0, n_groups // 2, body, 0)


def _natten(q, k, vt, kc, vct, tab):
    b, t, d = q.shape
    lc = kc.shape[1]
    n_groups = t // NAT_CHUNK
    nk = NAT_KEYS + lc
    blk = lambda h, i: (i, 0, h)
    return pl.pallas_call(
        functools.partial(_natten_kernel, n_groups=n_groups),
        grid=(N_PAIRS, b),
        in_specs=[
            pl.BlockSpec((1, t, LANES), blk),
            pl.BlockSpec((1, t, LANES), blk),
            pl.BlockSpec((1, n_groups, LANES, NAT_CHUNK), lambda h, i: (i, 0, h, 0)),
            pl.BlockSpec((1, lc, LANES), blk),
            pl.BlockSpec((1, 1, LANES, lc), lambda h, i: (i, 0, h, 0)),
            pl.BlockSpec((1,) + tab.shape[1:], lambda h, i: (h, 0, 0, 0, 0)),
        ],
        out_specs=pl.BlockSpec((1, t, LANES), blk),
        out_shape=jax.ShapeDtypeStruct((b, t, d), BF16),
        scratch_shapes=[pltpu.VMEM((2 * NAT_CHUNK // LANES, nk, LANES), F32)] * 2
        + [pltpu.VMEM((2 * NAT_CHUNK // LANES, nk, LANES), BF16)] * 2,
        compiler_params=_cparams(("parallel", "parallel")),
        name="natten_b",
    )(q, k, vt, kc, vct, tab)


def _outproj_final_kernel(o_ref, gate_ref, x_ref, gx_ref, w_ref, fg_ref, out_ref):
    u = (o_ref[0].astype(F32) * gate_ref[0].astype(F32)).astype(BF16)
    y = jnp.dot(u, w_ref[...], preferred_element_type=F32)
    x2 = x_ref[0] + gx_ref[0, 0] * y
    ms = jnp.mean(x2 * x2, axis=-1, keepdims=True)
    out_ref[0] = x2 * lax.rsqrt(ms + NORM_EPS) * fg_ref[...]


def _outproj_final(o, gate, x, mods, w_out, fg, tm):
    b, t, d = x.shape
    row = lambda i, j: (i, j, 0)
    return pl.pallas_call(
        _outproj_final_kernel,
        grid=(b, t // tm),
        in_specs=[
            pl.BlockSpec((1, tm, d), row),
            pl.BlockSpec((1, tm, d), row),
            pl.BlockSpec((1, tm, d), row),
            pl.BlockSpec((1, 1, 1, d), lambda i, j: (i, 2, 0, 0)),
            pl.BlockSpec((d, d), lambda i, j: (0, 0)),
            pl.BlockSpec((1, d), lambda i, j: (0, 0)),
        ],
        out_specs=pl.BlockSpec((1, tm, d), row),
        out_shape=jax.ShapeDtypeStruct((b, t, d), F32),
        compiler_params=_cparams(("parallel", "parallel")),
        name="outproj_final",
    )(o, gate, x, mods, w_out, fg)


def _rope_tables(t_len):
    pos = jnp.arange(t_len, dtype=jnp.int32)
    row = (pos // GRID_W).astype(F32)
    col = (pos % GRID_W).astype(F32)
    inv = ROPE_THETA ** (-jnp.arange(0, ROPE_AXIS_DIM, 2, dtype=F32) / ROPE_AXIS_DIM)
    ang_r = row[:, None] * inv
    ang_c = col[:, None] * inv
    zero = jnp.zeros_like(ang_r)
    cos_h = jnp.concatenate([jnp.cos(ang_r), jnp.cos(ang_r), jnp.cos(ang_c), jnp.cos(ang_c)], axis=1)
    sup_h = jnp.concatenate([-jnp.sin(ang_r), zero, -jnp.sin(ang_c), zero], axis=1)
    sdn_h = jnp.concatenate([zero, jnp.sin(ang_r), zero, jnp.sin(ang_c)], axis=1)
    two = lambda a: jnp.concatenate([a, a], axis=1)
    return two(cos_h), two(sup_h), two(sdn_h)


def _nat_table_kernel(t2_ref, o_ref, *, plans):
    low = lax.broadcasted_iota(jnp.int32, (GRID_W, LANES), 1) < GRID_W
    neg = jnp.full((GRID_W, LANES), NEG_BIG, F32)
    half_rows = NAT_ROWS // 2
    for kind, plan in enumerate(plans):
        for head in range(2):
            for i2 in range(half_rows):
                for s in range(NAT_SPAN * NAT_ROWS):
                    d0, d1 = plan[s * NAT_ROWS + 2 * i2], plan[s * NAT_ROWS + 2 * i2 + 1]
                    b0 = neg if d0 is None else t2_ref[0, head, d0]
                    b1 = neg if d1 is None else t2_ref[0, head, d1]
                    o_ref[0, kind, head * half_rows + i2, s * GRID_W:(s + 1) * GRID_W, :] = jnp.where(low, b0, b1)


def _natten_tables(rpb, rows):
    h = rpb.shape[0]
    qcol = np.arange(GRID_W)
    c0 = np.clip(qcol - WIN_C // 2, 0, GRID_W - WIN_C)
    kcol = np.arange(GRID_W)
    valid = (kcol[:, None] >= c0[None, :]) & (kcol[:, None] < c0[None, :] + WIN_C)
    pad = GRID_W - WIN_C
    flipped = jnp.pad(rpb.astype(F32) * LOG2E, ((0, 0), (0, 0), (pad, pad)))[:, :, ::-1]
    rows2 = [jnp.concatenate([flipped[:, :, GRID_W - 1 - k:2 * GRID_W - 1 - k]] * 2, axis=2) for k in range(GRID_W)]
    t2 = jnp.where(jnp.asarray(np.concatenate([valid, valid], axis=1))[None, None], jnp.stack(rows2, axis=2), NEG_BIG)
    t2 = t2.reshape(h // 2, 2, 2 * WIN_R - 1, GRID_W, LANES)
    n_groups = rows // NAT_ROWS
    span_rows = NAT_SPAN * NAT_ROWS

    def plan(g):
        ks = int(np.clip(g - 1, 0, n_groups - NAT_SPAN)) * NAT_ROWS
        out = []
        for s in range(span_rows):
            for i in range(NAT_ROWS):
                rq, rk = NAT_ROWS * g + i, ks + s
                r0 = int(np.clip(rq - WIN_R // 2, 0, rows - WIN_R))
                out.append(rk - rq + WIN_R - 1 if r0 <= rk < r0 + WIN_R else None)
        return tuple(out)

    plans = (plan(0), plan(1), plan(n_groups - 1))
    assert all(plan(g) == plans[1] for g in range(1, n_groups - 1))
    n_col = 2 * NAT_ROWS * GRID_W // LANES
    return pl.pallas_call(
        functools.partial(_nat_table_kernel, plans=plans),
        grid=(h // 2,),
        in_specs=[pl.BlockSpec((1,) + t2.shape[1:], lambda p: (p, 0, 0, 0, 0))],
        out_specs=pl.BlockSpec((1, 3, n_col, span_rows * GRID_W, LANES), lambda p: (p, 0, 0, 0, 0)),
        out_shape=jax.ShapeDtypeStruct((h // 2, 3, n_col, span_rows * GRID_W, LANES), F32),
        compiler_params=_cparams(("parallel",)),
        name="natten_table",
    )(t2)


def kernel(x, c, ctx, c_ctx, norm_g, w_mod, b_mod, a_w_in, a_q_norm_g, a_k_norm_g, a_w_out,
           b_w_in, b_rpb, b_w_out, final_norm_g):
    bsz, t, d = x.shape
    lc = ctx.shape[1]

    n_rows = ((bsz + 1 + 7) // 8) * 8
    c_rows = jnp.zeros((n_rows, d), F32).at[:bsz].set(c).at[bsz].set(c_ctx)
    mods = _modulation(c_rows, w_mod, b_mod).reshape(w_mod.shape[0], n_rows, 3, 1, d)

    cos_t, sup, sdn = _rope_tables(t)
    one_t = jnp.ones((lc, LANES), F32)
    zero_t = jnp.zeros((lc, LANES), F32)
    head_block = np.kron(np.eye(LANES // HEAD_DIM), np.ones((HEAD_DIM, HEAD_DIM)))
    bd = jnp.asarray(head_block, BF16)
    two = lambda g: jnp.concatenate([g, g]).reshape(1, LANES).astype(F32)

    mx, mc = mods[0, :bsz], mods[0, bsz:bsz + 1]
    ng = norm_g[0].reshape(1, d)
    w_in = a_w_in[0].astype(BF16)
    qg, kg = two(a_q_norm_g[0]) * (ATTN_SCALE * LOG2E), two(a_k_norm_g[0])
    q, k, vt, gate = _inproj_a(x, ng, mx, w_in, qg, kg, bd, cos_t, sup, sdn, tm=1024, tk=512)
    qc, kc, vct, gate_c = _inproj_a(ctx, ng, mc, w_in, qg, kg, bd, one_t, zero_t, zero_t, tm=lc, tk=lc)
    w_out = a_w_out[0].astype(BF16)
    x1 = _attn_a(q, k, vt, kc, vct, gate, x, mx, w_out, tq=512)
    ctx1 = _attn_a(qc, None, None, kc, vct, gate_c, ctx, mc, w_out, tq=lc)

    mx, mc = mods[1, :bsz], mods[1, bsz:bsz + 1]
    ng = norm_g[1].reshape(1, d)
    w_in = b_w_in[0].astype(BF16)
    q, k, vt, gate = _inproj_b(x1, ng, mx, w_in, tm=512, kv_only=False)
    kc, vct = _inproj_b(ctx1, ng, mc, w_in[:, B_WIDTH:3 * B_WIDTH], tm=lc, kv_only=True)
    o = _natten(q, k, vt, kc, vct, _natten_tables(b_rpb[0], t // GRID_W))
    return _outproj_final(o, gate, x1, mx, b_w_out[0].astype(BF16), final_norm_g.reshape(1, d), tm=512)
```

```python
import functools

import jax
import jax.numpy as jnp
import numpy as np
from jax import lax
from jax.experimental import pallas as pl
from jax.experimental.pallas import tpu as pltpu

F32 = jnp.float32
BF16 = jnp.bfloat16

LANES = 128
VMEM_LIMIT = 56 * 1024 * 1024

D_MODEL = 1024
GRID_W = 64
HEAD_DIM = 64
NORM_EPS = 1e-6
ATTN_SCALE = HEAD_DIM ** -0.5
A_HEADS = 16
A_KV_HEADS = 4
A_WIDTH = A_HEADS * HEAD_DIM
A_KV_WIDTH = A_KV_HEADS * HEAD_DIM
A_GROUP = A_HEADS // A_KV_HEADS
ROPE_THETA = 10000.0
ROPE_AXIS_DIM = HEAD_DIM // 2
ROPE_HALF = ROPE_AXIS_DIM // 2
B_HEADS = 16
B_WIDTH = B_HEADS * HEAD_DIM
WIN_R = 8
WIN_C = 16
N_PAIRS = D_MODEL // LANES
NEG_BIG = -1e30
MAX_ROWS = 128
EXP_ROWS = 64
SUM_ROWS = 16
NAT_ROWS = 4
NAT_CHUNK = NAT_ROWS * GRID_W
NAT_SPAN = 3
NAT_KEYS = NAT_SPAN * NAT_CHUNK
NAT_BUFS = 3
N_SCORE_BUFS = 3
LOG2E = 1.4426950408889634


def _cparams(sem, flags=None):
    return pltpu.CompilerParams(dimension_semantics=sem, vmem_limit_bytes=VMEM_LIMIT, flags=flags)


def _mod_kernel(c_ref, w_ref, b_ref, o_ref):
    c = c_ref[...]
    s = c * jax.nn.sigmoid(c)
    o_ref[0] = jnp.dot(s, w_ref[0], precision=lax.Precision.HIGHEST,
                       preferred_element_type=F32) + b_ref[0]


def _modulation(c_rows, w_mod, b_mod):
    depth, d, n = w_mod.shape
    rows = c_rows.shape[0]
    tn = 1024
    return pl.pallas_call(
        _mod_kernel,
        grid=(depth, n // tn),
        in_specs=[
            pl.BlockSpec((rows, d), lambda l, j: (0, 0)),
            pl.BlockSpec((1, d, tn), lambda l, j: (l, 0, j)),
            pl.BlockSpec((1, 1, tn), lambda l, j: (l, 0, j)),
        ],
        out_specs=pl.BlockSpec((1, rows, tn), lambda l, j: (l, 0, j)),
        out_shape=jax.ShapeDtypeStruct((depth, rows, n), F32),
        compiler_params=_cparams(("arbitrary", "arbitrary")),
        name="adaln_mod",
    )(c_rows, w_mod, b_mod.reshape(depth, 1, n))


def _adaln(x, ng, sc, sh):
    ms = jnp.mean(x * x, axis=-1, keepdims=True)
    y = x * lax.rsqrt(ms + NORM_EPS) * ng
    return y * (1.0 + sc) + sh


def _silu(z):
    return z * jax.nn.sigmoid(z)


def _head_norm_rope(blk, gain, bd, cos_t, sin_up, sin_dn):
    sq = blk * blk
    hi = sq.astype(BF16)
    lo = (sq - hi.astype(F32)).astype(BF16)
    ssum = (jnp.dot(hi, bd, preferred_element_type=F32) + jnp.dot(lo, bd, preferred_element_type=F32))
    n = blk * lax.rsqrt(ssum * (1.0 / HEAD_DIM) + NORM_EPS) * gain
    up = pltpu.roll(n, LANES - ROPE_HALF, 1)
    dn = pltpu.roll(n, ROPE_HALF, 1)
    return n * cos_t + up * sin_up + dn * sin_dn


def _inproj_a_kernel(x_ref, ng_ref, sc_ref, sh_ref, w_ref, qg_ref, kg_ref, bd_ref, cos_ref, sup_ref, sdn_ref,
                     q_ref, k_ref, v_ref, g_ref):
    h = _adaln(x_ref[0], ng_ref[...], sc_ref[0, 0], sh_ref[0, 0])
    p = jnp.dot(h.astype(BF16), w_ref[...], preferred_element_type=F32)
    bd = bd_ref[...]
    cos_t, sup, sdn = cos_ref[...], sup_ref[...], sdn_ref[...]
    qg, kg = qg_ref[...], kg_ref[...]
    for j in range(A_WIDTH // LANES):
        blk = p[:, j * LANES:(j + 1) * LANES]
        q_ref[0, :, j * LANES:(j + 1) * LANES] = _head_norm_rope(blk, qg, bd, cos_t, sup, sdn).astype(BF16)
    for j in range(A_KV_WIDTH // LANES):
        blk = p[:, A_WIDTH + j * LANES:A_WIDTH + (j + 1) * LANES]
        k_ref[0, :, j * LANES:(j + 1) * LANES] = _head_norm_rope(blk, kg, bd, cos_t, sup, sdn).astype(BF16)
    v = p[:, A_WIDTH + A_KV_WIDTH:A_WIDTH + 2 * A_KV_WIDTH]
    tk = v_ref.shape[3]
    for c in range(v.shape[0] // tk):
        v_ref[0, c] = v[c * tk:(c + 1) * tk].T.astype(BF16)
    g_ref[0] = _silu(p[:, A_WIDTH + 2 * A_KV_WIDTH:]).astype(BF16)


def _inproj_a(x, ng, mods, w, qg, kg, bd, cos_t, sup, sdn, tm, tk):
    b, r, d = x.shape
    n = w.shape[1]
    shared = mods.shape[0] == 1
    mod_idx = (lambda i, j, c: (0, c, 0, 0)) if shared else (lambda i, j, c: (i, c, 0, 0))
    row = lambda i, j: (i, j, 0)
    const2 = lambda i, j: (0, 0)
    return pl.pallas_call(
        _inproj_a_kernel,
        grid=(b, r // tm),
        in_specs=[
            pl.BlockSpec((1, tm, d), row),
            pl.BlockSpec((1, d), const2),
            pl.BlockSpec((1, 1, 1, d), lambda i, j: mod_idx(i, j, 1)),
            pl.BlockSpec((1, 1, 1, d), lambda i, j: mod_idx(i, j, 0)),
            pl.BlockSpec((d, n), const2),
            pl.BlockSpec((1, LANES), const2),
            pl.BlockSpec((1, LANES), const2),
            pl.BlockSpec((LANES, LANES), const2),
            pl.BlockSpec((tm, LANES), lambda i, j: (j, 0)),
            pl.BlockSpec((tm, LANES), lambda i, j: (j, 0)),
            pl.BlockSpec((tm, LANES), lambda i, j: (j, 0)),
        ],
        out_specs=[
            pl.BlockSpec((1, tm, A_WIDTH), row),
            pl.BlockSpec((1, tm, A_KV_WIDTH), row),
            pl.BlockSpec((1, tm // tk, A_KV_WIDTH, tk), lambda i, j: (i, j, 0, 0)),
            pl.BlockSpec((1, tm, A_WIDTH), row),
        ],
        out_shape=[
            jax.ShapeDtypeStruct((b, r, A_WIDTH), BF16),
            jax.ShapeDtypeStruct((b, r, A_KV_WIDTH), BF16),
            jax.ShapeDtypeStruct((b, r // tk, A_KV_WIDTH, tk), BF16),
            jax.ShapeDtypeStruct((b, r, A_WIDTH), BF16),
        ],
        compiler_params=_cparams(("parallel", "parallel")),
        name="inproj_a",
    )(x, ng, mods, mods, w, qg, kg, bd, cos_t, sup, sdn)


def _stack_qt(q4, half):
    qt = q4.astype(F32).T
    zero = jnp.zeros((HEAD_DIM, qt.shape[1]), F32)
    first = half == 0

    def place(h_t):
        return jnp.where(first, jnp.concatenate([h_t, zero], axis=0), jnp.concatenate([zero, h_t], axis=0))

    heads = [place(qt[h * HEAD_DIM:(h + 1) * HEAD_DIM]) for h in range(A_GROUP)]
    return jnp.concatenate(heads, axis=1).astype(BF16)


def _store_cols(scr, row_start, val):
    n = val.shape[0]
    for j in range(scr.shape[0]):
        scr[j, row_start:row_start + n, :] = val[:, j * LANES:(j + 1) * LANES]


def _load_cols(scr, nk):
    return jnp.concatenate([scr[j, 0:nk, :] for j in range(scr.shape[0])], axis=1)


def _softmax_cols(s_scr, p_scr, nk, m):
    m_out, a_out = [], []
    for j in range(s_scr.shape[0]):
        sl = slice(j * LANES, (j + 1) * LANES)
        mx = s_scr[j, 0:MAX_ROWS, :]
        for r in range(1, nk // MAX_ROWS):
            mx = jnp.maximum(mx, s_scr[j, r * MAX_ROWS:(r + 1) * MAX_ROWS, :])
        m_new = jnp.maximum(m[:, sl], jnp.max(mx, axis=0, keepdims=True))
        for r in range(nk // EXP_ROWS):
            rows = slice(r * EXP_ROWS, (r + 1) * EXP_ROWS)
            p_scr[j, rows, :] = jnp.exp2(s_scr[j, rows, :] - m_new).astype(BF16)
        m_out.append(m_new)
        a_out.append(jnp.exp2(m[:, sl] - m_new))
    cat = lambda xs: jnp.concatenate(xs, axis=1)
    return cat(m_out), cat(a_out)


def _attn_a_kernel(*refs, tq, tk, n_x_chunks):
    n_in = 10 if n_x_chunks else 8
    if n_x_chunks:
        q_ref, kx_ref, vxt_ref, kc_ref, vct_ref, gate_ref, x_ref, gx_ref, w_ref, out_ref = refs[:n_in]
    else:
        q_ref, kc_ref, vct_ref, gate_ref, x_ref, gx_ref, w_ref, out_ref = refs[:n_in]
    o_scr = refs[n_in]
    s_bufs = refs[n_in + 1:n_in + 1 + N_SCORE_BUFS]
    p_bufs = refs[n_in + 1 + N_SCORE_BUFS:]
    kv_head = pl.program_id(2)
    vrow = pl.multiple_of(kv_head * HEAD_DIM, HEAD_DIM)
    a_t = _stack_qt(q_ref[0], kv_head % 2)
    cols = A_GROUP * tq
    lc = kc_ref.shape[1]

    n_chunks = 1 + n_x_chunks
    chunk_rows = lambda i: lc if i == 0 else tk
    k_blk = lambda i: kc_ref[0] if i == 0 else kx_ref[0, (i - 1) * tk:i * tk, :]
    v_blk = lambda i: (vct_ref[0, 0, pl.ds(vrow, HEAD_DIM), :] if i == 0
                       else vxt_ref[0, i - 1, pl.ds(vrow, HEAD_DIM), :])

    def scores(i):
        _store_cols(s_bufs[i % N_SCORE_BUFS], 0, jnp.dot(k_blk(i), a_t, preferred_element_type=F32))

    def consume(i, carry):
        m, acc = carry
        nk = chunk_rows(i)
        p_scr = p_bufs[i % len(p_bufs)]
        m, alpha = _softmax_cols(s_bufs[i % N_SCORE_BUFS], p_scr, nk, m)
        v_ext = jnp.concatenate([v_blk(i), jnp.ones((SUM_ROWS, nk), BF16)], axis=0)
        acc = alpha * acc + jnp.dot(v_ext, _load_cols(p_scr, nk), preferred_element_type=F32)
        return m, acc

    carry = (jnp.full((1, cols), NEG_BIG, F32), jnp.zeros((HEAD_DIM + SUM_ROWS, cols), F32))
    scores(0)
    for i in range(n_chunks):
        if i + 1 < n_chunks:
            scores(i + 1)
        carry = consume(i, carry)
    _, acc = carry
    o_t = acc[:HEAD_DIM] / acc[HEAD_DIM:HEAD_DIM + 1]
    o_scr[kv_head] = jnp.concatenate([o_t[:, h * tq:(h + 1) * tq] for h in range(A_GROUP)], axis=0).T

    @pl.when(kv_head == A_KV_HEADS - 1)
    def _():
        o_full = jnp.concatenate([o_scr[j] for j in range(A_KV_HEADS)], axis=1)
        u = (o_full * gate_ref[0].astype(F32)).astype(BF16)
        y = jnp.dot(u, w_ref[...], preferred_element_type=F32)
        out_ref[0] = x_ref[0] + gx_ref[0, 0] * y


def _attn_a(q, kx, vxt, kc, vct, gate, x, mods, w_out, tq):
    b, r, d = x.shape
    has_x = kx is not None
    shared = mods.shape[0] == 1
    qrow = lambda i, j, h: (i, j, 0)
    kvp = lambda i, j, h: (i, 0, h // 2)
    whole = lambda i, j, h: (i, 0, 0, 0)
    in_specs = [pl.BlockSpec((1, tq, A_GROUP * HEAD_DIM), lambda i, j, h: (i, j, h))]
    args = [q]
    lc = kc.shape[1]
    tk = lc
    n_x_chunks = 0
    if has_x:
        t = kx.shape[1]
        n_x_chunks, tk = vxt.shape[1], vxt.shape[3]
        in_specs += [pl.BlockSpec((1, t, LANES), kvp), pl.BlockSpec((1,) + vxt.shape[1:], whole)]
        args += [kx, vxt]
    in_specs += [
        pl.BlockSpec((1, lc, LANES), kvp),
        pl.BlockSpec((1,) + vct.shape[1:], whole),
        pl.BlockSpec((1, tq, d), qrow),
        pl.BlockSpec((1, tq, d), qrow),
        pl.BlockSpec((1, 1, 1, d), (lambda i, j, h: (0, 2, 0, 0)) if shared else (lambda i, j, h: (i, 2, 0, 0))),
        pl.BlockSpec((d, d), lambda i, j, h: (0, 0)),
    ]
    args += [kc, vct, gate, x, mods, w_out]
    return pl.pallas_call(
        functools.partial(_attn_a_kernel, tq=tq, tk=tk, n_x_chunks=n_x_chunks),
        grid=(b, r // tq, A_KV_HEADS),
        in_specs=in_specs,
        out_specs=pl.BlockSpec((1, tq, d), qrow),
        out_shape=jax.ShapeDtypeStruct((b, r, d), F32),
        scratch_shapes=[
            pltpu.VMEM((A_KV_HEADS, tq, A_GROUP * HEAD_DIM), F32),
            *[pltpu.VMEM((A_GROUP * tq // LANES, max(tk, lc), LANES), F32)] * N_SCORE_BUFS,
            *[pltpu.VMEM((A_GROUP * tq // LANES, max(tk, lc), LANES), BF16)] * 2,
        ],
        compiler_params=_cparams(("parallel", "parallel", "arbitrary")),
        name="attn_a_x" if has_x else "attn_a_ctx",
    )(*args)


def _store_vt_chunks(v_ref, v):
    for c in range(v.shape[0] // NAT_CHUNK):
        v_ref[0, c] = v[c * NAT_CHUNK:(c + 1) * NAT_CHUNK].T.astype(BF16)


def _inproj_b_kernel(x_ref, ng_ref, sc_ref, sh_ref, w_ref, *out_refs, kv_only):
    h = _adaln(x_ref[0], ng_ref[...], sc_ref[0, 0], sh_ref[0, 0])
    p = jnp.dot(h.astype(BF16), w_ref[...], preferred_element_type=F32)
    if kv_only:
        k_ref, v_ref = out_refs
        k_ref[0] = p[:, :B_WIDTH].astype(BF16)
        _store_vt_chunks(v_ref, p[:, B_WIDTH:])
    else:
        q_ref, k_ref, v_ref, g_ref = out_refs
        q_ref[0] = (p[:, :B_WIDTH] * (ATTN_SCALE * LOG2E)).astype(BF16)
        k_ref[0] = p[:, B_WIDTH:2 * B_WIDTH].astype(BF16)
        _store_vt_chunks(v_ref, p[:, 2 * B_WIDTH:3 * B_WIDTH])
        g_ref[0] = _silu(p[:, 3 * B_WIDTH:]).astype(BF16)


def _inproj_b(x, ng, mods, w, tm, kv_only):
    b, r, d = x.shape
    n = w.shape[1]
    shared = mods.shape[0] == 1
    mod_idx = (lambda i, c: (0, c, 0, 0)) if shared else (lambda i, c: (i, c, 0, 0))
    row = lambda i, j: (i, j, 0)
    row_spec = pl.BlockSpec((1, tm, B_WIDTH), row)
    row_shape = jax.ShapeDtypeStruct((b, r, B_WIDTH), BF16)
    vt_spec = pl.BlockSpec((1, tm // NAT_CHUNK, B_WIDTH, NAT_CHUNK), lambda i, j: (i, j, 0, 0))
    vt_shape = jax.ShapeDtypeStruct((b, r // NAT_CHUNK, B_WIDTH, NAT_CHUNK), BF16)
    if kv_only:
        out_specs, out_shape = [row_spec, vt_spec], [row_shape, vt_shape]
    else:
        out_specs, out_shape = [row_spec, row_spec, vt_spec, row_spec], [row_shape, row_shape, vt_shape, row_shape]
    return pl.pallas_call(
        functools.partial(_inproj_b_kernel, kv_only=kv_only),
        grid=(b, r // tm),
        in_specs=[
            pl.BlockSpec((1, tm, d), row),
            pl.BlockSpec((1, d), lambda i, j: (0, 0)),
            pl.BlockSpec((1, 1, 1, d), lambda i, j: mod_idx(i, 1)),
            pl.BlockSpec((1, 1, 1, d), lambda i, j: mod_idx(i, 0)),
            pl.BlockSpec((d, n), lambda i, j: (0, 0)),
        ],
        out_specs=out_specs,
        out_shape=out_shape,
        compiler_params=_cparams(("parallel", "parallel")),
        name="inproj_b_ctx" if kv_only else "inproj_b_x",
    )(x, ng, mods, mods, w)


def _nat_scores(g, n_groups, q_ref, k_ref, kc_ref, tab_ref, s_scr):
    c0 = min(max(g - 1, 0), n_groups - NAT_SPAN)
    kind = int(g > 0) + int(g == n_groups - 1)
    qs, ks = g * NAT_CHUNK, c0 * NAT_CHUNK
    qt = q_ref[0, qs:qs + NAT_CHUNK, :].astype(F32).T
    top = lax.broadcasted_iota(jnp.int32, qt.shape, 0) < HEAD_DIM
    a_t = jnp.concatenate([jnp.where(top, qt, 0.0), jnp.where(top, 0.0, qt)], axis=1).astype(BF16)
    s_span = jnp.dot(k_ref[0, ks:ks + NAT_KEYS, :], a_t, preferred_element_type=F32)
    for j in range(s_scr.shape[0]):
        s_scr[j, 0:NAT_KEYS, :] = s_span[:, j * LANES:(j + 1) * LANES] + tab_ref[0, kind, j]
    _store_cols(s_scr, NAT_KEYS, jnp.dot(kc_ref[0], a_t, preferred_element_type=F32))


def _nat_softmax(s_scr, p_scr):
    nk = s_scr.shape[1]
    for j in range(s_scr.shape[0]):
        mx = s_scr[j, 0:MAX_ROWS, :]
        for r in range(1, nk // MAX_ROWS):
            mx = jnp.maximum(mx, s_scr[j, r * MAX_ROWS:(r + 1) * MAX_ROWS, :])
        m = jnp.max(mx, axis=0, keepdims=True)
        for r in range(nk // EXP_ROWS):
            rows = slice(r * EXP_ROWS, (r + 1) * EXP_ROWS)
            p_scr[j, rows, :] = jnp.exp2(s_scr[j, rows, :] - m).astype(BF16)


def _nat_output(g, n_groups, vt_ref, vct_ref, p_scr, o_ref):
    nk = p_scr.shape[1]
    c0 = min(max(g - 1, 0), n_groups - NAT_SPAN)
    qs = g * NAT_CHUNK
    v_t = jnp.concatenate([vt_ref[0, c0 + c] for c in range(NAT_SPAN)] + [vct_ref[0, 0]], axis=1)
    v_ext = jnp.concatenate([v_t, jnp.ones((SUM_ROWS, nk), BF16)], axis=0)
    acc = jnp.dot(v_ext, _load_cols(p_scr, nk), preferred_element_type=F32)
    o_t = acc[:LANES] / acc[LANES:LANES + 1]
    both = jnp.concatenate([o_t[:HEAD_DIM, :NAT_CHUNK], o_t[HEAD_DIM:, NAT_CHUNK:]], axis=0)
    o_ref[0, qs:qs + NAT_CHUNK, :] = both.T.astype(BF16)


def _natten_kernel(q_ref, k_ref, vt_ref, kc_ref, vct_ref, tab_ref, o_ref, *scratch, n_groups):
    n_buf = len(scratch) // 2
    for g in range(n_groups):
        s_scr, p_scr = scratch[g % n_buf], scratch[n_buf + g % n_buf]
        _nat_scores(g, n_groups, q_ref, k_ref, kc_ref, tab_ref, s_scr)
        _nat_softmax(s_scr, p_scr)
        _nat_output(g, n_groups, vt_ref, vct_ref, p_scr, o_ref)


def _natten(q, k, vt, kc, vct, tab):
    b, t, d = q.shape
    lc = kc.shape[1]
    n_groups = t // NAT_CHUNK
    nk = NAT_KEYS + lc
    blk = lambda h, i: (i, 0, h)
    return pl.pallas_call(
        functools.partial(_natten_kernel, n_groups=n_groups),
        grid=(N_PAIRS, b),
        in_specs=[
            pl.BlockSpec((1, t, LANES), blk),
            pl.BlockSpec((1, t, LANES), blk),
            pl.BlockSpec((1, n_groups, LANES, NAT_CHUNK), lambda h, i: (i, 0, h, 0)),
            pl.BlockSpec((1, lc, LANES), blk),
            pl.BlockSpec((1, 1, LANES, lc), lambda h, i: (i, 0, h, 0)),
            pl.BlockSpec((1,) + tab.shape[1:], lambda h, i: (h, 0, 0, 0, 0)),
        ],
        out_specs=pl.BlockSpec((1, t, LANES), blk),
        out_shape=jax.ShapeDtypeStruct((b, t, d), BF16),
        scratch_shapes=[pltpu.VMEM((2 * NAT_CHUNK // LANES, nk, LANES), F32)] * NAT_BUFS
        + [pltpu.VMEM((2 * NAT_CHUNK // LANES, nk, LANES), BF16)] * NAT_BUFS,
        compiler_params=_cparams(("parallel", "parallel")),
        name="natten_b",
    )(q, k, vt, kc, vct, tab)


def _outproj_final_kernel(o_ref, gate_ref, x_ref, gx_ref, w_ref, fg_ref, out_ref):
    u = (o_ref[0].astype(F32) * gate_ref[0].astype(F32)).astype(BF16)
    y = jnp.dot(u, w_ref[...], preferred_element_type=F32)
    x2 = x_ref[0] + gx_ref[0, 0] * y
    ms = jnp.mean(x2 * x2, axis=-1, keepdims=True)
    out_ref[0] = x2 * lax.rsqrt(ms + NORM_EPS) * fg_ref[...]


def _outproj_final(o, gate, x, mods, w_out, fg, tm):
    b, t, d = x.shape
    row = lambda i, j: (i, j, 0)
    return pl.pallas_call(
        _outproj_final_kernel,
        grid=(b, t // tm),
        in_specs=[
            pl.BlockSpec((1, tm, d), row),
            pl.BlockSpec((1, tm, d), row),
            pl.BlockSpec((1, tm, d), row),
            pl.BlockSpec((1, 1, 1, d), lambda i, j: (i, 2, 0, 0)),
            pl.BlockSpec((d, d), lambda i, j: (0, 0)),
            pl.BlockSpec((1, d), lambda i, j: (0, 0)),
        ],
        out_specs=pl.BlockSpec((1, tm, d), row),
        out_shape=jax.ShapeDtypeStruct((b, t, d), F32),
        compiler_params=_cparams(("parallel", "parallel")),
        name="outproj_final",
    )(o, gate, x, mods, w_out, fg)


def _rope_tables(t_len):
    pos = jnp.arange(t_len, dtype=jnp.int32)
    row = (pos // GRID_W).astype(F32)
    col = (pos % GRID_W).astype(F32)
    inv = ROPE_THETA ** (-jnp.arange(0, ROPE_AXIS_DIM, 2, dtype=F32) / ROPE_AXIS_DIM)
    ang_r = row[:, None] * inv
    ang_c = col[:, None] * inv
    zero = jnp.zeros_like(ang_r)
    cos_h = jnp.concatenate([jnp.cos(ang_r), jnp.cos(ang_r), jnp.cos(ang_c), jnp.cos(ang_c)], axis=1)
    sup_h = jnp.concatenate([-jnp.sin(ang_r), zero, -jnp.sin(ang_c), zero], axis=1)
    sdn_h = jnp.concatenate([zero, jnp.sin(ang_r), zero, jnp.sin(ang_c)], axis=1)
    two = lambda a: jnp.concatenate([a, a], axis=1)
    return two(cos_h), two(sup_h), two(sdn_h)


def _nat_table_kernel(t2_ref, o_ref, *, plans):
    low = lax.broadcasted_iota(jnp.int32, (GRID_W, LANES), 1) < GRID_W
    neg = jnp.full((GRID_W, LANES), NEG_BIG, F32)
    half_rows = NAT_ROWS // 2
    for kind, plan in enumerate(plans):
        for head in range(2):
            for i2 in range(half_rows):
                for s in range(NAT_SPAN * NAT_ROWS):
                    d0, d1 = plan[s * NAT_ROWS + 2 * i2], plan[s * NAT_ROWS + 2 * i2 + 1]
                    b0 = neg if d0 is None else t2_ref[0, head, d0]
                    b1 = neg if d1 is None else t2_ref[0, head, d1]
                    o_ref[0, kind, head * half_rows + i2, s * GRID_W:(s + 1) * GRID_W, :] = jnp.where(low, b0, b1)


def _natten_tables(rpb, rows):
    h = rpb.shape[0]
    qcol = np.arange(GRID_W)
    c0 = np.clip(qcol - WIN_C // 2, 0, GRID_W - WIN_C)
    kcol = np.arange(GRID_W)
    valid = (kcol[:, None] >= c0[None, :]) & (kcol[:, None] < c0[None, :] + WIN_C)
    pad = GRID_W - WIN_C
    flipped = jnp.pad(rpb.astype(F32) * LOG2E, ((0, 0), (0, 0), (pad, pad)))[:, :, ::-1]
    rows2 = [jnp.concatenate([flipped[:, :, GRID_W - 1 - k:2 * GRID_W - 1 - k]] * 2, axis=2) for k in range(GRID_W)]
    t2 = jnp.where(jnp.asarray(np.concatenate([valid, valid], axis=1))[None, None], jnp.stack(rows2, axis=2), NEG_BIG)
    t2 = t2.reshape(h // 2, 2, 2 * WIN_R - 1, GRID_W, LANES)
    n_groups = rows // NAT_ROWS
    span_rows = NAT_SPAN * NAT_ROWS

    def plan(g):
        ks = int(np.clip(g - 1, 0, n_groups - NAT_SPAN)) * NAT_ROWS
        out = []
        for s in range(span_rows):
            for i in range(NAT_ROWS):
                rq, rk = NAT_ROWS * g + i, ks + s
                r0 = int(np.clip(rq - WIN_R // 2, 0, rows - WIN_R))
                out.append(rk - rq + WIN_R - 1 if r0 <= rk < r0 + WIN_R else None)
        return tuple(out)

    plans = (plan(0), plan(1), plan(n_groups - 1))
    assert all(plan(g) == plans[1] for g in range(1, n_groups - 1))
    n_col = 2 * NAT_ROWS * GRID_W // LANES
    return pl.pallas_call(
        functools.partial(_nat_table_kernel, plans=plans),
        grid=(h // 2,),
        in_specs=[pl.BlockSpec((1,) + t2.shape[1:], lambda p: (p, 0, 0, 0, 0))],
        out_specs=pl.BlockSpec((1, 3, n_col, span_rows * GRID_W, LANES), lambda p: (p, 0, 0, 0, 0)),
        out_shape=jax.ShapeDtypeStruct((h // 2, 3, n_col, span_rows * GRID_W, LANES), F32),
        compiler_params=_cparams(("parallel",)),
        name="natten_table",
    )(t2)


def kernel(x, c, ctx, c_ctx, norm_g, w_mod, b_mod, a_w_in, a_q_norm_g, a_k_norm_g, a_w_out,
           b_w_in, b_rpb, b_w_out, final_norm_g):
    bsz, t, d = x.shape
    lc = ctx.shape[1]

    n_rows = ((bsz + 1 + 7) // 8) * 8
    c_rows = jnp.zeros((n_rows, d), F32).at[:bsz].set(c).at[bsz].set(c_ctx)
    mods = _modulation(c_rows, w_mod, b_mod).reshape(w_mod.shape[0], n_rows, 3, 1, d)

    cos_t, sup, sdn = _rope_tables(t)
    one_t = jnp.ones((lc, LANES), F32)
    zero_t = jnp.zeros((lc, LANES), F32)
    head_block = np.kron(np.eye(LANES // HEAD_DIM), np.ones((HEAD_DIM, HEAD_DIM)))
    bd = jnp.asarray(head_block, BF16)
    two = lambda g: jnp.concatenate([g, g]).reshape(1, LANES).astype(F32)

    mx, mc = mods[0, :bsz], mods[0, bsz:bsz + 1]
    ng = norm_g[0].reshape(1, d)
    w_in = a_w_in[0].astype(BF16)
    qg, kg = two(a_q_norm_g[0]) * (ATTN_SCALE * LOG2E), two(a_k_norm_g[0])
    q, k, vt, gate = _inproj_a(x, ng, mx, w_in, qg, kg, bd, cos_t, sup, sdn, tm=1024, tk=512)
    qc, kc, vct, gate_c = _inproj_a(ctx, ng, mc, w_in, qg, kg, bd, one_t, zero_t, zero_t, tm=lc, tk=lc)
    w_out = a_w_out[0].astype(BF16)
    x1 = _attn_a(q, k, vt, kc, vct, gate, x, mx, w_out, tq=512)
    ctx1 = _attn_a(qc, None, None, kc, vct, gate_c, ctx, mc, w_out, tq=lc)

    mx, mc = mods[1, :bsz], mods[1, bsz:bsz + 1]
    ng = norm_g[1].reshape(1, d)
    w_in = b_w_in[0].astype(BF16)
    q, k, vt, gate = _inproj_b(x1, ng, mx, w_in, tm=512, kv_only=False)
    kc, vct = _inproj_b(ctx1, ng, mc, w_in[:, B_WIDTH:3 * B_WIDTH], tm=lc, kv_only=True)
    o = _natten(q, k, vt, kc, vct, _natten_tables(b_rpb[0], t // GRID_W))
    return _outproj_final(o, gate, x1, mx, b_w_out[0].astype(BF16), final_norm_g.reshape(1, d), tm=512)
```

```python
import functools

import jax
import jax.numpy as jnp
import numpy as np
from jax import lax
from jax.experimental import pallas as pl
from jax.experimental.pallas import tpu as pltpu

F32 = jnp.float32
BF16 = jnp.bfloat16

LANES = 128
VMEM_LIMIT = 56 * 1024 * 1024

D_MODEL = 1024
GRID_W = 64
HEAD_DIM = 64
NORM_EPS = 1e-6
ATTN_SCALE = HEAD_DIM ** -0.5
A_HEADS = 16
A_KV_HEADS = 4
A_WIDTH = A_HEADS * HEAD_DIM
A_KV_WIDTH = A_KV_HEADS * HEAD_DIM
A_GROUP = A_HEADS // A_KV_HEADS
ROPE_THETA = 10000.0
ROPE_AXIS_DIM = HEAD_DIM // 2
ROPE_HALF = ROPE_AXIS_DIM // 2
B_HEADS = 16
B_WIDTH = B_HEADS * HEAD_DIM
WIN_R = 8
WIN_C = 16
N_PAIRS = D_MODEL // LANES
NEG_BIG = -1e30
MAX_ROWS = 128
EXP_ROWS = 64
SUM_ROWS = 16
NAT_ROWS = 4
NAT_CHUNK = NAT_ROWS * GRID_W
NAT_SPAN = 3
NAT_KEYS = NAT_SPAN * NAT_CHUNK
NAT_BUFS = 3
N_SCORE_BUFS = 3
LOG2E = 1.4426950408889634


def _cparams(sem, flags=None):
    return pltpu.CompilerParams(dimension_semantics=sem, vmem_limit_bytes=VMEM_LIMIT, flags=flags)


def _mod_kernel(c_ref, w_ref, b_ref, o_ref):
    c = c_ref[...]
    s = c * jax.nn.sigmoid(c)
    o_ref[0] = jnp.dot(s, w_ref[0], precision=lax.Precision.HIGHEST,
                       preferred_element_type=F32) + b_ref[0]


def _modulation(c_rows, w_mod, b_mod):
    depth, d, n = w_mod.shape
    rows = c_rows.shape[0]
    tn = 1024
    return pl.pallas_call(
        _mod_kernel,
        grid=(depth, n // tn),
        in_specs=[
            pl.BlockSpec((rows, d), lambda l, j: (0, 0)),
            pl.BlockSpec((1, d, tn), lambda l, j: (l, 0, j)),
            pl.BlockSpec((1, 1, tn), lambda l, j: (l, 0, j)),
        ],
        out_specs=pl.BlockSpec((1, rows, tn), lambda l, j: (l, 0, j)),
        out_shape=jax.ShapeDtypeStruct((depth, rows, n), F32),
        compiler_params=_cparams(("arbitrary", "arbitrary")),
        name="adaln_mod",
    )(c_rows, w_mod, b_mod.reshape(depth, 1, n))


def _adaln(x, ng, sc, sh):
    ms = jnp.mean(x * x, axis=-1, keepdims=True)
    y = x * lax.rsqrt(ms + NORM_EPS) * ng
    return y * (1.0 + sc) + sh


def _silu(z):
    return z * jax.nn.sigmoid(z)


def _head_norm_rope(blk, gain, bd, cos_t, sin_up, sin_dn):
    sq = blk * blk
    hi = sq.astype(BF16)
    lo = (sq - hi.astype(F32)).astype(BF16)
    ssum = (jnp.dot(hi, bd, preferred_element_type=F32) + jnp.dot(lo, bd, preferred_element_type=F32))
    n = blk * lax.rsqrt(ssum * (1.0 / HEAD_DIM) + NORM_EPS) * gain
    up = pltpu.roll(n, LANES - ROPE_HALF, 1)
    dn = pltpu.roll(n, ROPE_HALF, 1)
    return n * cos_t + up * sin_up + dn * sin_dn


def _inproj_a_kernel(x_ref, ng_ref, sc_ref, sh_ref, w_ref, qg_ref, kg_ref, bd_ref, cos_ref, sup_ref, sdn_ref,
                     q_ref, k_ref, v_ref, g_ref):
    h = _adaln(x_ref[0], ng_ref[...], sc_ref[0, 0], sh_ref[0, 0])
    p = jnp.dot(h.astype(BF16), w_ref[...], preferred_element_type=F32)
    bd = bd_ref[...]
    cos_t, sup, sdn = cos_ref[...], sup_ref[...], sdn_ref[...]
    qg, kg = qg_ref[...], kg_ref[...]
    for j in range(A_WIDTH // LANES):
        blk = p[:, j * LANES:(j + 1) * LANES]
        q_ref[0, :, j * LANES:(j + 1) * LANES] = _head_norm_rope(blk, qg, bd, cos_t, sup, sdn).astype(BF16)
    for j in range(A_KV_WIDTH // LANES):
        blk = p[:, A_WIDTH + j * LANES:A_WIDTH + (j + 1) * LANES]
        k_ref[0, :, j * LANES:(j + 1) * LANES] = _head_norm_rope(blk, kg, bd, cos_t, sup, sdn).astype(BF16)
    v = p[:, A_WIDTH + A_KV_WIDTH:A_WIDTH + 2 * A_KV_WIDTH]
    tk = v_ref.shape[3]
    for c in range(v.shape[0] // tk):
        v_ref[0, c] = v[c * tk:(c + 1) * tk].T.astype(BF16)
    g_ref[0] = _silu(p[:, A_WIDTH + 2 * A_KV_WIDTH:]).astype(BF16)


def _inproj_a(x, ng, mods, w, qg, kg, bd, cos_t, sup, sdn, tm, tk):
    b, r, d = x.shape
    n = w.shape[1]
    shared = mods.shape[0] == 1
    mod_idx = (lambda i, j, c: (0, c, 0, 0)) if shared else (lambda i, j, c: (i, c, 0, 0))
    row = lambda i, j: (i, j, 0)
    const2 = lambda i, j: (0, 0)
    return pl.pallas_call(
        _inproj_a_kernel,
        grid=(b, r // tm),
        in_specs=[
            pl.BlockSpec((1, tm, d), row),
            pl.BlockSpec((1, d), const2),
            pl.BlockSpec((1, 1, 1, d), lambda i, j: mod_idx(i, j, 1)),
            pl.BlockSpec((1, 1, 1, d), lambda i, j: mod_idx(i, j, 0)),
            pl.BlockSpec((d, n), const2),
            pl.BlockSpec((1, LANES), const2),
            pl.BlockSpec((1, LANES), const2),
            pl.BlockSpec((LANES, LANES), const2),
            pl.BlockSpec((tm, LANES), lambda i, j: (j, 0)),
            pl.BlockSpec((tm, LANES), lambda i, j: (j, 0)),
            pl.BlockSpec((tm, LANES), lambda i, j: (j, 0)),
        ],
        out_specs=[
            pl.BlockSpec((1, tm, A_WIDTH), row),
            pl.BlockSpec((1, tm, A_KV_WIDTH), row),
            pl.BlockSpec((1, tm // tk, A_KV_WIDTH, tk), lambda i, j: (i, j, 0, 0)),
            pl.BlockSpec((1, tm, A_WIDTH), row),
        ],
        out_shape=[
            jax.ShapeDtypeStruct((b, r, A_WIDTH), BF16),
            jax.ShapeDtypeStruct((b, r, A_KV_WIDTH), BF16),
            jax.ShapeDtypeStruct((b, r // tk, A_KV_WIDTH, tk), BF16),
            jax.ShapeDtypeStruct((b, r, A_WIDTH), BF16),
        ],
        compiler_params=_cparams(("parallel", "parallel")),
        name="inproj_a",
    )(x, ng, mods, mods, w, qg, kg, bd, cos_t, sup, sdn)


def _stack_qt(q4, half):
    qt = q4.astype(F32).T
    zero = jnp.zeros((HEAD_DIM, qt.shape[1]), F32)
    first = half == 0

    def place(h_t):
        return jnp.where(first, jnp.concatenate([h_t, zero], axis=0), jnp.concatenate([zero, h_t], axis=0))

    heads = [place(qt[h * HEAD_DIM:(h + 1) * HEAD_DIM]) for h in range(A_GROUP)]
    return jnp.concatenate(heads, axis=1).astype(BF16)


def _store_cols(scr, row_start, val):
    n = val.shape[0]
    for j in range(scr.shape[0]):
        scr[j, row_start:row_start + n, :] = val[:, j * LANES:(j + 1) * LANES]


def _load_cols(scr, nk):
    return jnp.concatenate([scr[j, 0:nk, :] for j in range(scr.shape[0])], axis=1)


def _softmax_cols(s_scr, p_scr, nk, m):
    m_out, a_out = [], []
    for j in range(s_scr.shape[0]):
        sl = slice(j * LANES, (j + 1) * LANES)
        mx = s_scr[j, 0:MAX_ROWS, :]
        for r in range(1, nk // MAX_ROWS):
            mx = jnp.maximum(mx, s_scr[j, r * MAX_ROWS:(r + 1) * MAX_ROWS, :])
        m_new = jnp.maximum(m[:, sl], jnp.max(mx, axis=0, keepdims=True))
        for r in range(nk // EXP_ROWS):
            rows = slice(r * EXP_ROWS, (r + 1) * EXP_ROWS)
            p_scr[j, rows, :] = jnp.exp2(s_scr[j, rows, :] - m_new).astype(BF16)
        m_out.append(m_new)
        a_out.append(jnp.exp2(m[:, sl] - m_new))
    cat = lambda xs: jnp.concatenate(xs, axis=1)
    return cat(m_out), cat(a_out)


def _attn_a_kernel(*refs, tq, tk, n_x_chunks):
    n_in = 10 if n_x_chunks else 8
    if n_x_chunks:
        q_ref, kx_ref, vxt_ref, kc_ref, vct_ref, gate_ref, x_ref, gx_ref, w_ref, out_ref = refs[:n_in]
    else:
        q_ref, kc_ref, vct_ref, gate_ref, x_ref, gx_ref, w_ref, out_ref = refs[:n_in]
    o_scr = refs[n_in]
    s_bufs = refs[n_in + 1:n_in + 1 + N_SCORE_BUFS]
    p_bufs = refs[n_in + 1 + N_SCORE_BUFS:]
    kv_head = pl.program_id(2)
    vrow = pl.multiple_of(kv_head * HEAD_DIM, HEAD_DIM)
    a_t = _stack_qt(q_ref[0], kv_head % 2)
    cols = A_GROUP * tq
    lc = kc_ref.shape[1]

    n_chunks = 1 + n_x_chunks
    chunk_rows = lambda i: lc if i == 0 else tk
    k_blk = lambda i: kc_ref[0] if i == 0 else kx_ref[0, (i - 1) * tk:i * tk, :]
    v_blk = lambda i: (vct_ref[0, 0, pl.ds(vrow, HEAD_DIM), :] if i == 0
                       else vxt_ref[0, i - 1, pl.ds(vrow, HEAD_DIM), :])

    def scores(i):
        _store_cols(s_bufs[i % N_SCORE_BUFS], 0, jnp.dot(k_blk(i), a_t, preferred_element_type=F32))

    def consume(i, carry):
        m, acc = carry
        nk = chunk_rows(i)
        p_scr = p_bufs[i % len(p_bufs)]
        m, alpha = _softmax_cols(s_bufs[i % N_SCORE_BUFS], p_scr, nk, m)
        v_ext = jnp.concatenate([v_blk(i), jnp.ones((SUM_ROWS, nk), BF16)], axis=0)
        acc = alpha * acc + jnp.dot(v_ext, _load_cols(p_scr, nk), preferred_element_type=F32)
        return m, acc

    carry = (jnp.full((1, cols), NEG_BIG, F32), jnp.zeros((HEAD_DIM + SUM_ROWS, cols), F32))
    scores(0)
    for i in range(n_chunks):
        if i + 1 < n_chunks:
            scores(i + 1)
        carry = consume(i, carry)
    _, acc = carry
    o_t = acc[:HEAD_DIM] / acc[HEAD_DIM:HEAD_DIM + 1]
    o_scr[kv_head] = jnp.concatenate([o_t[:, h * tq:(h + 1) * tq] for h in range(A_GROUP)], axis=0).T

    @pl.when(kv_head == A_KV_HEADS - 1)
    def _():
        o_full = jnp.concatenate([o_scr[j] for j in range(A_KV_HEADS)], axis=1)
        u = (o_full * gate_ref[0].astype(F32)).astype(BF16)
        y = jnp.dot(u, w_ref[...], preferred_element_type=F32)
        out_ref[0] = x_ref[0] + gx_ref[0, 0] * y


def _attn_a(q, kx, vxt, kc, vct, gate, x, mods, w_out, tq):
    b, r, d = x.shape
    has_x = kx is not None
    shared = mods.shape[0] == 1
    qrow = lambda i, j, h: (i, j, 0)
    kvp = lambda i, j, h: (i, 0, h // 2)
    whole = lambda i, j, h: (i, 0, 0, 0)
    in_specs = [pl.BlockSpec((1, tq, A_GROUP * HEAD_DIM), lambda i, j, h: (i, j, h))]
    args = [q]
    lc = kc.shape[1]
    tk = lc
    n_x_chunks = 0
    if has_x:
        t = kx.shape[1]
        n_x_chunks, tk = vxt.shape[1], vxt.shape[3]
        in_specs += [pl.BlockSpec((1, t, LANES), kvp), pl.BlockSpec((1,) + vxt.shape[1:], whole)]
        args += [kx, vxt]
    in_specs += [
        pl.BlockSpec((1, lc, LANES), kvp),
        pl.BlockSpec((1,) + vct.shape[1:], whole),
        pl.BlockSpec((1, tq, d), qrow),
        pl.BlockSpec((1, tq, d), qrow),
        pl.BlockSpec((1, 1, 1, d), (lambda i, j, h: (0, 2, 0, 0)) if shared else (lambda i, j, h: (i, 2, 0, 0))),
        pl.BlockSpec((d, d), lambda i, j, h: (0, 0)),
    ]
    args += [kc, vct, gate, x, mods, w_out]
    return pl.pallas_call(
        functools.partial(_attn_a_kernel, tq=tq, tk=tk, n_x_chunks=n_x_chunks),
        grid=(b, r // tq, A_KV_HEADS),
        in_specs=in_specs,
        out_specs=pl.BlockSpec((1, tq, d), qrow),
        out_shape=jax.ShapeDtypeStruct((b, r, d), F32),
        scratch_shapes=[
            pltpu.VMEM((A_KV_HEADS, tq, A_GROUP * HEAD_DIM), F32),
            *[pltpu.VMEM((A_GROUP * tq // LANES, max(tk, lc), LANES), F32)] * N_SCORE_BUFS,
            *[pltpu.VMEM((A_GROUP * tq // LANES, max(tk, lc), LANES), BF16)] * 2,
        ],
        compiler_params=_cparams(("parallel", "parallel", "arbitrary")),
        name="attn_a_x" if has_x else "attn_a_ctx",
    )(*args)


def _store_vt_chunks(v_ref, v):
    for c in range(v.shape[0] // NAT_CHUNK):
        v_ref[0, c] = v[c * NAT_CHUNK:(c + 1) * NAT_CHUNK].T.astype(BF16)


def _inproj_b_kernel(x_ref, ng_ref, sc_ref, sh_ref, w_ref, *out_refs, kv_only):
    h = _adaln(x_ref[0], ng_ref[...], sc_ref[0, 0], sh_ref[0, 0])
    p = jnp.dot(h.astype(BF16), w_ref[...], preferred_element_type=F32)
    if kv_only:
        k_ref, v_ref = out_refs
        k_ref[0] = p[:, :B_WIDTH].astype(BF16)
        _store_vt_chunks(v_ref, p[:, B_WIDTH:])
    else:
        q_ref, k_ref, v_ref, g_ref = out_refs
        q_ref[0] = (p[:, :B_WIDTH] * (ATTN_SCALE * LOG2E)).astype(BF16)
        k_ref[0] = p[:, B_WIDTH:2 * B_WIDTH].astype(BF16)
        _store_vt_chunks(v_ref, p[:, 2 * B_WIDTH:3 * B_WIDTH])
        g_ref[0] = _silu(p[:, 3 * B_WIDTH:]).astype(BF16)


def _inproj_b(x, ng, mods, w, tm, kv_only):
    b, r, d = x.shape
    n = w.shape[1]
    shared = mods.shape[0] == 1
    mod_idx = (lambda i, c: (0, c, 0, 0)) if shared else (lambda i, c: (i, c, 0, 0))
    row = lambda i, j: (i, j, 0)
    row_spec = pl.BlockSpec((1, tm, B_WIDTH), row)
    row_shape = jax.ShapeDtypeStruct((b, r, B_WIDTH), BF16)
    vt_spec = pl.BlockSpec((1, tm // NAT_CHUNK, B_WIDTH, NAT_CHUNK), lambda i, j: (i, j, 0, 0))
    vt_shape = jax.ShapeDtypeStruct((b, r // NAT_CHUNK, B_WIDTH, NAT_CHUNK), BF16)
    if kv_only:
        out_specs, out_shape = [row_spec, vt_spec], [row_shape, vt_shape]
    else:
        out_specs, out_shape = [row_spec, row_spec, vt_spec, row_spec], [row_shape, row_shape, vt_shape, row_shape]
    return pl.pallas_call(
        functools.partial(_inproj_b_kernel, kv_only=kv_only),
        grid=(b, r // tm),
        in_specs=[
            pl.BlockSpec((1, tm, d), row),
            pl.BlockSpec((1, d), lambda i, j: (0, 0)),
            pl.BlockSpec((1, 1, 1, d), lambda i, j: mod_idx(i, 1)),
            pl.BlockSpec((1, 1, 1, d), lambda i, j: mod_idx(i, 0)),
            pl.BlockSpec((d, n), lambda i, j: (0, 0)),
        ],
        out_specs=out_specs,
        out_shape=out_shape,
        compiler_params=_cparams(("parallel", "parallel")),
        name="inproj_b_ctx" if kv_only else "inproj_b_x",
    )(x, ng, mods, mods, w)


def _nat_scores(g, n_groups, q_ref, k_ref, kc_ref, tab_ref, s_scr):
    c0 = min(max(g - 1, 0), n_groups - NAT_SPAN)
    kind = int(g > 0) + int(g == n_groups - 1)
    qs, ks = g * NAT_CHUNK, c0 * NAT_CHUNK
    qt = q_ref[0, qs:qs + NAT_CHUNK, :].astype(F32).T
    top = lax.broadcasted_iota(jnp.int32, qt.shape, 0) < HEAD_DIM
    a_t = jnp.concatenate([jnp.where(top, qt, 0.0), jnp.where(top, 0.0, qt)], axis=1).astype(BF16)
    s_span = jnp.dot(k_ref[0, ks:ks + NAT_KEYS, :], a_t, preferred_element_type=F32)
    for j in range(s_scr.shape[0]):
        s_scr[j, 0:NAT_KEYS, :] = s_span[:, j * LANES:(j + 1) * LANES] + tab_ref[0, kind, j]
    _store_cols(s_scr, NAT_KEYS, jnp.dot(kc_ref[0], a_t, preferred_element_type=F32))


def _nat_softmax(s_scr, p_scr):
    nk = s_scr.shape[1]
    for j in range(s_scr.shape[0]):
        mx = s_scr[j, 0:MAX_ROWS, :]
        for r in range(1, nk // MAX_ROWS):
            mx = jnp.maximum(mx, s_scr[j, r * MAX_ROWS:(r + 1) * MAX_ROWS, :])
        m = jnp.max(mx, axis=0, keepdims=True)
        for r in range(nk // EXP_ROWS):
            rows = slice(r * EXP_ROWS, (r + 1) * EXP_ROWS)
            p_scr[j, rows, :] = jnp.exp2(s_scr[j, rows, :] - m).astype(BF16)


def _nat_output(g, n_groups, vt_ref, vct_ref, p_scr, o_ref):
    nk = p_scr.shape[1]
    c0 = min(max(g - 1, 0), n_groups - NAT_SPAN)
    qs = g * NAT_CHUNK
    v_t = jnp.concatenate([vt_ref[0, c0 + c] for c in range(NAT_SPAN)] + [vct_ref[0, 0]], axis=1)
    v_ext = jnp.concatenate([v_t, jnp.ones((SUM_ROWS, nk), BF16)], axis=0)
    acc = jnp.dot(v_ext, _load_cols(p_scr, nk), preferred_element_type=F32)
    o_t = acc[:LANES] / acc[LANES:LANES + 1]
    both = jnp.concatenate([o_t[:HEAD_DIM, :NAT_CHUNK], o_t[HEAD_DIM:, NAT_CHUNK:]], axis=0)
    o_ref[0, qs:qs + NAT_CHUNK, :] = both.T.astype(BF16)


def _natten_kernel(q_ref, k_ref, vt_ref, kc_ref, vct_ref, tab_ref, o_ref, *scratch, n_groups):
    n_buf = len(scratch) // 2
    s_of = lambda g: scratch[g % n_buf]
    p_of = lambda g: scratch[n_buf + g % n_buf]
    _nat_scores(0, n_groups, q_ref, k_ref, kc_ref, tab_ref, s_of(0))
    _nat_scores(1, n_groups, q_ref, k_ref, kc_ref, tab_ref, s_of(1))
    for g in range(n_groups):
        if g + 2 < n_groups:
            _nat_scores(g + 2, n_groups, q_ref, k_ref, kc_ref, tab_ref, s_of(g + 2))
        _nat_softmax(s_of(g), p_of(g))
        _nat_output(g, n_groups, vt_ref, vct_ref, p_of(g), o_ref)


def _natten(q, k, vt, kc, vct, tab):
    b, t, d = q.shape
    lc = kc.shape[1]
    n_groups = t // NAT_CHUNK
    nk = NAT_KEYS + lc
    blk = lambda h, i: (i, 0, h)
    return pl.pallas_call(
        functools.partial(_natten_kernel, n_groups=n_groups),
        grid=(N_PAIRS, b),
        in_specs=[
            pl.BlockSpec((1, t, LANES), blk),
            pl.BlockSpec((1, t, LANES), blk),
            pl.BlockSpec((1, n_groups, LANES, NAT_CHUNK), lambda h, i: (i, 0, h, 0)),
            pl.BlockSpec((1, lc, LANES), blk),
            pl.BlockSpec((1, 1, LANES, lc), lambda h, i: (i, 0, h, 0)),
            pl.BlockSpec((1,) + tab.shape[1:], lambda h, i: (h, 0, 0, 0, 0)),
        ],
        out_specs=pl.BlockSpec((1, t, LANES), blk),
        out_shape=jax.ShapeDtypeStruct((b, t, d), BF16),
        scratch_shapes=[pltpu.VMEM((2 * NAT_CHUNK // LANES, nk, LANES), F32)] * NAT_BUFS
        + [pltpu.VMEM((2 * NAT_CHUNK // LANES, nk, LANES), BF16)] * NAT_BUFS,
        compiler_params=_cparams(("parallel", "parallel")),
        name="natten_b",
    )(q, k, vt, kc, vct, tab)


def _outproj_final_kernel(o_ref, gate_ref, x_ref, gx_ref, w_ref, fg_ref, out_ref):
    u = (o_ref[0].astype(F32) * gate_ref[0].astype(F32)).astype(BF16)
    y = jnp.dot(u, w_ref[...], preferred_element_type=F32)
    x2 = x_ref[0] + gx_ref[0, 0] * y
    ms = jnp.mean(x2 * x2, axis=-1, keepdims=True)
    out_ref[0] = x2 * lax.rsqrt(ms + NORM_EPS) * fg_ref[...]


def _outproj_final(o, gate, x, mods, w_out, fg, tm):
    b, t, d = x.shape
    row = lambda i, j: (i, j, 0)
    return pl.pallas_call(
        _outproj_final_kernel,
        grid=(b, t // tm),
        in_specs=[
            pl.BlockSpec((1, tm, d), row),
            pl.BlockSpec((1, tm, d), row),
            pl.BlockSpec((1, tm, d), row),
            pl.BlockSpec((1, 1, 1, d), lambda i, j: (i, 2, 0, 0)),
            pl.BlockSpec((d, d), lambda i, j: (0, 0)),
            pl.BlockSpec((1, d), lambda i, j: (0, 0)),
        ],
        out_specs=pl.BlockSpec((1, tm, d), row),
        out_shape=jax.ShapeDtypeStruct((b, t, d), F32),
        compiler_params=_cparams(("parallel", "parallel")),
        name="outproj_final",
    )(o, gate, x, mods, w_out, fg)


def _rope_tables(t_len):
    pos = np.arange(t_len)
    inv = ROPE_THETA ** (-np.arange(0, ROPE_AXIS_DIM, 2, dtype=np.float64) / ROPE_AXIS_DIM)
    ang_r = (pos // GRID_W)[:, None] * inv
    ang_c = (pos % GRID_W)[:, None] * inv
    zero = np.zeros_like(ang_r)
    cos_h = np.concatenate([np.cos(ang_r), np.cos(ang_r), np.cos(ang_c), np.cos(ang_c)], axis=1)
    sup_h = np.concatenate([-np.sin(ang_r), zero, -np.sin(ang_c), zero], axis=1)
    sdn_h = np.concatenate([zero, np.sin(ang_r), zero, np.sin(ang_c)], axis=1)
    two = lambda a: jnp.asarray(np.concatenate([a, a], axis=1), F32)
    return two(cos_h), two(sup_h), two(sdn_h)


def _nat_table_kernel(t2_ref, o_ref, *, plans):
    low = lax.broadcasted_iota(jnp.int32, (GRID_W, LANES), 1) < GRID_W
    neg = jnp.full((GRID_W, LANES), NEG_BIG, F32)
    half_rows = NAT_ROWS // 2
    for kind, plan in enumerate(plans):
        for head in range(2):
            for i2 in range(half_rows):
                for s in range(NAT_SPAN * NAT_ROWS):
                    d0, d1 = plan[s * NAT_ROWS + 2 * i2], plan[s * NAT_ROWS + 2 * i2 + 1]
                    b0 = neg if d0 is None else t2_ref[0, head, d0]
                    b1 = neg if d1 is None else t2_ref[0, head, d1]
                    o_ref[0, kind, head * half_rows + i2, s * GRID_W:(s + 1) * GRID_W, :] = jnp.where(low, b0, b1)


def _natten_tables(rpb, rows):
    h = rpb.shape[0]
    qcol = np.arange(GRID_W)
    c0 = np.clip(qcol - WIN_C // 2, 0, GRID_W - WIN_C)
    kcol = np.arange(GRID_W)
    valid = (kcol[:, None] >= c0[None, :]) & (kcol[:, None] < c0[None, :] + WIN_C)
    pad = GRID_W - WIN_C
    flipped = jnp.pad(rpb.astype(F32) * LOG2E, ((0, 0), (0, 0), (pad, pad)))[:, :, ::-1]
    rows2 = [jnp.concatenate([flipped[:, :, GRID_W - 1 - k:2 * GRID_W - 1 - k]] * 2, axis=2) for k in range(GRID_W)]
    t2 = jnp.where(jnp.asarray(np.concatenate([valid, valid], axis=1))[None, None], jnp.stack(rows2, axis=2), NEG_BIG)
    t2 = t2.reshape(h // 2, 2, 2 * WIN_R - 1, GRID_W, LANES)
    n_groups = rows // NAT_ROWS
    span_rows = NAT_SPAN * NAT_ROWS

    def plan(g):
        ks = int(np.clip(g - 1, 0, n_groups - NAT_SPAN)) * NAT_ROWS
        out = []
        for s in range(span_rows):
            for i in range(NAT_ROWS):
                rq, rk = NAT_ROWS * g + i, ks + s
                r0 = int(np.clip(rq - WIN_R // 2, 0, rows - WIN_R))
                out.append(rk - rq + WIN_R - 1 if r0 <= rk < r0 + WIN_R else None)
        return tuple(out)

    plans = (plan(0), plan(1), plan(n_groups - 1))
    assert all(plan(g) == plans[1] for g in range(1, n_groups - 1))
    n_col = 2 * NAT_ROWS * GRID_W // LANES
    return pl.pallas_call(
        functools.partial(_nat_table_kernel, plans=plans),
        grid=(h // 2,),
        in_specs=[pl.BlockSpec((1,) + t2.shape[1:], lambda p: (p, 0, 0, 0, 0))],
        out_specs=pl.BlockSpec((1, 3, n_col, span_rows * GRID_W, LANES), lambda p: (p, 0, 0, 0, 0)),
        out_shape=jax.ShapeDtypeStruct((h // 2, 3, n_col, span_rows * GRID_W, LANES), F32),
        compiler_params=_cparams(("parallel",)),
        name="natten_table",
    )(t2)


def kernel(x, c, ctx, c_ctx, norm_g, w_mod, b_mod, a_w_in, a_q_norm_g, a_k_norm_g, a_w_out,
           b_w_in, b_rpb, b_w_out, final_norm_g):
    bsz, t, d = x.shape
    lc = ctx.shape[1]

    n_rows = ((bsz + 1 + 7) // 8) * 8
    c_rows = jnp.zeros((n_rows, d), F32).at[:bsz].set(c).at[bsz].set(c_ctx)
    mods = _modulation(c_rows, w_mod, b_mod).reshape(w_mod.shape[0], n_rows, 3, 1, d)

    cos_t, sup, sdn = _rope_tables(t)
    one_t = jnp.ones((lc, LANES), F32)
    zero_t = jnp.zeros((lc, LANES), F32)
    head_block = np.kron(np.eye(LANES // HEAD_DIM), np.ones((HEAD_DIM, HEAD_DIM)))
    bd = jnp.asarray(head_block, BF16)
    two = lambda g: jnp.concatenate([g, g]).reshape(1, LANES).astype(F32)

    mx, mc = mods[0, :bsz], mods[0, bsz:bsz + 1]
    ng = norm_g[0].reshape(1, d)
    w_in = a_w_in[0].astype(BF16)
    qg, kg = two(a_q_norm_g[0]) * (ATTN_SCALE * LOG2E), two(a_k_norm_g[0])
    q, k, vt, gate = _inproj_a(x, ng, mx, w_in, qg, kg, bd, cos_t, sup, sdn, tm=1024, tk=512)
    qc, kc, vct, gate_c = _inproj_a(ctx, ng, mc, w_in, qg, kg, bd, one_t, zero_t, zero_t, tm=lc, tk=lc)
    w_out = a_w_out[0].astype(BF16)
    x1 = _attn_a(q, k, vt, kc, vct, gate, x, mx, w_out, tq=512)
    ctx1 = _attn_a(qc, None, None, kc, vct, gate_c, ctx, mc, w_out, tq=lc)

    mx, mc = mods[1, :bsz], mods[1, bsz:bsz + 1]
    ng = norm_g[1].reshape(1, d)
    w_in = b_w_in[0].astype(BF16)
    q, k, vt, gate = _inproj_b(x1, ng, mx, w_in, tm=512, kv_only=False)
    kc, vct = _inproj_b(ctx1, ng, mc, w_in[:, B_WIDTH:3 * B_WIDTH], tm=lc, kv_only=True)
    o = _natten(q, k, vt, kc, vct, _natten_tables(b_rpb[0], t // GRID_W))
    return _outproj_final(o, gate, x1, mx, b_w_out[0].astype(BF16), final_norm_g.reshape(1, d), tm=512)
```

```python
import functools

import jax
import jax.numpy as jnp
import numpy as np
from jax import lax
from jax.experimental import pallas as pl
from jax.experimental.pallas import tpu as pltpu

F32 = jnp.float32
BF16 = jnp.bfloat16

LANES = 128
VMEM_LIMIT = 56 * 1024 * 1024

D_MODEL = 1024
GRID_W = 64
HEAD_DIM = 64
NORM_EPS = 1e-6
ATTN_SCALE = HEAD_DIM ** -0.5
A_HEADS = 16
A_KV_HEADS = 4
A_WIDTH = A_HEADS * HEAD_DIM
A_KV_WIDTH = A_KV_HEADS * HEAD_DIM
A_GROUP = A_HEADS // A_KV_HEADS
ROPE_THETA = 10000.0
ROPE_AXIS_DIM = HEAD_DIM // 2
ROPE_HALF = ROPE_AXIS_DIM // 2
B_HEADS = 16
B_WIDTH = B_HEADS * HEAD_DIM
WIN_R = 8
WIN_C = 16
N_PAIRS = D_MODEL // LANES
NEG_BIG = -1e30
MAX_ROWS = 128
EXP_ROWS = 64
SUM_ROWS = 16
NAT_ROWS = 4
NAT_CHUNK = NAT_ROWS * GRID_W
NAT_SPAN = 3
NAT_KEYS = NAT_SPAN * NAT_CHUNK
NAT_BUFS = 3
N_SCORE_BUFS = 3
TM_INPROJ_A = 1024
TK_ATTN_A = 512
TQ_ATTN_A = 512
TM_INPROJ_B = 512
TM_OUTPROJ = 1024
LOG2E = 1.4426950408889634


def _cparams(sem, flags=None):
    return pltpu.CompilerParams(dimension_semantics=sem, vmem_limit_bytes=VMEM_LIMIT, flags=flags)


def _mod_kernel(c_ref, w_ref, b_ref, o_ref):
    c = c_ref[...]
    s = c * jax.nn.sigmoid(c)
    o_ref[0] = jnp.dot(s, w_ref[0], precision=lax.Precision.HIGHEST,
                       preferred_element_type=F32) + b_ref[0]


def _modulation(c_rows, w_mod, b_mod):
    depth, d, n = w_mod.shape
    rows = c_rows.shape[0]
    tn = 1024
    return pl.pallas_call(
        _mod_kernel,
        grid=(depth, n // tn),
        in_specs=[
            pl.BlockSpec((rows, d), lambda l, j: (0, 0)),
            pl.BlockSpec((1, d, tn), lambda l, j: (l, 0, j)),
            pl.BlockSpec((1, 1, tn), lambda l, j: (l, 0, j)),
        ],
        out_specs=pl.BlockSpec((1, rows, tn), lambda l, j: (l, 0, j)),
        out_shape=jax.ShapeDtypeStruct((depth, rows, n), F32),
        compiler_params=_cparams(("arbitrary", "arbitrary")),
        name="adaln_mod",
    )(c_rows, w_mod, b_mod.reshape(depth, 1, n))


def _adaln(x, ng, sc, sh):
    ms = jnp.mean(x * x, axis=-1, keepdims=True)
    y = x * lax.rsqrt(ms + NORM_EPS) * ng
    return y * (1.0 + sc) + sh


def _silu(z):
    return z * jax.nn.sigmoid(z)


def _head_norm_rope(blk, gain, bd, cos_t, sin_up, sin_dn):
    sq = blk * blk
    hi = sq.astype(BF16)
    lo = (sq - hi.astype(F32)).astype(BF16)
    ssum = (jnp.dot(hi, bd, preferred_element_type=F32) + jnp.dot(lo, bd, preferred_element_type=F32))
    n = blk * lax.rsqrt(ssum * (1.0 / HEAD_DIM) + NORM_EPS) * gain
    up = pltpu.roll(n, LANES - ROPE_HALF, 1)
    dn = pltpu.roll(n, ROPE_HALF, 1)
    return n * cos_t + up * sin_up + dn * sin_dn


def _inproj_a_kernel(x_ref, ng_ref, sc_ref, sh_ref, w_ref, qg_ref, kg_ref, bd_ref, cos_ref, sup_ref, sdn_ref,
                     q_ref, k_ref, v_ref, g_ref):
    h = _adaln(x_ref[0], ng_ref[...], sc_ref[0, 0], sh_ref[0, 0])
    p = jnp.dot(h.astype(BF16), w_ref[...], preferred_element_type=F32)
    bd = bd_ref[...]
    cos_t, sup, sdn = cos_ref[...], sup_ref[...], sdn_ref[...]
    qg, kg = qg_ref[...], kg_ref[...]
    for j in range(A_WIDTH // LANES):
        blk = p[:, j * LANES:(j + 1) * LANES]
        q_ref[0, :, j * LANES:(j + 1) * LANES] = _head_norm_rope(blk, qg, bd, cos_t, sup, sdn).astype(BF16)
    for j in range(A_KV_WIDTH // LANES):
        blk = p[:, A_WIDTH + j * LANES:A_WIDTH + (j + 1) * LANES]
        k_ref[0, :, j * LANES:(j + 1) * LANES] = _head_norm_rope(blk, kg, bd, cos_t, sup, sdn).astype(BF16)
    v = p[:, A_WIDTH + A_KV_WIDTH:A_WIDTH + 2 * A_KV_WIDTH]
    tk = v_ref.shape[3]
    for c in range(v.shape[0] // tk):
        v_ref[0, c] = v[c * tk:(c + 1) * tk].T.astype(BF16)
    g_ref[0] = _silu(p[:, A_WIDTH + 2 * A_KV_WIDTH:]).astype(BF16)


def _inproj_a(x, ng, mods, w, qg, kg, bd, cos_t, sup, sdn, tm, tk):
    b, r, d = x.shape
    n = w.shape[1]
    shared = mods.shape[0] == 1
    mod_idx = (lambda i, j, c: (0, c, 0, 0)) if shared else (lambda i, j, c: (i, c, 0, 0))
    row = lambda i, j: (i, j, 0)
    const2 = lambda i, j: (0, 0)
    return pl.pallas_call(
        _inproj_a_kernel,
        grid=(b, r // tm),
        in_specs=[
            pl.BlockSpec((1, tm, d), row),
            pl.BlockSpec((1, d), const2),
            pl.BlockSpec((1, 1, 1, d), lambda i, j: mod_idx(i, j, 1)),
            pl.BlockSpec((1, 1, 1, d), lambda i, j: mod_idx(i, j, 0)),
            pl.BlockSpec((d, n), const2),
            pl.BlockSpec((1, LANES), const2),
            pl.BlockSpec((1, LANES), const2),
            pl.BlockSpec((LANES, LANES), const2),
            pl.BlockSpec((tm, LANES), lambda i, j: (j, 0)),
            pl.BlockSpec((tm, LANES), lambda i, j: (j, 0)),
            pl.BlockSpec((tm, LANES), lambda i, j: (j, 0)),
        ],
        out_specs=[
            pl.BlockSpec((1, tm, A_WIDTH), row),
            pl.BlockSpec((1, tm, A_KV_WIDTH), row),
            pl.BlockSpec((1, tm // tk, A_KV_WIDTH, tk), lambda i, j: (i, j, 0, 0)),
            pl.BlockSpec((1, tm, A_WIDTH), row),
        ],
        out_shape=[
            jax.ShapeDtypeStruct((b, r, A_WIDTH), BF16),
            jax.ShapeDtypeStruct((b, r, A_KV_WIDTH), BF16),
            jax.ShapeDtypeStruct((b, r // tk, A_KV_WIDTH, tk), BF16),
            jax.ShapeDtypeStruct((b, r, A_WIDTH), BF16),
        ],
        compiler_params=_cparams(("parallel", "parallel")),
        name="inproj_a",
    )(x, ng, mods, mods, w, qg, kg, bd, cos_t, sup, sdn)


def _stack_qt(q4, half):
    qt = q4.astype(F32).T
    zero = jnp.zeros((HEAD_DIM, qt.shape[1]), F32)
    first = half == 0

    def place(h_t):
        return jnp.where(first, jnp.concatenate([h_t, zero], axis=0), jnp.concatenate([zero, h_t], axis=0))

    heads = [place(qt[h * HEAD_DIM:(h + 1) * HEAD_DIM]) for h in range(A_GROUP)]
    return jnp.concatenate(heads, axis=1).astype(BF16)


def _store_cols(scr, row_start, val):
    n = val.shape[0]
    for j in range(scr.shape[0]):
        scr[j, row_start:row_start + n, :] = val[:, j * LANES:(j + 1) * LANES]


def _load_cols(scr, nk):
    return jnp.concatenate([scr[j, 0:nk, :] for j in range(scr.shape[0])], axis=1)


def _softmax_cols(s_scr, p_scr, nk, m):
    m_out, a_out = [], []
    for j in range(s_scr.shape[0]):
        sl = slice(j * LANES, (j + 1) * LANES)
        mx = s_scr[j, 0:MAX_ROWS, :]
        for r in range(1, nk // MAX_ROWS):
            mx = jnp.maximum(mx, s_scr[j, r * MAX_ROWS:(r + 1) * MAX_ROWS, :])
        m_new = jnp.maximum(m[:, sl], jnp.max(mx, axis=0, keepdims=True))
        for r in range(nk // EXP_ROWS):
            rows = slice(r * EXP_ROWS, (r + 1) * EXP_ROWS)
            p_scr[j, rows, :] = jnp.exp2(s_scr[j, rows, :] - m_new).astype(BF16)
        m_out.append(m_new)
        a_out.append(jnp.exp2(m[:, sl] - m_new))
    cat = lambda xs: jnp.concatenate(xs, axis=1)
    return cat(m_out), cat(a_out)


def _attn_a_kernel(*refs, tq, tk, n_x_chunks):
    n_in = 10 if n_x_chunks else 8
    if n_x_chunks:
        q_ref, kx_ref, vxt_ref, kc_ref, vct_ref, gate_ref, x_ref, gx_ref, w_ref, out_ref = refs[:n_in]
    else:
        q_ref, kc_ref, vct_ref, gate_ref, x_ref, gx_ref, w_ref, out_ref = refs[:n_in]
    o_scr = refs[n_in]
    s_bufs = refs[n_in + 1:n_in + 1 + N_SCORE_BUFS]
    p_bufs = refs[n_in + 1 + N_SCORE_BUFS:]
    kv_head = pl.program_id(2)
    vrow = pl.multiple_of(kv_head * HEAD_DIM, HEAD_DIM)
    a_t = _stack_qt(q_ref[0], kv_head % 2)
    cols = A_GROUP * tq
    lc = kc_ref.shape[1]

    n_chunks = 1 + n_x_chunks
    chunk_rows = lambda i: lc if i == 0 else tk
    k_blk = lambda i: kc_ref[0] if i == 0 else kx_ref[0, (i - 1) * tk:i * tk, :]
    v_blk = lambda i: (vct_ref[0, 0, pl.ds(vrow, HEAD_DIM), :] if i == 0
                       else vxt_ref[0, i - 1, pl.ds(vrow, HEAD_DIM), :])

    def scores(i):
        _store_cols(s_bufs[i % N_SCORE_BUFS], 0, jnp.dot(k_blk(i), a_t, preferred_element_type=F32))

    def consume(i, carry):
        m, acc = carry
        nk = chunk_rows(i)
        p_scr = p_bufs[i % len(p_bufs)]
        m, alpha = _softmax_cols(s_bufs[i % N_SCORE_BUFS], p_scr, nk, m)
        v_ext = jnp.concatenate([v_blk(i), jnp.ones((SUM_ROWS, nk), BF16)], axis=0)
        acc = alpha * acc + jnp.dot(v_ext, _load_cols(p_scr, nk), preferred_element_type=F32)
        return m, acc

    carry = (jnp.full((1, cols), NEG_BIG, F32), jnp.zeros((HEAD_DIM + SUM_ROWS, cols), F32))
    scores(0)
    for i in range(n_chunks):
        if i + 1 < n_chunks:
            scores(i + 1)
        carry = consume(i, carry)
    _, acc = carry
    o_t = acc[:HEAD_DIM] / acc[HEAD_DIM:HEAD_DIM + 1]
    o_scr[kv_head] = jnp.concatenate([o_t[:, h * tq:(h + 1) * tq] for h in range(A_GROUP)], axis=0).T

    @pl.when(kv_head == A_KV_HEADS - 1)
    def _():
        o_full = jnp.concatenate([o_scr[j] for j in range(A_KV_HEADS)], axis=1)
        u = (o_full * gate_ref[0].astype(F32)).astype(BF16)
        y = jnp.dot(u, w_ref[...], preferred_element_type=F32)
        out_ref[0] = x_ref[0] + gx_ref[0, 0] * y


def _attn_a(q, kx, vxt, kc, vct, gate, x, mods, w_out, tq):
    b, r, d = x.shape
    has_x = kx is not None
    shared = mods.shape[0] == 1
    qrow = lambda i, j, h: (i, j, 0)
    kvp = lambda i, j, h: (i, 0, h // 2)
    whole = lambda i, j, h: (i, 0, 0, 0)
    in_specs = [pl.BlockSpec((1, tq, A_GROUP * HEAD_DIM), lambda i, j, h: (i, j, h))]
    args = [q]
    lc = kc.shape[1]
    tk = lc
    n_x_chunks = 0
    if has_x:
        t = kx.shape[1]
        n_x_chunks, tk = vxt.shape[1], vxt.shape[3]
        in_specs += [pl.BlockSpec((1, t, LANES), kvp), pl.BlockSpec((1,) + vxt.shape[1:], whole)]
        args += [kx, vxt]
    in_specs += [
        pl.BlockSpec((1, lc, LANES), kvp),
        pl.BlockSpec((1,) + vct.shape[1:], whole),
        pl.BlockSpec((1, tq, d), qrow),
        pl.BlockSpec((1, tq, d), qrow),
        pl.BlockSpec((1, 1, 1, d), (lambda i, j, h: (0, 2, 0, 0)) if shared else (lambda i, j, h: (i, 2, 0, 0))),
        pl.BlockSpec((d, d), lambda i, j, h: (0, 0)),
    ]
    args += [kc, vct, gate, x, mods, w_out]
    return pl.pallas_call(
        functools.partial(_attn_a_kernel, tq=tq, tk=tk, n_x_chunks=n_x_chunks),
        grid=(b, r // tq, A_KV_HEADS),
        in_specs=in_specs,
        out_specs=pl.BlockSpec((1, tq, d), qrow),
        out_shape=jax.ShapeDtypeStruct((b, r, d), F32),
        scratch_shapes=[
            pltpu.VMEM((A_KV_HEADS, tq, A_GROUP * HEAD_DIM), F32),
            *[pltpu.VMEM((A_GROUP * tq // LANES, max(tk, lc), LANES), F32)] * N_SCORE_BUFS,
            *[pltpu.VMEM((A_GROUP * tq // LANES, max(tk, lc), LANES), BF16)] * 2,
        ],
        compiler_params=_cparams(("parallel", "parallel", "arbitrary")),
        name="attn_a_x" if has_x else "attn_a_ctx",
    )(*args)


def _store_vt_chunks(v_ref, v):
    for c in range(v.shape[0] // NAT_CHUNK):
        v_ref[0, c] = v[c * NAT_CHUNK:(c + 1) * NAT_CHUNK].T.astype(BF16)


def _inproj_b_kernel(x_ref, ng_ref, sc_ref, sh_ref, w_ref, *out_refs, kv_only):
    h = _adaln(x_ref[0], ng_ref[...], sc_ref[0, 0], sh_ref[0, 0])
    p = jnp.dot(h.astype(BF16), w_ref[...], preferred_element_type=F32)
    if kv_only:
        k_ref, v_ref = out_refs
        k_ref[0] = p[:, :B_WIDTH].astype(BF16)
        _store_vt_chunks(v_ref, p[:, B_WIDTH:])
    else:
        q_ref, k_ref, v_ref, g_ref = out_refs
        q_ref[0] = (p[:, :B_WIDTH] * (ATTN_SCALE * LOG2E)).astype(BF16)
        k_ref[0] = p[:, B_WIDTH:2 * B_WIDTH].astype(BF16)
        _store_vt_chunks(v_ref, p[:, 2 * B_WIDTH:3 * B_WIDTH])
        g_ref[0] = _silu(p[:, 3 * B_WIDTH:]).astype(BF16)


def _inproj_b(x, ng, mods, w, tm, kv_only):
    b, r, d = x.shape
    n = w.shape[1]
    shared = mods.shape[0] == 1
    mod_idx = (lambda i, c: (0, c, 0, 0)) if shared else (lambda i, c: (i, c, 0, 0))
    row = lambda i, j: (i, j, 0)
    row_spec = pl.BlockSpec((1, tm, B_WIDTH), row)
    row_shape = jax.ShapeDtypeStruct((b, r, B_WIDTH), BF16)
    vt_spec = pl.BlockSpec((1, tm // NAT_CHUNK, B_WIDTH, NAT_CHUNK), lambda i, j: (i, j, 0, 0))
    vt_shape = jax.ShapeDtypeStruct((b, r // NAT_CHUNK, B_WIDTH, NAT_CHUNK), BF16)
    if kv_only:
        out_specs, out_shape = [row_spec, vt_spec], [row_shape, vt_shape]
    else:
        out_specs, out_shape = [row_spec, row_spec, vt_spec, row_spec], [row_shape, row_shape, vt_shape, row_shape]
    return pl.pallas_call(
        functools.partial(_inproj_b_kernel, kv_only=kv_only),
        grid=(b, r // tm),
        in_specs=[
            pl.BlockSpec((1, tm, d), row),
            pl.BlockSpec((1, d), lambda i, j: (0, 0)),
            pl.BlockSpec((1, 1, 1, d), lambda i, j: mod_idx(i, 1)),
            pl.BlockSpec((1, 1, 1, d), lambda i, j: mod_idx(i, 0)),
            pl.BlockSpec((d, n), lambda i, j: (0, 0)),
        ],
        out_specs=out_specs,
        out_shape=out_shape,
        compiler_params=_cparams(("parallel", "parallel")),
        name="inproj_b_ctx" if kv_only else "inproj_b_x",
    )(x, ng, mods, mods, w)


def _nat_scores(g, n_groups, q_ref, k_ref, kc_ref, tab_ref, s_scr):
    c0 = min(max(g - 1, 0), n_groups - NAT_SPAN)
    kind = int(g > 0) + int(g == n_groups - 1)
    qs, ks = g * NAT_CHUNK, c0 * NAT_CHUNK
    qt = q_ref[0, qs:qs + NAT_CHUNK, :].astype(F32).T
    top = lax.broadcasted_iota(jnp.int32, qt.shape, 0) < HEAD_DIM
    a_t = jnp.concatenate([jnp.where(top, qt, 0.0), jnp.where(top, 0.0, qt)], axis=1).astype(BF16)
    s_span = jnp.dot(k_ref[0, ks:ks + NAT_KEYS, :], a_t, preferred_element_type=F32)
    for j in range(s_scr.shape[0]):
        s_scr[j, 0:NAT_KEYS, :] = s_span[:, j * LANES:(j + 1) * LANES] + tab_ref[0, kind, j]
    _store_cols(s_scr, NAT_KEYS, jnp.dot(kc_ref[0], a_t, preferred_element_type=F32))


def _nat_softmax(s_scr, p_scr):
    nk = s_scr.shape[1]
    for j in range(s_scr.shape[0]):
        mx = s_scr[j, 0:MAX_ROWS, :]
        for r in range(1, nk // MAX_ROWS):
            mx = jnp.maximum(mx, s_scr[j, r * MAX_ROWS:(r + 1) * MAX_ROWS, :])
        m = jnp.max(mx, axis=0, keepdims=True)
        for r in range(nk // EXP_ROWS):
            rows = slice(r * EXP_ROWS, (r + 1) * EXP_ROWS)
            p_scr[j, rows, :] = jnp.exp2(s_scr[j, rows, :] - m).astype(BF16)


def _nat_output(g, n_groups, vt_ref, vct_ref, p_scr, o_ref):
    nk = p_scr.shape[1]
    c0 = min(max(g - 1, 0), n_groups - NAT_SPAN)
    qs = g * NAT_CHUNK
    v_t = jnp.concatenate([vt_ref[0, c0 + c] for c in range(NAT_SPAN)] + [vct_ref[0, 0]], axis=1)
    v_ext = jnp.concatenate([v_t, jnp.ones((SUM_ROWS, nk), BF16)], axis=0)
    acc = jnp.dot(v_ext, _load_cols(p_scr, nk), preferred_element_type=F32)
    o_t = acc[:LANES] / acc[LANES:LANES + 1]
    both = jnp.concatenate([o_t[:HEAD_DIM, :NAT_CHUNK], o_t[HEAD_DIM:, NAT_CHUNK:]], axis=0)
    o_ref[0, qs:qs + NAT_CHUNK, :] = both.T.astype(BF16)


def _natten_kernel(q_ref, k_ref, vt_ref, kc_ref, vct_ref, tab_ref, o_ref, *scratch, n_groups):
    n_buf = len(scratch) // 2
    s_of = lambda g: scratch[g % n_buf]
    p_of = lambda g: scratch[n_buf + g % n_buf]
    _nat_scores(0, n_groups, q_ref, k_ref, kc_ref, tab_ref, s_of(0))
    _nat_scores(1, n_groups, q_ref, k_ref, kc_ref, tab_ref, s_of(1))
    for g in range(n_groups):
        if g + 2 < n_groups:
            _nat_scores(g + 2, n_groups, q_ref, k_ref, kc_ref, tab_ref, s_of(g + 2))
        _nat_softmax(s_of(g), p_of(g))
        _nat_output(g, n_groups, vt_ref, vct_ref, p_of(g), o_ref)


def _natten(q, k, vt, kc, vct, tab):
    b, t, d = q.shape
    lc = kc.shape[1]
    n_groups = t // NAT_CHUNK
    nk = NAT_KEYS + lc
    blk = lambda h, i: (i, 0, h)
    return pl.pallas_call(
        functools.partial(_natten_kernel, n_groups=n_groups),
        grid=(N_PAIRS, b),
        in_specs=[
            pl.BlockSpec((1, t, LANES), blk),
            pl.BlockSpec((1, t, LANES), blk),
            pl.BlockSpec((1, n_groups, LANES, NAT_CHUNK), lambda h, i: (i, 0, h, 0)),
            pl.BlockSpec((1, lc, LANES), blk),
            pl.BlockSpec((1, 1, LANES, lc), lambda h, i: (i, 0, h, 0)),
            pl.BlockSpec((1,) + tab.shape[1:], lambda h, i: (h, 0, 0, 0, 0)),
        ],
        out_specs=pl.BlockSpec((1, t, LANES), blk),
        out_shape=jax.ShapeDtypeStruct((b, t, d), BF16),
        scratch_shapes=[pltpu.VMEM((2 * NAT_CHUNK // LANES, nk, LANES), F32)] * NAT_BUFS
        + [pltpu.VMEM((2 * NAT_CHUNK // LANES, nk, LANES), BF16)] * NAT_BUFS,
        compiler_params=_cparams(("parallel", "parallel")),
        name="natten_b",
    )(q, k, vt, kc, vct, tab)


def _outproj_final_kernel(o_ref, gate_ref, x_ref, gx_ref, w_ref, fg_ref, out_ref):
    u = (o_ref[0].astype(F32) * gate_ref[0].astype(F32)).astype(BF16)
    y = jnp.dot(u, w_ref[...], preferred_element_type=F32)
    x2 = x_ref[0] + gx_ref[0, 0] * y
    ms = jnp.mean(x2 * x2, axis=-1, keepdims=True)
    out_ref[0] = x2 * lax.rsqrt(ms + NORM_EPS) * fg_ref[...]


def _outproj_final(o, gate, x, mods, w_out, fg, tm):
    b, t, d = x.shape
    row = lambda i, j: (i, j, 0)
    return pl.pallas_call(
        _outproj_final_kernel,
        grid=(b, t // tm),
        in_specs=[
            pl.BlockSpec((1, tm, d), row),
            pl.BlockSpec((1, tm, d), row),
            pl.BlockSpec((1, tm, d), row),
            pl.BlockSpec((1, 1, 1, d), lambda i, j: (i, 2, 0, 0)),
            pl.BlockSpec((d, d), lambda i, j: (0, 0)),
            pl.BlockSpec((1, d), lambda i, j: (0, 0)),
        ],
        out_specs=pl.BlockSpec((1, tm, d), row),
        out_shape=jax.ShapeDtypeStruct((b, t, d), F32),
        compiler_params=_cparams(("parallel", "parallel")),
        name="outproj_final",
    )(o, gate, x, mods, w_out, fg)


def _rope_tables(t_len):
    pos = np.arange(t_len)
    inv = ROPE_THETA ** (-np.arange(0, ROPE_AXIS_DIM, 2, dtype=np.float64) / ROPE_AXIS_DIM)
    ang_r = (pos // GRID_W)[:, None] * inv
    ang_c = (pos % GRID_W)[:, None] * inv
    zero = np.zeros_like(ang_r)
    cos_h = np.concatenate([np.cos(ang_r), np.cos(ang_r), np.cos(ang_c), np.cos(ang_c)], axis=1)
    sup_h = np.concatenate([-np.sin(ang_r), zero, -np.sin(ang_c), zero], axis=1)
    sdn_h = np.concatenate([zero, np.sin(ang_r), zero, np.sin(ang_c)], axis=1)
    two = lambda a: jnp.asarray(np.concatenate([a, a], axis=1), F32)
    return two(cos_h), two(sup_h), two(sdn_h)


def _nat_table_kernel(t2_ref, o_ref, *, plans):
    low = lax.broadcasted_iota(jnp.int32, (GRID_W, LANES), 1) < GRID_W
    neg = jnp.full((GRID_W, LANES), NEG_BIG, F32)
    half_rows = NAT_ROWS // 2
    for kind, plan in enumerate(plans):
        for head in range(2):
            for i2 in range(half_rows):
                for s in range(NAT_SPAN * NAT_ROWS):
                    d0, d1 = plan[s * NAT_ROWS + 2 * i2], plan[s * NAT_ROWS + 2 * i2 + 1]
                    b0 = neg if d0 is None else t2_ref[0, head, d0]
                    b1 = neg if d1 is None else t2_ref[0, head, d1]
                    o_ref[0, kind, head * half_rows + i2, s * GRID_W:(s + 1) * GRID_W, :] = jnp.where(low, b0, b1)


def _natten_tables(rpb, rows):
    h = rpb.shape[0]
    qcol = np.arange(GRID_W)
    c0 = np.clip(qcol - WIN_C // 2, 0, GRID_W - WIN_C)
    kcol = np.arange(GRID_W)
    valid = (kcol[:, None] >= c0[None, :]) & (kcol[:, None] < c0[None, :] + WIN_C)
    pad = GRID_W - WIN_C
    flipped = jnp.pad(rpb.astype(F32) * LOG2E, ((0, 0), (0, 0), (pad, pad)))[:, :, ::-1]
    rows2 = [jnp.concatenate([flipped[:, :, GRID_W - 1 - k:2 * GRID_W - 1 - k]] * 2, axis=2) for k in range(GRID_W)]
    t2 = jnp.where(jnp.asarray(np.concatenate([valid, valid], axis=1))[None, None], jnp.stack(rows2, axis=2), NEG_BIG)
    t2 = t2.reshape(h // 2, 2, 2 * WIN_R - 1, GRID_W, LANES)
    n_groups = rows // NAT_ROWS
    span_rows = NAT_SPAN * NAT_ROWS

    def plan(g):
        ks = int(np.clip(g - 1, 0, n_groups - NAT_SPAN)) * NAT_ROWS
        out = []
        for s in range(span_rows):
            for i in range(NAT_ROWS):
                rq, rk = NAT_ROWS * g + i, ks + s
                r0 = int(np.clip(rq - WIN_R // 2, 0, rows - WIN_R))
                out.append(rk - rq + WIN_R - 1 if r0 <= rk < r0 + WIN_R else None)
        return tuple(out)

    plans = (plan(0), plan(1), plan(n_groups - 1))
    assert all(plan(g) == plans[1] for g in range(1, n_groups - 1))
    n_col = 2 * NAT_ROWS * GRID_W // LANES
    return pl.pallas_call(
        functools.partial(_nat_table_kernel, plans=plans),
        grid=(h // 2,),
        in_specs=[pl.BlockSpec((1,) + t2.shape[1:], lambda p: (p, 0, 0, 0, 0))],
        out_specs=pl.BlockSpec((1, 3, n_col, span_rows * GRID_W, LANES), lambda p: (p, 0, 0, 0, 0)),
        out_shape=jax.ShapeDtypeStruct((h // 2, 3, n_col, span_rows * GRID_W, LANES), F32),
        compiler_params=_cparams(("parallel",)),
        name="natten_table",
    )(t2)


def kernel(x, c, ctx, c_ctx, norm_g, w_mod, b_mod, a_w_in, a_q_norm_g, a_k_norm_g, a_w_out,
           b_w_in, b_rpb, b_w_out, final_norm_g):
    bsz, t, d = x.shape
    lc = ctx.shape[1]

    n_rows = ((bsz + 1 + 7) // 8) * 8
    c_rows = jnp.zeros((n_rows, d), F32).at[:bsz].set(c).at[bsz].set(c_ctx)
    mods = _modulation(c_rows, w_mod, b_mod).reshape(w_mod.shape[0], n_rows, 3, 1, d)

    cos_t, sup, sdn = _rope_tables(t)
    one_t = jnp.ones((lc, LANES), F32)
    zero_t = jnp.zeros((lc, LANES), F32)
    head_block = np.kron(np.eye(LANES // HEAD_DIM), np.ones((HEAD_DIM, HEAD_DIM)))
    bd = jnp.asarray(head_block, BF16)
    two = lambda g: jnp.concatenate([g, g]).reshape(1, LANES).astype(F32)

    mx, mc = mods[0, :bsz], mods[0, bsz:bsz + 1]
    ng = norm_g[0].reshape(1, d)
    w_in = a_w_in[0].astype(BF16)
    qg, kg = two(a_q_norm_g[0]) * (ATTN_SCALE * LOG2E), two(a_k_norm_g[0])
    q, k, vt, gate = _inproj_a(x, ng, mx, w_in, qg, kg, bd, cos_t, sup, sdn, tm=TM_INPROJ_A, tk=TK_ATTN_A)
    qc, kc, vct, gate_c = _inproj_a(ctx, ng, mc, w_in, qg, kg, bd, one_t, zero_t, zero_t, tm=lc, tk=lc)
    w_out = a_w_out[0].astype(BF16)
    x1 = _attn_a(q, k, vt, kc, vct, gate, x, mx, w_out, tq=TQ_ATTN_A)
    ctx1 = _attn_a(qc, None, None, kc, vct, gate_c, ctx, mc, w_out, tq=lc)

    mx, mc = mods[1, :bsz], mods[1, bsz:bsz + 1]
    ng = norm_g[1].reshape(1, d)
    w_in = b_w_in[0].astype(BF16)
    q, k, vt, gate = _inproj_b(x1, ng, mx, w_in, tm=TM_INPROJ_B, kv_only=False)
    kc, vct = _inproj_b(ctx1, ng, mc, w_in[:, B_WIDTH:3 * B_WIDTH], tm=lc, kv_only=True)
    o = _natten(q, k, vt, kc, vct, _natten_tables(b_rpb[0], t // GRID_W))
    return _outproj_final(o, gate, x1, mx, b_w_out[0].astype(BF16), final_norm_g.reshape(1, d), tm=TM_OUTPROJ)
```

```python
import functools

import jax
import jax.numpy as jnp
import numpy as np
from jax import lax
from jax.experimental import pallas as pl
from jax.experimental.pallas import tpu as pltpu

F32 = jnp.float32
BF16 = jnp.bfloat16

LANES = 128
VMEM_LIMIT = 56 * 1024 * 1024

D_MODEL = 1024
GRID_W = 64
HEAD_DIM = 64
NORM_EPS = 1e-6
ATTN_SCALE = HEAD_DIM ** -0.5
A_HEADS = 16
A_KV_HEADS = 4
A_WIDTH = A_HEADS * HEAD_DIM
A_KV_WIDTH = A_KV_HEADS * HEAD_DIM
A_GROUP = A_HEADS // A_KV_HEADS
ROPE_THETA = 10000.0
ROPE_AXIS_DIM = HEAD_DIM // 2
ROPE_HALF = ROPE_AXIS_DIM // 2
B_HEADS = 16
B_WIDTH = B_HEADS * HEAD_DIM
WIN_R = 8
WIN_C = 16
N_PAIRS = D_MODEL // LANES
NEG_BIG = -1e30
MAX_ROWS = 128
EXP_ROWS = 64
SUM_ROWS = 16
NAT_ROWS = 4
NAT_CHUNK = NAT_ROWS * GRID_W
NAT_SPAN = 3
NAT_KEYS = NAT_SPAN * NAT_CHUNK
NAT_BUFS = 3
N_SCORE_BUFS = 3
TM_INPROJ_A = 1024
TK_ATTN_A = 512
TQ_ATTN_A = 512
TM_INPROJ_B = 512
TM_OUTPROJ = 1024
LOG2E = 1.4426950408889634


def _cparams(sem, flags=None):
    return pltpu.CompilerParams(dimension_semantics=sem, vmem_limit_bytes=VMEM_LIMIT, flags=flags)


def _mod_kernel(c_ref, w_ref, b_ref, o_ref):
    c = c_ref[...]
    s = c * jax.nn.sigmoid(c)
    o_ref[0] = jnp.dot(s, w_ref[0], precision=lax.Precision.HIGHEST,
                       preferred_element_type=F32) + b_ref[0]


def _modulation(c_rows, w_mod, b_mod):
    depth, d, n = w_mod.shape
    rows = c_rows.shape[0]
    tn = 1024
    return pl.pallas_call(
        _mod_kernel,
        grid=(depth, n // tn),
        in_specs=[
            pl.BlockSpec((rows, d), lambda l, j: (0, 0)),
            pl.BlockSpec((1, d, tn), lambda l, j: (l, 0, j)),
            pl.BlockSpec((1, 1, tn), lambda l, j: (l, 0, j)),
        ],
        out_specs=pl.BlockSpec((1, rows, tn), lambda l, j: (l, 0, j)),
        out_shape=jax.ShapeDtypeStruct((depth, rows, n), F32),
        compiler_params=_cparams(("arbitrary", "arbitrary")),
        name="adaln_mod",
    )(c_rows, w_mod, b_mod.reshape(depth, 1, n))


def _adaln(x, ng, sc, sh):
    ms = jnp.mean(x * x, axis=-1, keepdims=True)
    y = x * lax.rsqrt(ms + NORM_EPS) * ng
    return y * (1.0 + sc) + sh


def _silu(z):
    return z * jax.nn.sigmoid(z)


def _head_norm_rope(blk, gain, bd, cos_t, sin_up, sin_dn):
    sq = blk * blk
    hi = sq.astype(BF16)
    lo = (sq - hi.astype(F32)).astype(BF16)
    ssum = (jnp.dot(hi, bd, preferred_element_type=F32) + jnp.dot(lo, bd, preferred_element_type=F32))
    n = blk * lax.rsqrt(ssum * (1.0 / HEAD_DIM) + NORM_EPS) * gain
    up = pltpu.roll(n, LANES - ROPE_HALF, 1)
    dn = pltpu.roll(n, ROPE_HALF, 1)
    return n * cos_t + up * sin_up + dn * sin_dn


def _inproj_a_kernel(x_ref, ng_ref, sc_ref, sh_ref, w_ref, qg_ref, kg_ref, bd_ref, cos_ref, sup_ref, sdn_ref,
                     q_ref, k_ref, v_ref, g_ref):
    h = _adaln(x_ref[0], ng_ref[...], sc_ref[0, 0], sh_ref[0, 0])
    p = jnp.dot(h.astype(BF16), w_ref[...], preferred_element_type=F32)
    bd = bd_ref[...]
    cos_t, sup, sdn = cos_ref[...], sup_ref[...], sdn_ref[...]
    qg, kg = qg_ref[...], kg_ref[...]
    for j in range(A_WIDTH // LANES):
        blk = p[:, j * LANES:(j + 1) * LANES]
        q_ref[0, :, j * LANES:(j + 1) * LANES] = _head_norm_rope(blk, qg, bd, cos_t, sup, sdn).astype(BF16)
    for j in range(A_KV_WIDTH // LANES):
        blk = p[:, A_WIDTH + j * LANES:A_WIDTH + (j + 1) * LANES]
        k_ref[0, :, j * LANES:(j + 1) * LANES] = _head_norm_rope(blk, kg, bd, cos_t, sup, sdn).astype(BF16)
    v = p[:, A_WIDTH + A_KV_WIDTH:A_WIDTH + 2 * A_KV_WIDTH]
    tk = v_ref.shape[3]
    for c in range(v.shape[0] // tk):
        v_ref[0, c] = v[c * tk:(c + 1) * tk].T.astype(BF16)
    g_ref[0] = _silu(p[:, A_WIDTH + 2 * A_KV_WIDTH:]).astype(BF16)


def _inproj_a(x, ng, mods, w, qg, kg, bd, cos_t, sup, sdn, tm, tk):
    b, r, d = x.shape
    n = w.shape[1]
    shared = mods.shape[0] == 1
    mod_idx = (lambda i, j, c: (0, c, 0, 0)) if shared else (lambda i, j, c: (i, c, 0, 0))
    row = lambda i, j: (i, j, 0)
    const2 = lambda i, j: (0, 0)
    return pl.pallas_call(
        _inproj_a_kernel,
        grid=(b, r // tm),
        in_specs=[
            pl.BlockSpec((1, tm, d), row),
            pl.BlockSpec((1, d), const2),
            pl.BlockSpec((1, 1, 1, d), lambda i, j: mod_idx(i, j, 1)),
            pl.BlockSpec((1, 1, 1, d), lambda i, j: mod_idx(i, j, 0)),
            pl.BlockSpec((d, n), const2),
            pl.BlockSpec((1, LANES), const2),
            pl.BlockSpec((1, LANES), const2),
            pl.BlockSpec((LANES, LANES), const2),
            pl.BlockSpec((tm, LANES), lambda i, j: (j, 0)),
            pl.BlockSpec((tm, LANES), lambda i, j: (j, 0)),
            pl.BlockSpec((tm, LANES), lambda i, j: (j, 0)),
        ],
        out_specs=[
            pl.BlockSpec((1, tm, A_WIDTH), row),
            pl.BlockSpec((1, tm, A_KV_WIDTH), row),
            pl.BlockSpec((1, tm // tk, A_KV_WIDTH, tk), lambda i, j: (i, j, 0, 0)),
            pl.BlockSpec((1, tm, A_WIDTH), row),
        ],
        out_shape=[
            jax.ShapeDtypeStruct((b, r, A_WIDTH), BF16),
            jax.ShapeDtypeStruct((b, r, A_KV_WIDTH), BF16),
            jax.ShapeDtypeStruct((b, r // tk, A_KV_WIDTH, tk), BF16),
            jax.ShapeDtypeStruct((b, r, A_WIDTH), BF16),
        ],
        compiler_params=_cparams(("parallel", "parallel")),
        name="inproj_a",
    )(x, ng, mods, mods, w, qg, kg, bd, cos_t, sup, sdn)


def _stack_qt(q4, half):
    qt = q4.astype(F32).T
    zero = jnp.zeros((HEAD_DIM, qt.shape[1]), F32)
    first = half == 0

    def place(h_t):
        return jnp.where(first, jnp.concatenate([h_t, zero], axis=0), jnp.concatenate([zero, h_t], axis=0))

    heads = [place(qt[h * HEAD_DIM:(h + 1) * HEAD_DIM]) for h in range(A_GROUP)]
    return jnp.concatenate(heads, axis=1).astype(BF16)


def _store_cols(scr, row_start, val):
    n = val.shape[0]
    for j in range(scr.shape[0]):
        scr[j, row_start:row_start + n, :] = val[:, j * LANES:(j + 1) * LANES]


def _load_cols(scr, nk):
    return jnp.concatenate([scr[j, 0:nk, :] for j in range(scr.shape[0])], axis=1)


def _softmax_cols(s_scr, p_scr, nk, m):
    m_out, a_out = [], []
    for j in range(s_scr.shape[0]):
        sl = slice(j * LANES, (j + 1) * LANES)
        mx = s_scr[j, 0:MAX_ROWS, :]
        for r in range(1, nk // MAX_ROWS):
            mx = jnp.maximum(mx, s_scr[j, r * MAX_ROWS:(r + 1) * MAX_ROWS, :])
        m_new = jnp.maximum(m[:, sl], jnp.max(mx, axis=0, keepdims=True))
        for r in range(nk // EXP_ROWS):
            rows = slice(r * EXP_ROWS, (r + 1) * EXP_ROWS)
            p_scr[j, rows, :] = jnp.exp2(s_scr[j, rows, :] - m_new).astype(BF16)
        m_out.append(m_new)
        a_out.append(jnp.exp2(m[:, sl] - m_new))
    cat = lambda xs: jnp.concatenate(xs, axis=1)
    return cat(m_out), cat(a_out)


def _attn_a_kernel(*refs, tq, tk, n_x_chunks):
    n_in = 10 if n_x_chunks else 8
    if n_x_chunks:
        q_ref, kx_ref, vxt_ref, kc_ref, vct_ref, gate_ref, x_ref, gx_ref, w_ref, out_ref = refs[:n_in]
    else:
        q_ref, kc_ref, vct_ref, gate_ref, x_ref, gx_ref, w_ref, out_ref = refs[:n_in]
    o_scr = refs[n_in]
    s_bufs = refs[n_in + 1:n_in + 1 + N_SCORE_BUFS]
    p_bufs = refs[n_in + 1 + N_SCORE_BUFS:]
    kv_head = pl.program_id(2)
    vrow = pl.multiple_of(kv_head * HEAD_DIM, HEAD_DIM)
    a_t = _stack_qt(q_ref[0], kv_head % 2)
    cols = A_GROUP * tq
    lc = kc_ref.shape[1]

    n_chunks = 1 + n_x_chunks
    chunk_rows = lambda i: lc if i == 0 else tk
    k_blk = lambda i: kc_ref[0] if i == 0 else kx_ref[0, (i - 1) * tk:i * tk, :]
    v_blk = lambda i: (vct_ref[0, 0, pl.ds(vrow, HEAD_DIM), :] if i == 0
                       else vxt_ref[0, i - 1, pl.ds(vrow, HEAD_DIM), :])

    def scores(i):
        _store_cols(s_bufs[i % N_SCORE_BUFS], 0, jnp.dot(k_blk(i), a_t, preferred_element_type=F32))

    def consume(i, carry):
        m, acc = carry
        nk = chunk_rows(i)
        p_scr = p_bufs[i % len(p_bufs)]
        m, alpha = _softmax_cols(s_bufs[i % N_SCORE_BUFS], p_scr, nk, m)
        v_ext = jnp.concatenate([v_blk(i), jnp.ones((SUM_ROWS, nk), BF16)], axis=0)
        acc = alpha * acc + jnp.dot(v_ext, _load_cols(p_scr, nk), preferred_element_type=F32)
        return m, acc

    carry = (jnp.full((1, cols), NEG_BIG, F32), jnp.zeros((HEAD_DIM + SUM_ROWS, cols), F32))
    scores(0)
    for i in range(n_chunks):
        if i + 1 < n_chunks:
            scores(i + 1)
        carry = consume(i, carry)
    _, acc = carry
    o_t = acc[:HEAD_DIM] / acc[HEAD_DIM:HEAD_DIM + 1]
    o_scr[kv_head] = jnp.concatenate([o_t[:, h * tq:(h + 1) * tq] for h in range(A_GROUP)], axis=0).T

    @pl.when(kv_head == A_KV_HEADS - 1)
    def _():
        o_full = jnp.concatenate([o_scr[j] for j in range(A_KV_HEADS)], axis=1)
        u = (o_full * gate_ref[0].astype(F32)).astype(BF16)
        y = jnp.dot(u, w_ref[...], preferred_element_type=F32)
        out_ref[0] = x_ref[0] + gx_ref[0, 0] * y


def _attn_a(q, kx, vxt, kc, vct, gate, x, mods, w_out, tq):
    b, r, d = x.shape
    has_x = kx is not None
    shared = mods.shape[0] == 1
    qrow = lambda i, j, h: (i, j, 0)
    kvp = lambda i, j, h: (i, 0, h // 2)
    whole = lambda i, j, h: (i, 0, 0, 0)
    in_specs = [pl.BlockSpec((1, tq, A_GROUP * HEAD_DIM), lambda i, j, h: (i, j, h))]
    args = [q]
    lc = kc.shape[1]
    tk = lc
    n_x_chunks = 0
    if has_x:
        t = kx.shape[1]
        n_x_chunks, tk = vxt.shape[1], vxt.shape[3]
        in_specs += [pl.BlockSpec((1, t, LANES), kvp), pl.BlockSpec((1,) + vxt.shape[1:], whole)]
        args += [kx, vxt]
    in_specs += [
        pl.BlockSpec((1, lc, LANES), kvp),
        pl.BlockSpec((1,) + vct.shape[1:], whole),
        pl.BlockSpec((1, tq, d), qrow),
        pl.BlockSpec((1, tq, d), qrow),
        pl.BlockSpec((1, 1, 1, d), (lambda i, j, h: (0, 2, 0, 0)) if shared else (lambda i, j, h: (i, 2, 0, 0))),
        pl.BlockSpec((d, d), lambda i, j, h: (0, 0)),
    ]
    args += [kc, vct, gate, x, mods, w_out]
    return pl.pallas_call(
        functools.partial(_attn_a_kernel, tq=tq, tk=tk, n_x_chunks=n_x_chunks),
        grid=(b, r // tq, A_KV_HEADS),
        in_specs=in_specs,
        out_specs=pl.BlockSpec((1, tq, d), qrow),
        out_shape=jax.ShapeDtypeStruct((b, r, d), F32),
        scratch_shapes=[
            pltpu.VMEM((A_KV_HEADS, tq, A_GROUP * HEAD_DIM), F32),
            *[pltpu.VMEM((A_GROUP * tq // LANES, max(tk, lc), LANES), F32)] * N_SCORE_BUFS,
            *[pltpu.VMEM((A_GROUP * tq // LANES, max(tk, lc), LANES), BF16)] * 2,
        ],
        compiler_params=_cparams(("parallel", "parallel", "arbitrary")),
        name="attn_a_x" if has_x else "attn_a_ctx",
    )(*args)


def _store_vt_chunks(v_ref, v):
    for c in range(v.shape[0] // NAT_CHUNK):
        v_ref[0, c] = v[c * NAT_CHUNK:(c + 1) * NAT_CHUNK].T.astype(BF16)


def _inproj_b_kernel(x_ref, ng_ref, sc_ref, sh_ref, w_ref, *out_refs, kv_only):
    h = _adaln(x_ref[0], ng_ref[...], sc_ref[0, 0], sh_ref[0, 0])
    p = jnp.dot(h.astype(BF16), w_ref[...], preferred_element_type=F32)
    if kv_only:
        k_ref, v_ref = out_refs
        k_ref[0] = p[:, :B_WIDTH].astype(BF16)
        _store_vt_chunks(v_ref, p[:, B_WIDTH:])
    else:
        q_ref, k_ref, v_ref, g_ref = out_refs
        q_ref[0] = (p[:, :B_WIDTH] * (ATTN_SCALE * LOG2E)).astype(BF16)
        k_ref[0] = p[:, B_WIDTH:2 * B_WIDTH].astype(BF16)
        _store_vt_chunks(v_ref, p[:, 2 * B_WIDTH:3 * B_WIDTH])
        g_ref[0] = _silu(p[:, 3 * B_WIDTH:]).astype(BF16)


def _inproj_b(x, ng, mods, w, tm, kv_only):
    b, r, d = x.shape
    n = w.shape[1]
    shared = mods.shape[0] == 1
    mod_idx = (lambda i, c: (0, c, 0, 0)) if shared else (lambda i, c: (i, c, 0, 0))
    row = lambda i, j: (i, j, 0)
    row_spec = pl.BlockSpec((1, tm, B_WIDTH), row)
    row_shape = jax.ShapeDtypeStruct((b, r, B_WIDTH), BF16)
    vt_spec = pl.BlockSpec((1, tm // NAT_CHUNK, B_WIDTH, NAT_CHUNK), lambda i, j: (i, j, 0, 0))
    vt_shape = jax.ShapeDtypeStruct((b, r // NAT_CHUNK, B_WIDTH, NAT_CHUNK), BF16)
    if kv_only:
        out_specs, out_shape = [row_spec, vt_spec], [row_shape, vt_shape]
    else:
        out_specs, out_shape = [row_spec, row_spec, vt_spec, row_spec], [row_shape, row_shape, vt_shape, row_shape]
    return pl.pallas_call(
        functools.partial(_inproj_b_kernel, kv_only=kv_only),
        grid=(b, r // tm),
        in_specs=[
            pl.BlockSpec((1, tm, d), row),
            pl.BlockSpec((1, d), lambda i, j: (0, 0)),
            pl.BlockSpec((1, 1, 1, d), lambda i, j: mod_idx(i, 1)),
            pl.BlockSpec((1, 1, 1, d), lambda i, j: mod_idx(i, 0)),
            pl.BlockSpec((d, n), lambda i, j: (0, 0)),
        ],
        out_specs=out_specs,
        out_shape=out_shape,
        compiler_params=_cparams(("parallel", "parallel")),
        name="inproj_b_ctx" if kv_only else "inproj_b_x",
    )(x, ng, mods, mods, w)


def _nat_scores(g, n_groups, q_ref, k_ref, kc_ref, tab_ref, s_scr):
    c0 = min(max(g - 1, 0), n_groups - NAT_SPAN)
    kind = int(g > 0) + int(g == n_groups - 1)
    qs, ks = g * NAT_CHUNK, c0 * NAT_CHUNK
    qt = q_ref[0, qs:qs + NAT_CHUNK, :].astype(F32).T
    top = lax.broadcasted_iota(jnp.int32, qt.shape, 0) < HEAD_DIM
    a_t = jnp.concatenate([jnp.where(top, qt, 0.0), jnp.where(top, 0.0, qt)], axis=1).astype(BF16)
    s_span = jnp.dot(k_ref[0, ks:ks + NAT_KEYS, :], a_t, preferred_element_type=F32)
    for j in range(s_scr.shape[0]):
        s_scr[j, 0:NAT_KEYS, :] = s_span[:, j * LANES:(j + 1) * LANES] + tab_ref[0, kind, j]
    _store_cols(s_scr, NAT_KEYS, jnp.dot(kc_ref[0], a_t, preferred_element_type=F32))


def _nat_softmax(s_scr, p_scr):
    nk = s_scr.shape[1]
    for j in range(s_scr.shape[0]):
        mx = s_scr[j, 0:MAX_ROWS, :]
        for r in range(1, nk // MAX_ROWS):
            mx = jnp.maximum(mx, s_scr[j, r * MAX_ROWS:(r + 1) * MAX_ROWS, :])
        m = jnp.max(mx, axis=0, keepdims=True)
        for r in range(nk // EXP_ROWS):
            rows = slice(r * EXP_ROWS, (r + 1) * EXP_ROWS)
            p_scr[j, rows, :] = jnp.exp2(s_scr[j, rows, :] - m).astype(BF16)


def _nat_output(g, n_groups, vt_ref, vct_ref, p_scr, gate_ref, o_ref):
    nk = p_scr.shape[1]
    c0 = min(max(g - 1, 0), n_groups - NAT_SPAN)
    qs = g * NAT_CHUNK
    v_t = jnp.concatenate([vt_ref[0, c0 + c] for c in range(NAT_SPAN)] + [vct_ref[0, 0]], axis=1)
    v_ext = jnp.concatenate([v_t, jnp.ones((SUM_ROWS, nk), BF16)], axis=0)
    acc = jnp.dot(v_ext, _load_cols(p_scr, nk), preferred_element_type=F32)
    o_t = acc[:LANES] / acc[LANES:LANES + 1]
    both = jnp.concatenate([o_t[:HEAD_DIM, :NAT_CHUNK], o_t[HEAD_DIM:, NAT_CHUNK:]], axis=0)
    gate = gate_ref[0, qs:qs + NAT_CHUNK, :].astype(F32)
    o_ref[0, qs:qs + NAT_CHUNK, :] = (both.T * gate).astype(BF16)


def _natten_kernel(q_ref, k_ref, vt_ref, kc_ref, vct_ref, tab_ref, gate_ref, o_ref, *scratch, n_groups):
    n_buf = len(scratch) // 2
    s_of = lambda g: scratch[g % n_buf]
    p_of = lambda g: scratch[n_buf + g % n_buf]
    _nat_scores(0, n_groups, q_ref, k_ref, kc_ref, tab_ref, s_of(0))
    _nat_scores(1, n_groups, q_ref, k_ref, kc_ref, tab_ref, s_of(1))
    for g in range(n_groups):
        if g + 2 < n_groups:
            _nat_scores(g + 2, n_groups, q_ref, k_ref, kc_ref, tab_ref, s_of(g + 2))
        _nat_softmax(s_of(g), p_of(g))
        _nat_output(g, n_groups, vt_ref, vct_ref, p_of(g), gate_ref, o_ref)


def _natten(q, k, vt, kc, vct, tab, gate):
    b, t, d = q.shape
    lc = kc.shape[1]
    n_groups = t // NAT_CHUNK
    nk = NAT_KEYS + lc
    blk = lambda h, i: (i, 0, h)
    return pl.pallas_call(
        functools.partial(_natten_kernel, n_groups=n_groups),
        grid=(N_PAIRS, b),
        in_specs=[
            pl.BlockSpec((1, t, LANES), blk),
            pl.BlockSpec((1, t, LANES), blk),
            pl.BlockSpec((1, n_groups, LANES, NAT_CHUNK), lambda h, i: (i, 0, h, 0)),
            pl.BlockSpec((1, lc, LANES), blk),
            pl.BlockSpec((1, 1, LANES, lc), lambda h, i: (i, 0, h, 0)),
            pl.BlockSpec((1,) + tab.shape[1:], lambda h, i: (h, 0, 0, 0, 0)),
            pl.BlockSpec((1, t, LANES), blk),
        ],
        out_specs=pl.BlockSpec((1, t, LANES), blk),
        out_shape=jax.ShapeDtypeStruct((b, t, d), BF16),
        scratch_shapes=[pltpu.VMEM((2 * NAT_CHUNK // LANES, nk, LANES), F32)] * NAT_BUFS
        + [pltpu.VMEM((2 * NAT_CHUNK // LANES, nk, LANES), BF16)] * NAT_BUFS,
        compiler_params=_cparams(("parallel", "parallel")),
        name="natten_b",
    )(q, k, vt, kc, vct, tab, gate)


def _outproj_final_kernel(u_ref, x_ref, gx_ref, w_ref, fg_ref, out_ref):
    y = jnp.dot(u_ref[0], w_ref[...], preferred_element_type=F32)
    x2 = x_ref[0] + gx_ref[0, 0] * y
    ms = jnp.mean(x2 * x2, axis=-1, keepdims=True)
    out_ref[0] = x2 * lax.rsqrt(ms + NORM_EPS) * fg_ref[...]


def _outproj_final(u, x, mods, w_out, fg, tm):
    b, t, d = x.shape
    row = lambda i, j: (i, j, 0)
    return pl.pallas_call(
        _outproj_final_kernel,
        grid=(b, t // tm),
        in_specs=[
            pl.BlockSpec((1, tm, d), row),
            pl.BlockSpec((1, tm, d), row),
            pl.BlockSpec((1, 1, 1, d), lambda i, j: (i, 2, 0, 0)),
            pl.BlockSpec((d, d), lambda i, j: (0, 0)),
            pl.BlockSpec((1, d), lambda i, j: (0, 0)),
        ],
        out_specs=pl.BlockSpec((1, tm, d), row),
        out_shape=jax.ShapeDtypeStruct((b, t, d), F32),
        compiler_params=_cparams(("parallel", "parallel")),
        name="outproj_final",
    )(u, x, mods, w_out, fg)


def _rope_tables(t_len):
    pos = np.arange(t_len)
    inv = ROPE_THETA ** (-np.arange(0, ROPE_AXIS_DIM, 2, dtype=np.float64) / ROPE_AXIS_DIM)
    ang_r = (pos // GRID_W)[:, None] * inv
    ang_c = (pos % GRID_W)[:, None] * inv
    zero = np.zeros_like(ang_r)
    cos_h = np.concatenate([np.cos(ang_r), np.cos(ang_r), np.cos(ang_c), np.cos(ang_c)], axis=1)
    sup_h = np.concatenate([-np.sin(ang_r), zero, -np.sin(ang_c), zero], axis=1)
    sdn_h = np.concatenate([zero, np.sin(ang_r), zero, np.sin(ang_c)], axis=1)
    two = lambda a: jnp.asarray(np.concatenate([a, a], axis=1), F32)
    return two(cos_h), two(sup_h), two(sdn_h)


def _nat_table_kernel(t2_ref, o_ref, *, plans):
    low = lax.broadcasted_iota(jnp.int32, (GRID_W, LANES), 1) < GRID_W
    neg = jnp.full((GRID_W, LANES), NEG_BIG, F32)
    half_rows = NAT_ROWS // 2
    for kind, plan in enumerate(plans):
        for head in range(2):
            for i2 in range(half_rows):
                for s in range(NAT_SPAN * NAT_ROWS):
                    d0, d1 = plan[s * NAT_ROWS + 2 * i2], plan[s * NAT_ROWS + 2 * i2 + 1]
                    b0 = neg if d0 is None else t2_ref[0, head, d0]
                    b1 = neg if d1 is None else t2_ref[0, head, d1]
                    o_ref[0, kind, head * half_rows + i2, s * GRID_W:(s + 1) * GRID_W, :] = jnp.where(low, b0, b1)


def _natten_tables(rpb, rows):
    h = rpb.shape[0]
    qcol = np.arange(GRID_W)
    c0 = np.clip(qcol - WIN_C // 2, 0, GRID_W - WIN_C)
    kcol = np.arange(GRID_W)
    valid = (kcol[:, None] >= c0[None, :]) & (kcol[:, None] < c0[None, :] + WIN_C)
    pad = GRID_W - WIN_C
    flipped = jnp.pad(rpb.astype(F32) * LOG2E, ((0, 0), (0, 0), (pad, pad)))[:, :, ::-1]
    rows2 = [jnp.concatenate([flipped[:, :, GRID_W - 1 - k:2 * GRID_W - 1 - k]] * 2, axis=2) for k in range(GRID_W)]
    t2 = jnp.where(jnp.asarray(np.concatenate([valid, valid], axis=1))[None, None], jnp.stack(rows2, axis=2), NEG_BIG)
    t2 = t2.reshape(h // 2, 2, 2 * WIN_R - 1, GRID_W, LANES)
    n_groups = rows // NAT_ROWS
    span_rows = NAT_SPAN * NAT_ROWS

    def plan(g):
        ks = int(np.clip(g - 1, 0, n_groups - NAT_SPAN)) * NAT_ROWS
        out = []
        for s in range(span_rows):
            for i in range(NAT_ROWS):
                rq, rk = NAT_ROWS * g + i, ks + s
                r0 = int(np.clip(rq - WIN_R // 2, 0, rows - WIN_R))
                out.append(rk - rq + WIN_R - 1 if r0 <= rk < r0 + WIN_R else None)
        return tuple(out)

    plans = (plan(0), plan(1), plan(n_groups - 1))
    assert all(plan(g) == plans[1] for g in range(1, n_groups - 1))
    n_col = 2 * NAT_ROWS * GRID_W // LANES
    return pl.pallas_call(
        functools.partial(_nat_table_kernel, plans=plans),
        grid=(h // 2,),
        in_specs=[pl.BlockSpec((1,) + t2.shape[1:], lambda p: (p, 0, 0, 0, 0))],
        out_specs=pl.BlockSpec((1, 3, n_col, span_rows * GRID_W, LANES), lambda p: (p, 0, 0, 0, 0)),
        out_shape=jax.ShapeDtypeStruct((h // 2, 3, n_col, span_rows * GRID_W, LANES), F32),
        compiler_params=_cparams(("parallel",)),
        name="natten_table",
    )(t2)


def kernel(x, c, ctx, c_ctx, norm_g, w_mod, b_mod, a_w_in, a_q_norm_g, a_k_norm_g, a_w_out,
           b_w_in, b_rpb, b_w_out, final_norm_g):
    bsz, t, d = x.shape
    lc = ctx.shape[1]

    n_rows = ((bsz + 1 + 7) // 8) * 8
    c_rows = jnp.zeros((n_rows, d), F32).at[:bsz].set(c).at[bsz].set(c_ctx)
    mods = _modulation(c_rows, w_mod, b_mod).reshape(w_mod.shape[0], n_rows, 3, 1, d)

    cos_t, sup, sdn = _rope_tables(t)
    one_t = jnp.ones((lc, LANES), F32)
    zero_t = jnp.zeros((lc, LANES), F32)
    head_block = np.kron(np.eye(LANES // HEAD_DIM), np.ones((HEAD_DIM, HEAD_DIM)))
    bd = jnp.asarray(head_block, BF16)
    two = lambda g: jnp.concatenate([g, g]).reshape(1, LANES).astype(F32)

    mx, mc = mods[0, :bsz], mods[0, bsz:bsz + 1]
    ng = norm_g[0].reshape(1, d)
    w_in = a_w_in[0].astype(BF16)
    qg, kg = two(a_q_norm_g[0]) * (ATTN_SCALE * LOG2E), two(a_k_norm_g[0])
    q, k, vt, gate = _inproj_a(x, ng, mx, w_in, qg, kg, bd, cos_t, sup, sdn, tm=TM_INPROJ_A, tk=TK_ATTN_A)
    qc, kc, vct, gate_c = _inproj_a(ctx, ng, mc, w_in, qg, kg, bd, one_t, zero_t, zero_t, tm=lc, tk=lc)
    w_out = a_w_out[0].astype(BF16)
    x1 = _attn_a(q, k, vt, kc, vct, gate, x, mx, w_out, tq=TQ_ATTN_A)
    ctx1 = _attn_a(qc, None, None, kc, vct, gate_c, ctx, mc, w_out, tq=lc)

    mx, mc = mods[1, :bsz], mods[1, bsz:bsz + 1]
    ng = norm_g[1].reshape(1, d)
    w_in = b_w_in[0].astype(BF16)
    q, k, vt, gate = _inproj_b(x1, ng, mx, w_in, tm=TM_INPROJ_B, kv_only=False)
    kc, vct = _inproj_b(ctx1, ng, mc, w_in[:, B_WIDTH:3 * B_WIDTH], tm=lc, kv_only=True)
    u = _natten(q, k, vt, kc, vct, _natten_tables(b_rpb[0], t // GRID_W), gate)
    return _outproj_final(u, x1, mx, b_w_out[0].astype(BF16), final_norm_g.reshape(1, d), tm=TM_OUTPROJ)
```
